```python
import math
import jax, jax.numpy as jnp
from jax import lax
import numpy as np

D_MODEL = 2048
BATCH = 16
SEQ = 2048
DEPTH = 4

GRID_W = 64
CTX_LEN = 256
EPS = 1e-6

D_MIX = D_MODEL
N_GROUPS = 4
GROUP_W = D_MIX // N_GROUPS
S5_WIDTH = GROUP_W
S5_CH_PER_GROUP = 16
S5_GROUPS = S5_WIDTH // S5_CH_PER_GROUP
S5_STATE = 64
S5_MIN_NEG = -1e-4
ML_HEADS = 4
ML_HEAD_DIM = GROUP_W // ML_HEADS
MLSTM_CHUNK = 64
MLA_HEADS = 4
MLA_NOPE = 128
MLA_ROPE = 64
MLA_QK = MLA_NOPE + MLA_ROPE
MLA_V = GROUP_W // MLA_HEADS
MLA_Q_LORA = 384
MLA_KV_LORA = 128
ATTN_SCALE = MLA_QK ** -0.5
ROPE_BASE = 10000.0
Q_BLOCK = 128
LRU_WIDTH = GROUP_W
LRU_BLOCKS = 4
LRU_BLOCK_W = LRU_WIDTH // LRU_BLOCKS
LRU_CONV = 4
LRU_C = 8.0
D_FF = 4 * D_MODEL
IN_SPLITS = (S5_WIDTH, GROUP_W, GROUP_W, GROUP_W, GROUP_W, 4 * ML_HEADS,
             MLA_Q_LORA, MLA_KV_LORA, MLA_ROPE, LRU_WIDTH, LRU_WIDTH)
IN_COLS = sum(IN_SPLITS)

kernel_name = 'hybrid_parallel_group_flow_trunk'


def rmsnorm(x, w):
    xf = x.astype(jnp.float32)
    y = xf * lax.rsqrt(jnp.mean(xf * xf, axis=-1, keepdims=True) + EPS)
    return (y * w.astype(jnp.float32)).astype(x.dtype)


def modulate(h, shift, scale):
    return h * (1.0 + scale) + shift


def split_cols(z):
    idx = np.cumsum(np.array(IN_SPLITS))[:-1].tolist()
    return jnp.split(z, idx, axis=-1)


def last_state(h, reverse):
    return h[:, 0] if reverse else h[:, -1]


def _real_combine(e1, e2):
    a1, b1 = e1
    a2, b2 = e2
    return a1 * a2, a2 * b1 + b2


def linear_scan(a, b, h0, reverse):
    if reverse:
        a, b = jnp.flip(a, 1), jnp.flip(b, 1)
    if h0 is not None:
        b = b.at[:, 0].add(a[:, 0] * h0)
    _, h = lax.associative_scan(_real_combine, (a, b), axis=1)
    return jnp.flip(h, 1) if reverse else h


def _complex_combine(e1, e2):
    a1r, a1i, b1r, b1i = e1
    a2r, a2i, b2r, b2i = e2
    return (a1r * a2r - a1i * a2i, a1r * a2i + a1i * a2r,
            a2r * b1r - a2i * b1i + b2r, a2r * b1i + a2i * b1r + b2i)


def complex_scan(ar, ai, br, bi, h0, reverse):
    ar = jnp.broadcast_to(ar, br.shape)
    ai = jnp.broadcast_to(ai, br.shape)
    if reverse:
        br, bi = jnp.flip(br, 1), jnp.flip(bi, 1)
    if h0 is not None:
        h0r, h0i = h0
        br = br.at[:, 0].add(ar[:, 0] * h0r - ai[:, 0] * h0i)
        bi = bi.at[:, 0].add(ar[:, 0] * h0i + ai[:, 0] * h0r)
    _, _, hr, hi = lax.associative_scan(_complex_combine, (ar, ai, br, bi), axis=1)
    if reverse:
        hr, hi = jnp.flip(hr, 1), jnp.flip(hi, 1)
    return hr, hi


def s5_mixer(u_l, u_c, lam_re, lam_im, log_dt, b_re, b_im, c_re, c_im, d_skip, glu_w, glu_b, need_ctx):
    def groups(u):
        return u.astype(jnp.float32).reshape(u.shape[0], u.shape[1], S5_GROUPS, S5_CH_PER_GROUP)
    gl, gc = groups(u_l), groups(u_c)
    y_l, y_c = 0.0, 0.0
    for d in range(2):
        rev = d == 1
        lr = jnp.minimum(lam_re[d].astype(jnp.float32), S5_MIN_NEG)
        li = lam_im[d].astype(jnp.float32)
        dt = jnp.exp(log_dt[d].astype(jnp.float32))[:, None]
        mag = jnp.exp(lr * dt)
        ar, ai = mag * jnp.cos(li * dt), mag * jnp.sin(li * dt)
        den = lr * lr + li * li
        fr = ((ar - 1.0) * lr + ai * li) / den
        fi = (ai * lr - (ar - 1.0) * li) / den
        bbr = fr[..., None] * b_re[d] - fi[..., None] * b_im[d]
        bbi = fr[..., None] * b_im[d] + fi[..., None] * b_re[d]

        def drive(u):
            return (jnp.einsum('blgc,gpc->blgp', u, bbr), jnp.einsum('blgc,gpc->blgp', u, bbi))

        def readout(sr, si):
            return jnp.einsum('blgp,gcp->blgc', sr, c_re[d]) - jnp.einsum('blgp,gcp->blgc', si, c_im[d])

        sc_r, sc_i = complex_scan(ar, ai, *drive(gc), None, rev)
        sl_r, sl_i = complex_scan(ar, ai, *drive(gl), (last_state(sc_r, rev), last_state(sc_i, rev)), rev)
        y_l = y_l + readout(sl_r, sl_i)
        if need_ctx:
            y_c = y_c + readout(sc_r, sc_i)

    def finish(y, u):
        y = y.reshape(u.shape[0], u.shape[1], S5_WIDTH) + d_skip * u.astype(jnp.float32)
        y = jax.nn.gelu(y)
        return y * jax.nn.sigmoid(y @ glu_w + glu_b)

    return finish(y_l, u_l), (finish(y_c, u_c) if need_ctx else None)


def mlstm_scan(q, k, v, logi, logf, state0, with_output):
    B_, H_, L_, dh = q.shape
    nc = L_ // MLSTM_CHUNK
    lower = jnp.tril(jnp.ones((MLSTM_CHUNK, MLSTM_CHUNK), dtype=bool))

    def chunks(t):
        return jnp.moveaxis(t.reshape((B_, H_, nc, MLSTM_CHUNK) + t.shape[3:]), 2, 0)

    def step(carry, xs):
        C, n, m = carry
        qc, kc, vc, li, lf = xs
        b = jnp.cumsum(lf, axis=-1)
        b_last = b[..., -1]
        w_log = b_last[..., None] - b + li
        m_new = jnp.maximum(b_last + m, jnp.max(w_log, axis=-1))
        decay = jnp.exp(b_last + m - m_new)
        w = jnp.exp(w_log - m_new[..., None])
        C_new = decay[..., None, None] * C + jnp.einsum('bhs,bhsd,bhse->bhde', w, vc, kc)
        n_new = decay[..., None] * n + jnp.einsum('bhs,bhse->bhe', w, kc)
        if not with_output:
            return (C_new, n_new, m_new), None
        g = b + m[..., None]
        d_log = jnp.where(lower, b[..., :, None] - b[..., None, :] + li[..., None, :], -jnp.inf)
        m_t = jnp.maximum(g, jnp.max(d_log, axis=-1))
        inter = jnp.exp(g - m_t)
        s = jnp.einsum('bhtd,bhsd->bhts', qc, kc) * jnp.exp(d_log - m_t[..., None])
        num = inter[..., None] * jnp.einsum('bhde,bhte->bhtd', C, qc) + jnp.einsum('bhts,bhsd->bhtd', s, vc)
        den = inter * jnp.einsum('bhe,bhte->bht', n, qc) + jnp.sum(s, axis=-1)
        h = num / jnp.maximum(jnp.abs(den), jnp.exp(-m_t))[..., None]
        return (C_new, n_new, m_new), h

    state, h = lax.scan(step, state0, tuple(chunks(t) for t in (q, k, v, logi, logf)))
    if with_output:
        h = jnp.moveaxis(h, 0, 2).reshape(B_, H_, L_, dh)
    return h, state


def mlstm_mixer(q_l, k_l, v_l, o_l, g_l, q_c, k_c, v_c, o_c, g_c, ig_bias, fg_bias, out_norm, need_ctx):
    def heads(t):
        return jnp.transpose(t.astype(jnp.float32).reshape(t.shape[0], t.shape[1], ML_HEADS, ML_HEAD_DIM), (0, 2, 1, 3))

    def gates(g, d):
        g = g.astype(jnp.float32).reshape(g.shape[0], g.shape[1], 2, 2, ML_HEADS)
        logi = g[:, :, d, 0] + ig_bias[d]
        logf = jax.nn.log_sigmoid(g[:, :, d, 1] + fg_bias[d])
        return (jnp.transpose(logi, (0, 2, 1)), jnp.transpose(logf, (0, 2, 1)))

    k_scale = ML_HEAD_DIM ** -0.5
    lat = (heads(q_l), heads(k_l) * k_scale, heads(v_l))
    ctx = (heads(q_c), heads(k_c) * k_scale, heads(v_c))
    B_ = q_l.shape[0]
    zero = (jnp.zeros((B_, ML_HEADS, ML_HEAD_DIM, ML_HEAD_DIM), jnp.float32),
            jnp.zeros((B_, ML_HEADS, ML_HEAD_DIM), jnp.float32),
            jnp.zeros((B_, ML_HEADS), jnp.float32))
    h_l, h_c = 0.0, 0.0
    for d in range(2):
        flip = (lambda t: jnp.flip(t, axis=2)) if d == 1 else (lambda t: t)
        seq_c = [flip(t) for t in ctx + gates(g_c, d)]
        seq_l = [flip(t) for t in lat + gates(g_l, d)]
        out_c, state = mlstm_scan(*seq_c, zero, need_ctx)
        out_l, _ = mlstm_scan(*seq_l, state, True)
        h_l = h_l + flip(out_l)
        if need_ctx:
            h_c = h_c + flip(out_c)

    def finish(h, o):
        hn = rmsnorm(h, out_norm[:, None, :])
        hn = jnp.transpose(hn, (0, 2, 1, 3)).reshape(o.shape[0], o.shape[1], GROUP_W)
        return hn * jax.nn.sigmoid(o.astype(jnp.float32))

    return finish(h_l, o_l), (finish(h_c, o_c) if need_ctx else None)


def axial_angles(n_tokens):
    n_rows = n_tokens // GRID_W
    rows = jnp.repeat(jnp.arange(n_rows, dtype=jnp.float32), GRID_W)
    cols = jnp.tile(jnp.arange(GRID_W, dtype=jnp.float32), n_rows)
    n_freq = MLA_ROPE // 4
    inv_freq = ROPE_BASE ** (-jnp.arange(n_freq, dtype=jnp.float32) / n_freq)
    return rows[:, None] * inv_freq, cols[:, None] * inv_freq


def rotate_pairs(x, ang):
    f = ang.shape[-1]
    cos, sin = jnp.cos(ang)[None, :, None, :], jnp.sin(ang)[None, :, None, :]
    x1, x2 = x[..., :f].astype(jnp.float32), x[..., f:].astype(jnp.float32)
    return jnp.concatenate([x1 * cos - x2 * sin, x2 * cos + x1 * sin], axis=-1)


def rope_2d(x, ang_row, ang_col):
    half = MLA_ROPE // 2
    rope = x[..., MLA_NOPE:]
    rot = jnp.concatenate([rotate_pairs(rope[..., :half], ang_row), rotate_pairs(rope[..., half:], ang_col)], axis=-1)
    return jnp.concatenate([x[..., :MLA_NOPE], rot.astype(x.dtype)], axis=-1)


def mla_queries(cq, q_a_norm, w_q_up, q_norm, angles):
    q = (rmsnorm(cq, q_a_norm) @ w_q_up).reshape(cq.shape[0], cq.shape[1], MLA_HEADS, MLA_QK)
    q = rmsnorm(q, q_norm)
    return q if angles is None else rope_2d(q, *angles)


def mla_keys_values(ckv, kr, kv_a_norm, w_kv_up, k_norm, angles):
    B_, L_ = ckv.shape[:2]
    kv = (rmsnorm(ckv, kv_a_norm) @ w_kv_up).reshape(B_, L_, MLA_HEADS, MLA_NOPE + MLA_V)
    k_rope = jnp.broadcast_to(kr[:, :, None, :], (B_, L_, MLA_HEADS, MLA_ROPE))
    k = rmsnorm(jnp.concatenate([kv[..., :MLA_NOPE], k_rope], axis=-1), k_norm)
    k = k if angles is None else rope_2d(k, *angles)
    return k, kv[..., MLA_NOPE:]


def attend(q, k, v):
    s = jnp.einsum('bqhd,bkhd->bhqk', q, k).astype(jnp.float32) * ATTN_SCALE
    p = jax.nn.softmax(s, axis=-1)
    return jnp.einsum('bhqk,bkhd->bqhd', p.astype(v.dtype), v)


def mla_mixer(cq_l, ckv_l, kr_l, cq_c, ckv_c, kr_c, q_a_norm, w_q_up, kv_a_norm, w_kv_up, q_norm, k_norm, need_ctx):
    B_, L_ = cq_l.shape[:2]
    ang = axial_angles(L_)
    q_l = mla_queries(cq_l, q_a_norm, w_q_up, q_norm, ang)
    k_l, v_l = mla_keys_values(ckv_l, kr_l, kv_a_norm, w_kv_up, k_norm, ang)
    k_c, v_c = mla_keys_values(ckv_c, kr_c, kv_a_norm, w_kv_up, k_norm, None)
    k_all = jnp.concatenate([k_c, k_l], axis=1)
    v_all = jnp.concatenate([v_c, v_l], axis=1)
    n_blocks = L_ // Q_BLOCK
    q_blocks = jnp.moveaxis(q_l.reshape(B_, n_blocks, Q_BLOCK, MLA_HEADS, MLA_QK), 1, 0)
    o_blocks = lax.map(lambda qb: attend(qb, k_all, v_all), q_blocks)
    y_l = jnp.moveaxis(o_blocks, 0, 1).reshape(B_, L_, MLA_HEADS * MLA_V)
    if not need_ctx:
        return y_l, None
    q_c = mla_queries(cq_c, q_a_norm, w_q_up, q_norm, None)
    y_c = attend(q_c, k_c, v_c).reshape(B_, cq_c.shape[1], MLA_HEADS * MLA_V)
    return y_l, y_c


def conv_centred(x, w, b):
    L_ = x.shape[1]
    left = LRU_CONV // 2
    xp = jnp.pad(x, ((0, 0), (left, LRU_CONV - 1 - left), (0, 0)))
    y = b
    for j in range(LRU_CONV):
        y = y + xp[:, j:j + L_] * w[j]
    return y


def rglru_mixer(x_l, gate_l, x_c, gate_c, conv_w, conv_b, wa, ba, wx, bx, lam, need_ctx):
    xs_l = conv_centred(x_l, conv_w, conv_b).astype(jnp.float32)
    xs_c = conv_centred(x_c, conv_w, conv_b).astype(jnp.float32)

    def drive(xs, d):
        B_, L_ = xs.shape[:2]
        xb = xs.reshape(B_, L_, LRU_BLOCKS, LRU_BLOCK_W)
        r = jax.nn.sigmoid(jnp.einsum('blnc,ncd->blnd', xb, wa[d]).reshape(B_, L_, LRU_WIDTH) + ba[d])
        i = jax.nn.sigmoid(jnp.einsum('blnc,ncd->blnd', xb, wx[d]).reshape(B_, L_, LRU_WIDTH) + bx[d])
        log_a = -LRU_C * r * jax.nn.softplus(-lam[d].astype(jnp.float32))
        return jnp.exp(log_a), jnp.sqrt(-jnp.expm1(2.0 * log_a)) * (i * xs)

    y_l, y_c = 0.0, 0.0
    for d in range(2):
        rev = d == 1
        h_c = linear_scan(*drive(xs_c, d), None, rev)
        h_l = linear_scan(*drive(xs_l, d), last_state(h_c, rev), rev)
        y_l = y_l + h_l
        if need_ctx:
            y_c = y_c + h_c
    out_l = y_l * jax.nn.gelu(gate_l.astype(jnp.float32))
    out_c = y_c * jax.nn.gelu(gate_c.astype(jnp.float32)) if need_ctx else None
    return out_l, out_c


def sq_relu_mlp(h, w1, w2):
    a = jax.nn.relu(h @ w1)
    return (a * a) @ w2


def setup_inputs(seed: int = 0) -> dict:
    key = jax.random.key(seed)
    ks = iter(jax.random.split(key, 48))

    def nrm(shape, scale):
        return jax.random.normal(next(ks), shape, jnp.float32) * scale

    def gain(shape):
        return 1.0 + nrm(shape, 0.02)

    L2 = (DEPTH, 2)
    lru_u = jax.random.uniform(next(ks), L2 + (LRU_WIDTH,), jnp.float32, 0.9, 0.999)
    lru_a = lru_u ** (1.0 / LRU_C)
    return {
        'x': nrm((BATCH, SEQ, D_MODEL), 1.0),
        'c': nrm((BATCH, D_MODEL), 1.0),
        'ctx': nrm((BATCH, CTX_LEN, D_MODEL), 1.0),
        'c_ctx': nrm((D_MODEL,), 1.0),
        'ada_w': nrm((DEPTH, D_MODEL, 6 * D_MODEL), 0.5 * D_MODEL ** -0.5),
        'ada_b': nrm((DEPTH, 6 * D_MODEL), 0.02),
        'norm1_w': gain((DEPTH, D_MODEL)),
        'norm2_w': gain((DEPTH, D_MODEL)),
        'w_in': nrm((DEPTH, D_MODEL, IN_COLS), D_MODEL ** -0.5),
        'w_out': nrm((DEPTH, D_MIX, D_MODEL), D_MIX ** -0.5),
        's5_lam_re': -0.5 + nrm(L2 + (S5_GROUPS, S5_STATE), 0.01),
        's5_lam_im': jnp.pi * jnp.arange(S5_STATE, dtype=jnp.float32) + nrm(L2 + (S5_GROUPS, S5_STATE), 0.01),
        's5_log_dt': jax.random.uniform(next(ks), L2 + (S5_GROUPS,), jnp.float32, math.log(1e-3), math.log(1e-1)),
        's5_b_re': nrm(L2 + (S5_GROUPS, S5_STATE, S5_CH_PER_GROUP), (2 * S5_CH_PER_GROUP) ** -0.5),
        's5_b_im': nrm(L2 + (S5_GROUPS, S5_STATE, S5_CH_PER_GROUP), (2 * S5_CH_PER_GROUP) ** -0.5),
        's5_c_re': nrm(L2 + (S5_GROUPS, S5_CH_PER_GROUP, S5_STATE), (2 * S5_STATE) ** -0.5),
        's5_c_im': nrm(L2 + (S5_GROUPS, S5_CH_PER_GROUP, S5_STATE), (2 * S5_STATE) ** -0.5),
        's5_d': nrm((DEPTH, S5_WIDTH), 1.0),
        's5_glu_w': nrm((DEPTH, S5_WIDTH, S5_WIDTH), S5_WIDTH ** -0.5),
        's5_glu_b': nrm((DEPTH, S5_WIDTH), 0.02),
        'ml_ig_bias': nrm(L2 + (ML_HEADS,), 0.1),
        'ml_fg_bias': jnp.linspace(3.0, 6.0, ML_HEADS) + nrm(L2 + (ML_HEADS,), 0.1),
        'ml_out_norm': gain((DEPTH, ML_HEADS, ML_HEAD_DIM)),
        'mla_q_a_norm': gain((DEPTH, MLA_Q_LORA)),
        'mla_w_q_up': nrm((DEPTH, MLA_Q_LORA, MLA_HEADS * MLA_QK), MLA_Q_LORA ** -0.5),
        'mla_kv_a_norm': gain((DEPTH, MLA_KV_LORA)),
        'mla_w_kv_up': nrm((DEPTH, MLA_KV_LORA, MLA_HEADS * (MLA_NOPE + MLA_V)), MLA_KV_LORA ** -0.5),
        'mla_q_norm': gain((DEPTH, MLA_QK)),
        'mla_k_norm': gain((DEPTH, MLA_QK)),
        'lru_conv_w': nrm((DEPTH, LRU_CONV, LRU_WIDTH), 0.5),
        'lru_conv_b': nrm((DEPTH, LRU_WIDTH), 0.02),
        'lru_wa': nrm(L2 + (LRU_BLOCKS, LRU_BLOCK_W, LRU_BLOCK_W), LRU_BLOCK_W ** -0.5),
        'lru_ba': nrm(L2 + (LRU_WIDTH,), 0.02),
        'lru_wx': nrm(L2 + (LRU_BLOCKS, LRU_BLOCK_W, LRU_BLOCK_W), LRU_BLOCK_W ** -0.5),
        'lru_bx': nrm(L2 + (LRU_WIDTH,), 0.02),
        'lru_lam': jnp.log(lru_a) - jnp.log1p(-lru_a),
        'mlp_w1': nrm((DEPTH, D_MODEL, D_FF), D_MODEL ** -0.5),
        'mlp_w2': nrm((DEPTH, D_FF, D_MODEL), D_FF ** -0.5),
    }


def reference(x, c, ctx, c_ctx, ada_w, ada_b, norm1_w, norm2_w, w_in, w_out,
              s5_lam_re, s5_lam_im, s5_log_dt, s5_b_re, s5_b_im, s5_c_re, s5_c_im, s5_d, s5_glu_w, s5_glu_b,
              ml_ig_bias, ml_fg_bias, ml_out_norm,
              mla_q_a_norm, mla_w_q_up, mla_kv_a_norm, mla_w_kv_up, mla_q_norm, mla_k_norm,
              lru_conv_w, lru_conv_b, lru_wa, lru_ba, lru_wx, lru_bx, lru_lam,
              mlp_w1, mlp_w2):
    x_lat, x_ctx = x, ctx
    c_act = jax.nn.silu(c.astype(jnp.float32))
    cctx_act = jax.nn.silu(c_ctx.astype(jnp.float32))
    for l in range(DEPTH):
        need_ctx = l < DEPTH - 1
        mod_l = jnp.split((c_act @ ada_w[l] + ada_b[l])[:, None, :], 6, axis=-1)
        mod_c = jnp.split(cctx_act @ ada_w[l] + ada_b[l], 6, axis=-1)

        h_l = modulate(rmsnorm(x_lat, norm1_w[l]), mod_l[0], mod_l[1])
        h_c = modulate(rmsnorm(x_ctx, norm1_w[l]), mod_c[0], mod_c[1])
        (u_l, mq_l, mk_l, mv_l, mo_l, mg_l, cq_l, ckv_l, kr_l, lx_l, lg_l) = split_cols(h_l @ w_in[l])
        (u_c, mq_c, mk_c, mv_c, mo_c, mg_c, cq_c, ckv_c, kr_c, lx_c, lg_c) = split_cols(h_c @ w_in[l])

        a_l, a_c = s5_mixer(u_l, u_c, s5_lam_re[l], s5_lam_im[l], s5_log_dt[l], s5_b_re[l], s5_b_im[l],
                            s5_c_re[l], s5_c_im[l], s5_d[l], s5_glu_w[l], s5_glu_b[l], need_ctx)
        b_l, b_c = mlstm_mixer(mq_l, mk_l, mv_l, mo_l, mg_l, mq_c, mk_c, mv_c, mo_c, mg_c,
                               ml_ig_bias[l], ml_fg_bias[l], ml_out_norm[l], need_ctx)
        m_l, m_c = mla_mixer(cq_l, ckv_l, kr_l, cq_c, ckv_c, kr_c, mla_q_a_norm[l], mla_w_q_up[l],
                             mla_kv_a_norm[l], mla_w_kv_up[l], mla_q_norm[l], mla_k_norm[l], need_ctx)
        r_l, r_c = rglru_mixer(lx_l, lg_l, lx_c, lg_c, lru_conv_w[l], lru_conv_b[l], lru_wa[l], lru_ba[l],
                               lru_wx[l], lru_bx[l], lru_lam[l], need_ctx)
        x_lat = x_lat + mod_l[2] * (jnp.concatenate([a_l, b_l, m_l, r_l], axis=-1) @ w_out[l])
        if need_ctx:
            x_ctx = x_ctx + mod_c[2] * (jnp.concatenate([a_c, b_c, m_c, r_c], axis=-1) @ w_out[l])

        x_lat = x_lat + mod_l[5] * sq_relu_mlp(modulate(rmsnorm(x_lat, norm2_w[l]), mod_l[3], mod_l[4]), mlp_w1[l], mlp_w2[l])
        if need_ctx:
            x_ctx = x_ctx + mod_c[5] * sq_relu_mlp(modulate(rmsnorm(x_ctx, norm2_w[l]), mod_c[3], mod_c[4]), mlp_w1[l], mlp_w2[l])
    return x_lat
```

```python
import functools

import jax
import jax.numpy as jnp
from jax import lax
from jax.experimental import pallas as pl
from jax.experimental.pallas import tpu as pltpu

F32 = jnp.float32
BF16 = jnp.bfloat16
EPS = 1e-6
HIGHEST = lax.Precision.HIGHEST

GROUP_W = 512
TOKEN_TILE = 256
S5_STATE = 64
S5_CH = 16
S5_STRIPS = 4
S5_STRIP_STATES = 512
S5_MIN_NEG = -1e-4
S5_CHUNK = 64
ML_HEADS = 4
ML_DH = 128
ML_CHUNK = 128
MLA_HEADS = 4
MLA_NOPE = 128
MLA_ROPE = 64
MLA_QK = MLA_NOPE + MLA_ROPE
MLA_V = 128
MLA_Q_LORA = 384
MLA_KV_LORA = 128
MLA_PAD = 256
ATTN_SCALE = MLA_QK ** -0.5
ROPE_BASE = 10000.0
GRID_W = 64
Q_TILE = 256
LRU_BLOCKS = 4
LRU_BLOCK_W = 128
LRU_CONV = 4
LRU_C = 8.0
MLP_FF_TILE = 1024

COL_U, COL_MQ, COL_MK, COL_MV, COL_MO = 0, 512, 1024, 1536, 2048
COL_CQ, COL_CKV, COL_LX, COL_LG, COL_MISC = 2560, 2944, 3072, 3584, 4096
IN_COLS_PACKED = 4224
MISC_W = 128


def _cparams(sem, vmem_mb):
    return pltpu.CompilerParams(dimension_semantics=sem, vmem_limit_bytes=vmem_mb << 20)


def _norm_mod(x, nw, shift, scale):
    ms = jnp.mean(x * x, axis=-1, keepdims=True)
    y = (x * lax.rsqrt(ms + EPS)) * nw
    return (y * (1.0 + scale) + shift).astype(BF16)


def _log_sigmoid(x):
    return jnp.minimum(x, 0.0) - jnp.log1p(jnp.exp(-jnp.abs(x)))


def _softplus(x):
    return jnp.maximum(x, 0.0) + jnp.log1p(jnp.exp(-jnp.abs(x)))


def _ada_kernel(c_ref, w_ref, b_ref, o_ref):
    c = c_ref[...]
    act = (c * jax.nn.sigmoid(c)).astype(BF16)
    o_ref[...] = jnp.dot(act, w_ref[...].astype(BF16), preferred_element_type=F32) + b_ref[...]


def _ada_all(c, c_ctx, ada_w, ada_b):
    depth, d, n6 = ada_w.shape
    nb = c.shape[0]
    rows = nb + 8
    cc = jnp.concatenate([c.astype(F32), jnp.broadcast_to(c_ctx.astype(F32)[None], (8, d))], axis=0)
    tn = 1024
    out = pl.pallas_call(
        _ada_kernel,
        grid=(depth, n6 // tn),
        in_specs=[pl.BlockSpec((rows, d), lambda l, n: (0, 0)),
                  pl.BlockSpec((None, d, tn), lambda l, n: (l, 0, n)),
                  pl.BlockSpec((None, 1, tn), lambda l, n: (l, 0, n))],
        out_specs=pl.BlockSpec((None, rows, tn), lambda l, n: (l, 0, n)),
        out_shape=jax.ShapeDtypeStruct((depth, rows, n6), F32),
        compiler_params=_cparams(("parallel", "parallel"), 40),
        name="ada_mod",
    )(cc, ada_w, ada_b.reshape(depth, 1, n6))
    return out.reshape(depth, rows, 6, d)


def _win_kernel(x_ref, mod_ref, nw_ref, w_ref, wgt_ref,
                u_ref, mq_ref, mk_ref, mv_ref, mo_ref, cq_ref, ckv_ref, lx_ref, lg_ref, misc_ref, mgt_ref):
    h = _norm_mod(x_ref[...], nw_ref[...], mod_ref[0:1, :], mod_ref[1:2, :])

    def mm(lo, width):
        return jnp.dot(h, w_ref[:, lo:lo + width], preferred_element_type=F32)

    u_ref[...] = mm(COL_U, GROUP_W)
    mq_ref[...] = mm(COL_MQ, GROUP_W).astype(BF16)
    mk_ref[...] = (mm(COL_MK, GROUP_W) * (ML_DH ** -0.5)).astype(BF16)
    mv_ref[...] = mm(COL_MV, GROUP_W).astype(BF16)
    mo_ref[...] = mm(COL_MO, GROUP_W)
    cq_ref[...] = mm(COL_CQ, MLA_Q_LORA)
    ckv_ref[...] = mm(COL_CKV, MLA_KV_LORA)
    lx_ref[...] = mm(COL_LX, GROUP_W)
    lg_ref[...] = mm(COL_LG, GROUP_W)
    misc_ref[...] = mm(COL_MISC, MISC_W)
    mgt_ref[...] = lax.dot_general(wgt_ref[...], h, (((1,), (1,)), ((), ())), preferred_element_type=F32)


def _win(X, mod_l, nw, w_p, wgt, n_lat_tiles):
    B, Lt, D = X.shape
    tm = TOKEN_TILE
    nt = Lt // tm
    bm = lambda b, t: (b, t, 0)
    tmaj = lambda b, t: (t, b)
    in_specs = [
        pl.BlockSpec((None, tm, D), bm),
        pl.BlockSpec((None, 6, D), lambda b, t: (jnp.where(t >= n_lat_tiles, B, b), 0, 0)),
        pl.BlockSpec((1, D), lambda b, t: (0, 0)),
        pl.BlockSpec((D, IN_COLS_PACKED), lambda b, t: (0, 0), pipeline_mode=pl.Buffered(1)),
        pl.BlockSpec((16, D), lambda b, t: (0, 0)),
    ]
    out_specs = [
        pl.BlockSpec((tm, GROUP_W), tmaj),
        pl.BlockSpec((None, tm, GROUP_W), bm),
        pl.BlockSpec((None, tm, GROUP_W), bm),
        pl.BlockSpec((None, tm, GROUP_W), bm),
        pl.BlockSpec((None, tm, GROUP_W), bm),
        pl.BlockSpec((None, tm, MLA_Q_LORA), bm),
        pl.BlockSpec((None, tm, MLA_KV_LORA), bm),
        pl.BlockSpec((tm, GROUP_W), tmaj),
        pl.BlockSpec((tm, GROUP_W), tmaj),
        pl.BlockSpec((None, tm, MISC_W), bm),
        pl.BlockSpec((None, 16, tm), lambda b, t: (b, 0, t)),
    ]
    sds = jax.ShapeDtypeStruct
    out_shape = [
        sds((Lt, B * GROUP_W), F32),
        sds((B, Lt, GROUP_W), BF16), sds((B, Lt, GROUP_W), BF16), sds((B, Lt, GROUP_W), BF16),
        sds((B, Lt, GROUP_W), F32),
        sds((B, Lt, MLA_Q_LORA), F32), sds((B, Lt, MLA_KV_LORA), F32),
        sds((Lt, B * GROUP_W), F32), sds((Lt, B * GROUP_W), F32),
        sds((B, Lt, MISC_W), F32),
        sds((B, 16, Lt), F32),
    ]
    return pl.pallas_call(
        _win_kernel, grid=(B, nt), in_specs=in_specs, out_specs=out_specs, out_shape=out_shape,
        compiler_params=_cparams(("parallel", "parallel"), 52), name="in_proj",
    )(X, mod_l, nw.reshape(1, D), w_p, wgt)


def _s5_kernel(*refs, tc, nb, reverse, final):
    if final:
        u_ref, yp_ref, wd_ref, a_ref, wc_ref, dsk_ref, gw_ref, gb_ref, o_ref, buf, ybuf, st = refs
    else:
        u_ref, wd_ref, a_ref, wc_ref, o_ref, buf, st = refs
    R = tc * nb
    RB = min(R, 512)
    NS = S5_STRIP_STATES
    nh = nb // 8

    @pl.when(pl.program_id(0) == 0)
    def _():
        st[...] = jnp.zeros_like(st)

    for s in range(S5_STRIPS):
        cs = slice(s * 128, (s + 1) * 128)
        for rb in range(R // RB):
            rs = slice(rb * RB, (rb + 1) * RB)
            buf[rs, :] = jnp.dot(u_ref[rs, cs].astype(BF16), wd_ref[s], preferred_element_type=F32)
        ar = jnp.broadcast_to(a_ref[s, 0:1, :], (8, NS))
        ai = jnp.broadcast_to(a_ref[s, 1:2, :], (8, NS))
        init = []
        for hh in range(nh):
            init += [st[s, hh * 8:(hh + 1) * 8, 0:NS], st[s, hh * 8:(hh + 1) * 8, NS:2 * NS]]

        def body(k, carry, ar=ar, ai=ai):
            t = (tc - 1 - k) if reverse else k
            out = []
            for hh in range(nh):
                sr, si = carry[2 * hh], carry[2 * hh + 1]
                row = pl.multiple_of(t * nb + hh * 8, 8)
                br = buf[pl.ds(row, 8), 0:NS]
                bi = buf[pl.ds(row, 8), NS:2 * NS]
                nsr = ar * sr - ai * si + br
                nsi = ar * si + ai * sr + bi
                buf[pl.ds(row, 8), 0:NS] = nsr
                buf[pl.ds(row, 8), NS:2 * NS] = nsi
                out += [nsr, nsi]
            return tuple(out)

        fin = lax.fori_loop(0, tc, body, tuple(init), unroll=2)
        for hh in range(nh):
            st[s, hh * 8:(hh + 1) * 8, 0:NS] = fin[2 * hh]
            st[s, hh * 8:(hh + 1) * 8, NS:2 * NS] = fin[2 * hh + 1]
        for rb in range(R // RB):
            rs = slice(rb * RB, (rb + 1) * RB)
            y = jnp.dot(buf[rs, :].astype(BF16), wc_ref[s], preferred_element_type=F32)
            if final:
                ybuf[rs, cs] = y
            else:
                o_ref[rs, cs] = y

    if final:
        for rb in range(R // RB):
            rs = slice(rb * RB, (rb + 1) * RB)
            y = ybuf[rs, :] + yp_ref[rs, :] + dsk_ref[...] * u_ref[rs, :]
            g = jax.nn.gelu(y)
            z = jnp.dot(g.astype(BF16), gw_ref[...], preferred_element_type=F32) + gb_ref[...]
            o_ref[rs, :] = (g * jax.nn.sigmoid(z)).astype(BF16)


def _chunk_order(nc, nlc, reverse):
    if reverse:
        return lambda i: nc - 1 - i
    return lambda i: lax.rem(i + nlc, nc)


def _s5(u2, p, d_skip, glu_w, glu_b, nb, n_lat):
    rows = u2.shape[0]
    lt = rows // nb
    tc = S5_CHUNK
    R = tc * nb
    nc, nlc = lt // tc, n_lat // tc
    wd, a, wc = p
    outs = None
    for d in range(2):
        final = d == 1
        cidx = _chunk_order(nc, nlc, reverse=final)
        row_spec = pl.BlockSpec((R, GROUP_W), lambda i: (cidx(i), 0))
        full = lambda shape: pl.BlockSpec(shape, lambda i: (0,) * len(shape))
        in_specs = [row_spec] + ([row_spec] if final else []) + [
            full((S5_STRIPS, 128, 2 * S5_STRIP_STATES)), full((S5_STRIPS, 2, S5_STRIP_STATES)),
            full((S5_STRIPS, 2 * S5_STRIP_STATES, 128))]
        args = [u2] + ([outs] if final else []) + [wd[d], a[d], wc[d]]
        scratch = [pltpu.VMEM((R, 2 * S5_STRIP_STATES), F32)]
        if final:
            in_specs += [full((1, GROUP_W)), full((GROUP_W, GROUP_W)), full((1, GROUP_W))]
            args += [d_skip.reshape(1, GROUP_W), glu_w, glu_b.reshape(1, GROUP_W)]
            scratch += [pltpu.VMEM((R, GROUP_W), F32)]
        scratch += [pltpu.VMEM((S5_STRIPS, nb, 2 * S5_STRIP_STATES), F32)]
        outs = pl.pallas_call(
            functools.partial(_s5_kernel, tc=tc, nb=nb, reverse=final, final=final),
            grid=(nc,), in_specs=in_specs, out_specs=row_spec,
            out_shape=jax.ShapeDtypeStruct((rows, GROUP_W), BF16 if final else F32),
            scratch_shapes=scratch,
            compiler_params=_cparams(("arbitrary",), 48), name="s5_rev" if final else "s5_fwd",
        )(*args)
    return outs


def _s5_params(lam_re, lam_im, log_dt, b_re, b_im, c_re, c_im):
    lr = jnp.minimum(lam_re.astype(F32), S5_MIN_NEG)
    li = lam_im.astype(F32)
    dt = jnp.exp(log_dt.astype(F32))[..., None]
    mag = jnp.exp(lr * dt)
    ar, ai = mag * jnp.cos(li * dt), mag * jnp.sin(li * dt)
    den = lr * lr + li * li
    fr = ((ar - 1.0) * lr + ai * li) / den
    fi = (ai * lr - (ar - 1.0) * li) / den
    bbr = fr[..., None] * b_re - fi[..., None] * b_im
    bbi = fr[..., None] * b_im + fi[..., None] * b_re
    gs = 128 // S5_CH
    eye = jnp.eye(gs, dtype=F32)

    def drive(bb):
        t = bb.reshape(2, S5_STRIPS, gs, S5_STATE, S5_CH)
        return jnp.einsum('dsgpc,gh->dsgchp', t, eye).reshape(2, S5_STRIPS, 128, S5_STRIP_STATES)

    def read(cc):
        t = cc.reshape(2, S5_STRIPS, gs, S5_CH, S5_STATE)
        return jnp.einsum('dsgcp,gh->dsgphc', t, eye).reshape(2, S5_STRIPS, S5_STRIP_STATES, 128)

    wd = jnp.concatenate([drive(bbr), drive(bbi)], axis=-1).astype(BF16)
    wc = jnp.concatenate([read(c_re.astype(F32)), read(-c_im.astype(F32))], axis=-2).astype(BF16)
    a = jnp.stack([ar.reshape(2, S5_STRIPS, S5_STRIP_STATES), ai.reshape(2, S5_STRIPS, S5_STRIP_STATES)], axis=2)
    return wd, a, wc


def _lru_kernel(*refs, tc, nb, nlc, nc, reverse, final):
    if final:
        (x_ref, xp_ref, xn_ref, hp_ref, g_ref, cw_ref, cb_ref, w_ref, bias_ref, lam_ref,
         o_ref, xpad, abuf, bbuf, st) = refs
    else:
        x_ref, xp_ref, xn_ref, cw_ref, cb_ref, w_ref, bias_ref, lam_ref, o_ref, xpad, abuf, bbuf, st = refs
    i = pl.program_id(0)
    c = (nc - 1 - i) if reverse else lax.rem(i + nlc, nc)
    R = tc * nb
    RB = min(R, 256)
    nh = nb // 8
    left = (LRU_CONV // 2) * nb

    @pl.when(i == 0)
    def _():
        st[...] = jnp.zeros_like(st)

    starts = jnp.logical_or(c == 0, c == nlc)
    ends = jnp.logical_or(c == nlc - 1, c == nc - 1)
    xpad[0:left, :] = jnp.where(starts, 0.0, xp_ref[...])
    xpad[left:left + R, :] = x_ref[...]
    xpad[left + R:left + R + nb, :] = jnp.where(ends, 0.0, xn_ref[...])
    sp = _softplus(-lam_ref[...])

    for rb in range(R // RB):
        r0 = rb * RB
        xs = cb_ref[...]
        for j in range(LRU_CONV):
            xs = xs + xpad[r0 + j * nb:r0 + j * nb + RB, :] * cw_ref[j:j + 1, :]
        xb = xs.astype(BF16)
        for n in range(LRU_BLOCKS):
            cs = slice(n * LRU_BLOCK_W, (n + 1) * LRU_BLOCK_W)
            z = jnp.dot(xb[:, cs], w_ref[n], preferred_element_type=F32)
            r = jax.nn.sigmoid(z[:, :LRU_BLOCK_W] + bias_ref[0:1, cs])
            ig = jax.nn.sigmoid(z[:, LRU_BLOCK_W:] + bias_ref[1:2, cs])
            log_a = (-LRU_C * r) * sp[:, cs]
            abuf[r0:r0 + RB, cs] = jnp.exp(log_a)
            th = jnp.tanh(log_a)
            bbuf[r0:r0 + RB, cs] = jnp.sqrt(-2.0 * th / (1.0 - th)) * (ig * xs[:, cs])

    init = tuple(st[hh * 8:(hh + 1) * 8, :] for hh in range(nh))

    def body(k, carry):
        t = (tc - 1 - k) if reverse else k
        out = []
        for hh in range(nh):
            row = pl.multiple_of(t * nb + hh * 8, 8)
            hnew = abuf[pl.ds(row, 8), :] * carry[hh] + bbuf[pl.ds(row, 8), :]
            bbuf[pl.ds(row, 8), :] = hnew
            out.append(hnew)
        return tuple(out)

    fin = lax.fori_loop(0, tc, body, init, unroll=4)
    for hh in range(nh):
        st[hh * 8:(hh + 1) * 8, :] = fin[hh]
    if final:
        o_ref[...] = ((hp_ref[...] + bbuf[...]) * jax.nn.gelu(g_ref[...])).astype(BF16)
    else:
        o_ref[...] = bbuf[...]


def _lru(lx2, lg2, conv_w, conv_b, wab, bias, lam, nb, n_lat):
    rows = lx2.shape[0]
    lt = rows // nb
    tc = S5_CHUNK
    R = tc * nb
    nc, nlc = lt // tc, n_lat // tc
    outs = None
    for d in range(2):
        final = d == 1
        cidx = _chunk_order(nc, nlc, reverse=final)
        row_spec = pl.BlockSpec((R, GROUP_W), lambda i: (cidx(i), 0))
        prev_spec = pl.BlockSpec((2 * nb, GROUP_W), lambda i: (jnp.maximum(cidx(i) * (tc // 2) - 1, 0), 0))
        next_spec = pl.BlockSpec((nb, GROUP_W), lambda i: (jnp.minimum((cidx(i) + 1) * tc, lt - 1), 0))
        full = lambda shape: pl.BlockSpec(shape, lambda i: (0,) * len(shape))
        in_specs = [row_spec, prev_spec, next_spec] + ([row_spec, row_spec] if final else []) + [
            full((LRU_CONV, GROUP_W)), full((1, GROUP_W)), full((LRU_BLOCKS, LRU_BLOCK_W, 2 * LRU_BLOCK_W)),
            full((2, GROUP_W)), full((1, GROUP_W))]
        args = [lx2, lx2, lx2] + ([outs, lg2] if final else []) + [
            conv_w, conv_b.reshape(1, GROUP_W), wab[d], bias[d], lam[d].reshape(1, GROUP_W)]
        outs = pl.pallas_call(
            functools.partial(_lru_kernel, tc=tc, nb=nb, nlc=nlc, nc=nc, reverse=final, final=final),
            grid=(nc,), in_specs=in_specs, out_specs=row_spec,
            out_shape=jax.ShapeDtypeStruct((rows, GROUP_W), BF16 if final else F32),
            scratch_shapes=[pltpu.VMEM((R + 3 * nb, GROUP_W), F32), pltpu.VMEM((R, GROUP_W), F32),
                            pltpu.VMEM((R, GROUP_W), F32), pltpu.VMEM((nb, GROUP_W), F32)],
            compiler_params=_cparams(("arbitrary",), 48), name="lru_rev" if final else "lru_fwd",
        )(*args)
    return outs


def _mlstm_kernel(igb_ref, fgb_ref, q_ref, k_ref, v_ref, o_ref, gc_ref, gr_ref, nw_ref, out_ref,
                  hf_ref, hr_ref, *, nch, nlat):
    head = pl.program_id(1)
    T = ML_CHUNK
    rid = lax.broadcasted_iota(jnp.int32, (T, T), 0)
    cid = lax.broadcasted_iota(jnp.int32, (T, T), 1)

    def chunk_step(c, d, state):
        ct, n, m = state
        tri = (rid >= cid) if d == 0 else (rid <= cid)
        trf = tri.astype(F32)
        rows = pl.ds(pl.multiple_of(c * T, T), T)
        qc, kc, vc = q_ref[rows, :], k_ref[rows, :], v_ref[rows, :]
        gcol, grow = gc_ref[c], gr_ref[c]
        igb, fgb = igb_ref[d, head], fgb_ref[d, head]
        li_c = gcol[:, 2 * d:2 * d + 1] + igb
        lf_c = _log_sigmoid(gcol[:, 2 * d + 1:2 * d + 2] + fgb)
        li_r = grow[2 * d:2 * d + 1, :] + igb
        lf_r = _log_sigmoid(grow[2 * d + 1:2 * d + 2, :] + fgb)
        b_c = jnp.dot(trf, jnp.broadcast_to(lf_c, (T, 128)), precision=HIGHEST,
                      preferred_element_type=F32)[:, 0:1]
        b_r = lax.dot_general(jnp.broadcast_to(lf_r, (8, T)), trf, (((1,), (1,)), ((), ())),
                              precision=HIGHEST, preferred_element_type=F32)[0:1, :]
        tot = b_c[T - 1:T, :] if d == 0 else b_c[0:1, :]
        g_c = b_c + m
        dlog = jnp.where(tri, b_c - b_r + li_r, -jnp.inf)
        m_t = jnp.maximum(g_c, jnp.max(dlog, axis=1, keepdims=True))
        inter = jnp.exp(g_c - m_t)
        s = lax.dot_general(qc, kc, (((1,), (1,)), ((), ())), preferred_element_type=F32) * jnp.exp(dlog - m_t)
        num = (inter * jnp.dot(qc, ct.astype(BF16), preferred_element_type=F32)
               + jnp.dot(s.astype(BF16), vc, preferred_element_type=F32))
        den = (inter * jnp.sum(qc.astype(F32) * n, axis=1, keepdims=True)
               + jnp.sum(s, axis=1, keepdims=True))
        h = num / jnp.maximum(jnp.abs(den), jnp.exp(-m_t))
        wl_r = tot - b_r + li_r
        wl_c = tot - b_c + li_c
        m_new = jnp.maximum(tot + m, jnp.max(wl_r, axis=1, keepdims=True))
        decay = jnp.exp(tot + m - m_new)
        w_c = jnp.exp(wl_c - m_new)
        wv = (w_c * vc.astype(F32)).astype(BF16)
        ct_new = decay * ct + lax.dot_general(kc, wv, (((0,), (0,)), ((), ())), preferred_element_type=F32)
        n_new = decay * n + jnp.sum(w_c * kc.astype(F32), axis=0, keepdims=True)
        return h, rows, (ct_new, n_new, m_new)

    def body(i, carry):
        sf, sr = carry
        hf, rows_f, sf = chunk_step(lax.rem(i + nlat, nch), 0, sf)
        hf_ref[rows_f, :] = hf
        hr, rows_r, sr = chunk_step(nch - 1 - i, 1, sr)
        hr_ref[rows_r, :] = hr
        return sf, sr

    zero = (jnp.zeros((ML_DH, ML_DH), F32), jnp.zeros((1, ML_DH), F32), jnp.zeros((1, 1), F32))
    lax.fori_loop(0, nch, body, (zero, zero))
    hs = hf_ref[...] + hr_ref[...]
    hn = hs * lax.rsqrt(jnp.mean(hs * hs, axis=-1, keepdims=True) + EPS) * nw_ref[...]
    out_ref[...] = (hn * jax.nn.sigmoid(o_ref[...])).astype(BF16)


def _mlstm(mq, mk, mv, mo, misc, mgt, ig_bias, fg_bias, out_norm, n_lat):
    B, Lt, _ = mq.shape
    T = ML_CHUNK
    nch, nlat = Lt // T, n_lat // T
    gc = misc[:, :, MLA_ROPE:MLA_ROPE + 16].reshape(B, Lt, ML_HEADS, 4)
    gc = jnp.transpose(gc, (0, 2, 1, 3)).reshape(B, ML_HEADS, nch, T, 4)
    gr = jnp.transpose(mgt.reshape(B, ML_HEADS, 4, nch, T), (0, 1, 3, 2, 4))
    head_spec = pl.BlockSpec((None, Lt, ML_DH), lambda b, h: (b, 0, h))
    smem = pl.BlockSpec(memory_space=pltpu.SMEM)
    return pl.pallas_call(
        functools.partial(_mlstm_kernel, nch=nch, nlat=nlat),
        grid=(B, ML_HEADS),
        in_specs=[smem, smem, head_spec, head_spec, head_spec, head_spec,
                  pl.BlockSpec((None, None, nch, T, 4), lambda b, h: (b, h, 0, 0, 0)),
                  pl.BlockSpec((None, None, nch, 4, T), lambda b, h: (b, h, 0, 0, 0)),
                  pl.BlockSpec((None, 1, ML_DH), lambda b, h: (h, 0, 0))],
        out_specs=head_spec,
        out_shape=jax.ShapeDtypeStruct((B, Lt, GROUP_W), BF16),
        scratch_shapes=[pltpu.VMEM((Lt, ML_DH), F32), pltpu.VMEM((Lt, ML_DH), F32)],
        compiler_params=_cparams(("parallel", "parallel"), 40), name="mlstm",
    )(ig_bias.astype(F32), fg_bias.astype(F32), mq, mk, mv, mo, gc, gr,
      out_norm.astype(F32).reshape(ML_HEADS, 1, ML_DH))


def _rope(x, cos, sins):
    q = MLA_ROPE // 4
    sw = jnp.concatenate([x[:, q:2 * q], x[:, 0:q], x[:, 3 * q:4 * q], x[:, 2 * q:3 * q]], axis=1)
    return x * cos + sw * sins


def _mla_prep_kernel(cq_ref, ckv_ref, misc_ref, cos_ref, sin_ref, qan_ref, wq_ref, kvan_ref, wkv_ref,
                     qn_ref, kn_ref, qp_ref, kp_ref, v_ref):
    def rms(x, w):
        return (x * lax.rsqrt(jnp.mean(x * x, axis=-1, keepdims=True) + EPS)) * w

    cos, sins = cos_ref[...], sin_ref[...]
    hw = MLA_HEADS * MLA_NOPE
    q_all = jnp.dot(rms(cq_ref[...], qan_ref[...]).astype(BF16), wq_ref[...], preferred_element_type=F32)
    kv = jnp.dot(rms(ckv_ref[...], kvan_ref[...]).astype(BF16), wkv_ref[...], preferred_element_type=F32)
    kr = misc_ref[:, 0:MLA_ROPE]
    kr_sq = jnp.sum(kr * kr, axis=-1, keepdims=True)
    v_ref[...] = kv[:, hw:].astype(BF16)
    zeros = jnp.zeros((cq_ref.shape[0], MLA_PAD - MLA_QK), BF16)
    for h in range(MLA_HEADS):
        base = h * MLA_PAD
        qn = q_all[:, h * MLA_NOPE:(h + 1) * MLA_NOPE]
        qr = q_all[:, hw + h * MLA_ROPE:hw + (h + 1) * MLA_ROPE]
        inv = lax.rsqrt((jnp.sum(qn * qn, axis=-1, keepdims=True)
                         + jnp.sum(qr * qr, axis=-1, keepdims=True)) / MLA_QK + EPS)
        qp_ref[:, base:base + MLA_NOPE] = ((qn * inv) * qn_ref[:, 0:MLA_NOPE] * ATTN_SCALE).astype(BF16)
        qr = _rope((qr * inv) * qn_ref[:, MLA_NOPE:MLA_QK], cos, sins)
        qp_ref[:, base + MLA_NOPE:base + MLA_QK] = (qr * ATTN_SCALE).astype(BF16)
        qp_ref[:, base + MLA_QK:base + MLA_PAD] = zeros
        kn = kv[:, h * MLA_NOPE:(h + 1) * MLA_NOPE]
        inv = lax.rsqrt((jnp.sum(kn * kn, axis=-1, keepdims=True) + kr_sq) / MLA_QK + EPS)
        kp_ref[:, base:base + MLA_NOPE] = ((kn * inv) * kn_ref[:, 0:MLA_NOPE]).astype(BF16)
        kp_ref[:, base + MLA_NOPE:base + MLA_QK] = _rope((kr * inv) * kn_ref[:, MLA_NOPE:MLA_QK],
                                                         cos, sins).astype(BF16)
        kp_ref[:, base + MLA_QK:base + MLA_PAD] = zeros


def _attn_kernel(q_ref, k_ref, v_ref, o_ref, *, n_lat, n_lat_tiles):
    def attend(k, v):
        s = lax.dot_general(q_ref[...], k, (((1,), (1,)), ((), ())), preferred_element_type=F32)
        p = jnp.exp(s - jnp.max(s, axis=-1, keepdims=True))
        den = jnp.sum(p, axis=-1, keepdims=True)
        o_ref[...] = (jnp.dot(p.astype(BF16), v, preferred_element_type=F32) / den).astype(BF16)

    qi = pl.program_id(2)

    @pl.when(qi < n_lat_tiles)
    def _():
        attend(k_ref[...], v_ref[...])

    @pl.when(qi >= n_lat_tiles)
    def _():
        attend(k_ref[n_lat:, :], v_ref[n_lat:, :])


def _rope_tables(n_lat, n_ctx):
    n_rows = n_lat // GRID_W
    rows = jnp.repeat(jnp.arange(n_rows, dtype=F32), GRID_W)
    cols = jnp.tile(jnp.arange(GRID_W, dtype=F32), n_rows)
    n_freq = MLA_ROPE // 4
    inv_freq = ROPE_BASE ** (-jnp.arange(n_freq, dtype=F32) / n_freq)
    ar, ac = rows[:, None] * inv_freq, cols[:, None] * inv_freq
    cos = jnp.concatenate([jnp.cos(ar), jnp.cos(ar), jnp.cos(ac), jnp.cos(ac)], axis=1)
    sins = jnp.concatenate([-jnp.sin(ar), jnp.sin(ar), -jnp.sin(ac), jnp.sin(ac)], axis=1)
    cos = jnp.concatenate([cos, jnp.ones((n_ctx, MLA_ROPE), F32)], axis=0)
    sins = jnp.concatenate([sins, jnp.zeros((n_ctx, MLA_ROPE), F32)], axis=0)
    return cos, sins


def _mla(cq, ckv, misc, tables, qan, wq, kvan, wkv, qn, kn, n_lat):
    B, Lt, _ = cq.shape
    tm = TOKEN_TILE
    bm = lambda b, t: (b, t, 0)
    full = lambda shape: pl.BlockSpec(shape, lambda b, t: (0,) * len(shape))
    hp = MLA_HEADS * MLA_PAD
    qp, kp, v = pl.pallas_call(
        _mla_prep_kernel, grid=(B, Lt // tm),
        in_specs=[pl.BlockSpec((None, tm, MLA_Q_LORA), bm), pl.BlockSpec((None, tm, MLA_KV_LORA), bm),
                  pl.BlockSpec((None, tm, MISC_W), bm),
                  pl.BlockSpec((tm, MLA_ROPE), lambda b, t: (t, 0)), pl.BlockSpec((tm, MLA_ROPE), lambda b, t: (t, 0)),
                  full((1, MLA_Q_LORA)), full((MLA_Q_LORA, MLA_HEADS * MLA_QK)),
                  full((1, MLA_KV_LORA)), full((MLA_KV_LORA, MLA_HEADS * (MLA_NOPE + MLA_V))),
                  full((1, MLA_QK)), full((1, MLA_QK))],
        out_specs=[pl.BlockSpec((None, tm, hp), bm), pl.BlockSpec((None, tm, hp), bm),
                   pl.BlockSpec((None, tm, GROUP_W), bm)],
        out_shape=[jax.ShapeDtypeStruct((B, Lt, hp), BF16), jax.ShapeDtypeStruct((B, Lt, hp), BF16),
                   jax.ShapeDtypeStruct((B, Lt, GROUP_W), BF16)],
        compiler_params=_cparams(("parallel", "parallel"), 40), name="mla_prep",
    )(cq, ckv, misc, tables[0], tables[1], qan.reshape(1, -1), wq, kvan.reshape(1, -1), wkv,
      qn.reshape(1, -1), kn.reshape(1, -1))
    tq = Q_TILE
    return pl.pallas_call(
        functools.partial(_attn_kernel, n_lat=n_lat, n_lat_tiles=n_lat // tq),
        grid=(B, MLA_HEADS, Lt // tq),
        in_specs=[pl.BlockSpec((None, tq, MLA_PAD), lambda b, h, t: (b, t, h)),
                  pl.BlockSpec((None, Lt, MLA_PAD), lambda b, h, t: (b, 0, h)),
                  pl.BlockSpec((None, Lt, MLA_V), lambda b, h, t: (b, 0, h))],
        out_specs=pl.BlockSpec((None, tq, MLA_V), lambda b, h, t: (b, t, h)),
        out_shape=jax.ShapeDtypeStruct((B, Lt, GROUP_W), BF16),
        compiler_params=_cparams(("parallel", "parallel", "arbitrary"), 40), name="mla_attn",
    )(qp, kp, v)


def _wout_kernel(x_ref, mod_ref, a_ref, b_ref, m_ref, r_ref, w_ref, o_ref):
    acc = jnp.dot(a_ref[...], w_ref[0:GROUP_W, :], preferred_element_type=F32)
    acc += jnp.dot(b_ref[...], w_ref[GROUP_W:2 * GROUP_W, :], preferred_element_type=F32)
    acc += jnp.dot(m_ref[...], w_ref[2 * GROUP_W:3 * GROUP_W, :], preferred_element_type=F32)
    acc += jnp.dot(r_ref[...], w_ref[3 * GROUP_W:4 * GROUP_W, :], preferred_element_type=F32)
    o_ref[...] = x_ref[...] + mod_ref[2:3, :] * acc


def _wout(X, mod_l, a_tm, b_bm, m_bm, r_tm, w, n_lat_tiles):
    B, Lt, D = X.shape
    tm = TOKEN_TILE
    bm = lambda b, t: (b, t, 0)
    tmaj = lambda b, t: (t, b)
    return pl.pallas_call(
        _wout_kernel, grid=(B, Lt // tm),
        in_specs=[pl.BlockSpec((None, tm, D), bm),
                  pl.BlockSpec((None, 6, D), lambda b, t: (jnp.where(t >= n_lat_tiles, B, b), 0, 0)),
                  pl.BlockSpec((tm, GROUP_W), tmaj), pl.BlockSpec((None, tm, GROUP_W), bm),
                  pl.BlockSpec((None, tm, GROUP_W), bm), pl.BlockSpec((tm, GROUP_W), tmaj),
                  pl.BlockSpec((4 * GROUP_W, D), lambda b, t: (0, 0))],
        out_specs=pl.BlockSpec((None, tm, D), bm),
        out_shape=jax.ShapeDtypeStruct((B, Lt, D), F32),
        compiler_params=_cparams(("parallel", "parallel"), 48), name="out_proj",
    )(X, mod_l, a_tm, b_bm, m_bm, r_tm, w)


def _mlp_kernel(x_ref, mod_ref, nw_ref, w1_ref, w2_ref, o_ref, h_ref, acc_ref):
    f = pl.program_id(2)

    @pl.when(f == 0)
    def _():
        h_ref[...] = _norm_mod(x_ref[...], nw_ref[...], mod_ref[3:4, :], mod_ref[4:5, :])
        acc_ref[...] = jnp.zeros_like(acc_ref)

    a = jnp.maximum(jnp.dot(h_ref[...], w1_ref[...], preferred_element_type=F32), 0.0)
    acc_ref[...] += jnp.dot((a * a).astype(BF16), w2_ref[...], preferred_element_type=F32)

    @pl.when(f == pl.num_programs(2) - 1)
    def _():
        o_ref[...] = x_ref[...] + mod_ref[5:6, :] * acc_ref[...]


def _mlp(X, mod_l, nw, w1, w2, n_lat_tiles):
    B, Lt, D = X.shape
    tm, tf = TOKEN_TILE, MLP_FF_TILE
    ff = w1.shape[1]
    bm = lambda b, t, f: (b, t, 0)
    return pl.pallas_call(
        _mlp_kernel, grid=(B, Lt // tm, ff // tf),
        in_specs=[pl.BlockSpec((None, tm, D), bm),
                  pl.BlockSpec((None, 6, D), lambda b, t, f: (jnp.where(t >= n_lat_tiles, B, b), 0, 0)),
                  pl.BlockSpec((1, D), lambda b, t, f: (0, 0)),
                  pl.BlockSpec((D, tf), lambda b, t, f: (0, f)),
                  pl.BlockSpec((tf, D), lambda b, t, f: (f, 0))],
        out_specs=pl.BlockSpec((None, tm, D), bm),
        out_shape=jax.ShapeDtypeStruct((B, Lt, D), F32),
        scratch_shapes=[pltpu.VMEM((tm, D), BF16), pltpu.VMEM((tm, D), F32)],
        compiler_params=_cparams(("parallel", "parallel", "arbitrary"), 48), name="mlp",
    )(X, mod_l, nw.reshape(1, D), w1, w2)


def _pack_w_in(w_in):
    offs = [0, 512, 1024, 1536, 2048, 2560, 2576, 2960, 3088, 3152, 3664, 4176]
    u, mq, mk, mv, mo, mg, cq, ckv, kr, lx, lg = [w_in[..., offs[i]:offs[i + 1]] for i in range(11)]
    depth, d = w_in.shape[:2]
    mg = jnp.transpose(mg.reshape(depth, d, 2, 2, ML_HEADS), (0, 1, 4, 2, 3)).reshape(depth, d, 16)
    pad = jnp.zeros((depth, d, MISC_W - MLA_ROPE - 16), w_in.dtype)
    w_p = jnp.concatenate([u, mq, mk, mv, mo, cq, ckv, lx, lg, kr, mg, pad], axis=-1).astype(BF16)
    return w_p, jnp.transpose(mg, (0, 2, 1)).astype(BF16)


def _pack_mla(w_q_up, w_kv_up):
    depth = w_q_up.shape[0]
    wq = w_q_up.reshape(depth, MLA_Q_LORA, MLA_HEADS, MLA_QK)
    wq = jnp.concatenate([wq[..., :MLA_NOPE].reshape(depth, MLA_Q_LORA, -1),
                          wq[..., MLA_NOPE:].reshape(depth, MLA_Q_LORA, -1)], axis=-1).astype(BF16)
    wkv = w_kv_up.reshape(depth, MLA_KV_LORA, MLA_HEADS, MLA_NOPE + MLA_V)
    wkv = jnp.concatenate([wkv[..., :MLA_NOPE].reshape(depth, MLA_KV_LORA, -1),
                           wkv[..., MLA_NOPE:].reshape(depth, MLA_KV_LORA, -1)], axis=-1).astype(BF16)
    return wq, wkv


def _pack_lru(wa, ba, wx, bx):
    wab = jnp.concatenate([wa, wx], axis=-1).astype(BF16)
    bias = jnp.stack([ba, bx], axis=2).astype(F32)
    return wab, bias


def kernel(x, c, ctx, c_ctx, ada_w, ada_b, norm1_w, norm2_w, w_in, w_out, s5_lam_re, s5_lam_im, s5_log_dt, s5_b_re, s5_b_im, s5_c_re, s5_c_im, s5_d, s5_glu_w, s5_glu_b, ml_ig_bias, ml_fg_bias, ml_out_norm, mla_q_a_norm, mla_w_q_up, mla_kv_a_norm, mla_w_kv_up, mla_q_norm, mla_k_norm, lru_conv_w, lru_conv_b, lru_wa, lru_ba, lru_wx, lru_bx, lru_lam, mlp_w1, mlp_w2):
    B, L, D = x.shape
    Lc = ctx.shape[1]
    depth = ada_w.shape[0]
    n_lat_tiles = L // TOKEN_TILE
    X = jnp.concatenate([x, ctx], axis=1).astype(F32)
    mod = _ada_all(c, c_ctx, ada_w, ada_b)
    w_p, wgt = _pack_w_in(w_in)
    wq, wkv = _pack_mla(mla_w_q_up, mla_w_kv_up)
    wab, lru_bias = _pack_lru(lru_wa, lru_ba, lru_wx, lru_bx)
    tables = _rope_tables(L, Lc)
    w_out_b, glu_w_b = w_out.astype(BF16), s5_glu_w.astype(BF16)
    w1_b, w2_b = mlp_w1.astype(BF16), mlp_w2.astype(BF16)
    for l in range(depth):
        u, mq, mk, mv, mo, cq, ckv, lx, lg, misc, mgt = _win(X, mod[l], norm1_w[l], w_p[l], wgt[l], n_lat_tiles)
        s5p = _s5_params(s5_lam_re[l], s5_lam_im[l], s5_log_dt[l], s5_b_re[l], s5_b_im[l], s5_c_re[l], s5_c_im[l])
        a_tm = _s5(u.reshape(-1, GROUP_W), s5p, s5_d[l], glu_w_b[l], s5_glu_b[l], B, L)
        r_tm = _lru(lx.reshape(-1, GROUP_W), lg.reshape(-1, GROUP_W), lru_conv_w[l], lru_conv_b[l],
                    wab[l], lru_bias[l], lru_lam[l], B, L)
        b_bm = _mlstm(mq, mk, mv, mo, misc, mgt, ml_ig_bias[l], ml_fg_bias[l], ml_out_norm[l], L)
        m_bm = _mla(cq, ckv, misc, tables, mla_q_a_norm[l], wq[l], mla_kv_a_norm[l], wkv[l],
                    mla_q_norm[l], mla_k_norm[l], L)
        X = _wout(X, mod[l], a_tm.reshape(-1, B * GROUP_W), b_bm, m_bm, r_tm.reshape(-1, B * GROUP_W),
                  w_out_b[l], n_lat_tiles)
        X = _mlp(X, mod[l], norm2_w[l], w1_b[l], w2_b[l], n_lat_tiles)
    return X[:, :L]
```

```python
import functools

import jax
import jax.numpy as jnp
from jax import lax
from jax.experimental import pallas as pl
from jax.experimental.pallas import tpu as pltpu

F32 = jnp.float32
BF16 = jnp.bfloat16
EPS = 1e-6
HIGHEST = lax.Precision.HIGHEST

GROUP_W = 512
TOKEN_TILE = 256
S5_STATE = 64
S5_CH = 16
S5_STRIPS = 4
S5_STRIP_STATES = 512
S5_MIN_NEG = -1e-4
S5_CHUNK = 64
ML_HEADS = 4
ML_DH = 128
ML_CHUNK = 128
ML_HEADS_PER_STEP = 2
MLA_HEADS = 4
MLA_NOPE = 128
MLA_ROPE = 64
MLA_QK = MLA_NOPE + MLA_ROPE
MLA_V = 128
MLA_Q_LORA = 384
MLA_KV_LORA = 128
MLA_PAD = 256
ATTN_SCALE = MLA_QK ** -0.5
ROPE_BASE = 10000.0
GRID_W = 64
Q_TILE = 512
Q_SUB = 256
LRU_BLOCKS = 4
LRU_BLOCK_W = 128
LRU_CONV = 4
LRU_C = 8.0
MLP_FF_TILE = 1024
MLP_OUT_TILE = 256
MLP_ROWS = 512

COL_U, COL_MQ, COL_MK, COL_MV, COL_MO = 0, 512, 1024, 1536, 2048
COL_CQ, COL_CKV, COL_LX, COL_LG, COL_MISC = 2560, 2944, 3072, 3584, 4096
IN_COLS_PACKED = 4224
MISC_W = 128


def _cparams(sem, vmem_mb):
    return pltpu.CompilerParams(dimension_semantics=sem, vmem_limit_bytes=vmem_mb << 20)


def _norm_mod(x, nw, shift, scale):
    ms = jnp.mean(x * x, axis=-1, keepdims=True)
    y = (x * lax.rsqrt(ms + EPS)) * nw
    return (y * (1.0 + scale) + shift).astype(BF16)


def _log_sigmoid(x):
    return jnp.minimum(x, 0.0) - jnp.log1p(jnp.exp(-jnp.abs(x)))


def _softplus(x):
    return jnp.maximum(x, 0.0) + jnp.log1p(jnp.exp(-jnp.abs(x)))


def _ada_kernel(c_ref, w_ref, b_ref, o_ref):
    c = c_ref[...]
    act = (c * jax.nn.sigmoid(c)).astype(BF16)
    o_ref[...] = jnp.dot(act, w_ref[...].astype(BF16), preferred_element_type=F32) + b_ref[...]


def _ada_all(c, c_ctx, ada_w, ada_b):
    depth, d, n6 = ada_w.shape
    nb = c.shape[0]
    rows = nb + 8
    cc = jnp.concatenate([c.astype(F32), jnp.broadcast_to(c_ctx.astype(F32)[None], (8, d))], axis=0)
    tn = 1024
    out = pl.pallas_call(
        _ada_kernel,
        grid=(depth, n6 // tn),
        in_specs=[pl.BlockSpec((rows, d), lambda l, n: (0, 0)),
                  pl.BlockSpec((None, d, tn), lambda l, n: (l, 0, n)),
                  pl.BlockSpec((None, 1, tn), lambda l, n: (l, 0, n))],
        out_specs=pl.BlockSpec((None, rows, tn), lambda l, n: (l, 0, n)),
        out_shape=jax.ShapeDtypeStruct((depth, rows, n6), F32),
        compiler_params=_cparams(("parallel", "parallel"), 40),
        name="ada_mod",
    )(cc, ada_w, ada_b.reshape(depth, 1, n6))
    return out.reshape(depth, rows, 6, d)


def _win_kernel(x_ref, mod_ref, nw_ref, w_ref, wgt_ref, wvt_ref,
                u_ref, mq_ref, mk_ref, mvt_ref, mo_ref, cq_ref, ckv_ref, lx_ref, lg_ref, misc_ref, mgt_ref):
    h = _norm_mod(x_ref[...], nw_ref[...], mod_ref[0:1, :], mod_ref[1:2, :])
    nt_dims = (((1,), (1,)), ((), ()))

    def mm(lo, width):
        return jnp.dot(h, w_ref[:, lo:lo + width], preferred_element_type=F32)

    u_ref[...] = mm(COL_U, GROUP_W)
    mq_ref[...] = mm(COL_MQ, GROUP_W).astype(BF16)
    mk_ref[...] = (mm(COL_MK, GROUP_W) * (ML_DH ** -0.5)).astype(BF16)
    for cc in range(mvt_ref.shape[0]):
        hc = h[cc * ML_CHUNK:(cc + 1) * ML_CHUNK, :]
        mvt_ref[cc] = lax.dot_general(wvt_ref[...], hc, nt_dims, preferred_element_type=F32).astype(BF16)
    mo_ref[...] = mm(COL_MO, GROUP_W)
    cq_ref[...] = mm(COL_CQ, MLA_Q_LORA)
    ckv_ref[...] = mm(COL_CKV, MLA_KV_LORA)
    lx_ref[...] = mm(COL_LX, GROUP_W)
    lg_ref[...] = mm(COL_LG, GROUP_W)
    misc_ref[...] = mm(COL_MISC, MISC_W)
    mgt_ref[...] = lax.dot_general(wgt_ref[...], h, nt_dims, preferred_element_type=F32)


def _win(X, mod_l, nw, w_p, wgt, wvt, n_lat_tiles):
    B, Lt, D = X.shape
    tm = TOKEN_TILE
    nt = Lt // tm
    bm = lambda b, t: (b, t, 0)
    tmaj = lambda b, t: (t, b)
    in_specs = [
        pl.BlockSpec((None, tm, D), bm),
        pl.BlockSpec((None, 6, D), lambda b, t: (jnp.where(t >= n_lat_tiles, B, b), 0, 0)),
        pl.BlockSpec((1, D), lambda b, t: (0, 0)),
        pl.BlockSpec((D, IN_COLS_PACKED), lambda b, t: (0, 0), pipeline_mode=pl.Buffered(1)),
        pl.BlockSpec((16, D), lambda b, t: (0, 0)),
        pl.BlockSpec((GROUP_W, D), lambda b, t: (0, 0)),
    ]
    cpt = tm // ML_CHUNK
    out_specs = [
        pl.BlockSpec((tm, GROUP_W), tmaj),
        pl.BlockSpec((None, tm, GROUP_W), bm),
        pl.BlockSpec((None, tm, GROUP_W), bm),
        pl.BlockSpec((None, cpt, GROUP_W, ML_CHUNK), lambda b, t: (b, t, 0, 0)),
        pl.BlockSpec((None, tm, GROUP_W), bm),
        pl.BlockSpec((None, tm, MLA_Q_LORA), bm),
        pl.BlockSpec((None, tm, MLA_KV_LORA), bm),
        pl.BlockSpec((tm, GROUP_W), tmaj),
        pl.BlockSpec((tm, GROUP_W), tmaj),
        pl.BlockSpec((None, tm, MISC_W), bm),
        pl.BlockSpec((None, 16, tm), lambda b, t: (b, 0, t)),
    ]
    sds = jax.ShapeDtypeStruct
    out_shape = [
        sds((Lt, B * GROUP_W), F32),
        sds((B, Lt, GROUP_W), BF16), sds((B, Lt, GROUP_W), BF16),
        sds((B, Lt // ML_CHUNK, GROUP_W, ML_CHUNK), BF16),
        sds((B, Lt, GROUP_W), F32),
        sds((B, Lt, MLA_Q_LORA), F32), sds((B, Lt, MLA_KV_LORA), F32),
        sds((Lt, B * GROUP_W), F32), sds((Lt, B * GROUP_W), F32),
        sds((B, Lt, MISC_W), F32),
        sds((B, 16, Lt), F32),
    ]
    return pl.pallas_call(
        _win_kernel, grid=(B, nt), in_specs=in_specs, out_specs=out_specs, out_shape=out_shape,
        compiler_params=_cparams(("parallel", "parallel"), 52), name="in_proj",
    )(X, mod_l, nw.reshape(1, D), w_p, wgt, wvt)


def _s5_kernel(*refs, tc, nb, reverse, final):
    if final:
        u_ref, yp_ref, wd_ref, a_ref, wc_ref, dsk_ref, gw_ref, gb_ref, o_ref, buf, ybuf, st = refs
    else:
        u_ref, wd_ref, a_ref, wc_ref, o_ref, buf, st = refs
    R = tc * nb
    RB = min(R, 512)
    NS = S5_STRIP_STATES
    nh = nb // 8

    @pl.when(pl.program_id(0) == 0)
    def _():
        st[...] = jnp.zeros_like(st)

    for s in range(S5_STRIPS):
        cs = slice(s * 128, (s + 1) * 128)
        for rb in range(R // RB):
            rs = slice(rb * RB, (rb + 1) * RB)
            buf[rs, :] = jnp.dot(u_ref[rs, cs].astype(BF16), wd_ref[s], preferred_element_type=F32)
        ar = jnp.broadcast_to(a_ref[s, 0:1, :], (8, NS))
        ai = jnp.broadcast_to(a_ref[s, 1:2, :], (8, NS))
        init = []
        for hh in range(nh):
            init += [st[s, hh * 8:(hh + 1) * 8, 0:NS], st[s, hh * 8:(hh + 1) * 8, NS:2 * NS]]

        def body(k, carry, ar=ar, ai=ai):
            t = (tc - 1 - k) if reverse else k
            out = []
            for hh in range(nh):
                sr, si = carry[2 * hh], carry[2 * hh + 1]
                row = pl.multiple_of(t * nb + hh * 8, 8)
                br = buf[pl.ds(row, 8), 0:NS]
                bi = buf[pl.ds(row, 8), NS:2 * NS]
                nsr = ar * sr - ai * si + br
                nsi = ar * si + ai * sr + bi
                buf[pl.ds(row, 8), 0:NS] = nsr
                buf[pl.ds(row, 8), NS:2 * NS] = nsi
                out += [nsr, nsi]
            return tuple(out)

        fin = lax.fori_loop(0, tc, body, tuple(init), unroll=2)
        for hh in range(nh):
            st[s, hh * 8:(hh + 1) * 8, 0:NS] = fin[2 * hh]
            st[s, hh * 8:(hh + 1) * 8, NS:2 * NS] = fin[2 * hh + 1]
        for rb in range(R // RB):
            rs = slice(rb * RB, (rb + 1) * RB)
            y = jnp.dot(buf[rs, :].astype(BF16), wc_ref[s], preferred_element_type=F32)
            if final:
                ybuf[rs, cs] = y
            else:
                o_ref[rs, cs] = y

    if final:
        for rb in range(R // RB):
            rs = slice(rb * RB, (rb + 1) * RB)
            y = ybuf[rs, :] + yp_ref[rs, :] + dsk_ref[...] * u_ref[rs, :]
            g = jax.nn.gelu(y)
            z = jnp.dot(g.astype(BF16), gw_ref[...], preferred_element_type=F32) + gb_ref[...]
            o_ref[rs, :] = (g * jax.nn.sigmoid(z)).astype(BF16)


def _chunk_order(nc, nlc, reverse):
    if reverse:
        return lambda i: nc - 1 - i
    return lambda i: lax.rem(i + nlc, nc)


def _s5(u2, p, d_skip, glu_w, glu_b, nb, n_lat):
    rows = u2.shape[0]
    lt = rows // nb
    tc = S5_CHUNK
    R = tc * nb
    nc, nlc = lt // tc, n_lat // tc
    wd, a, wc = p
    outs = None
    for d in range(2):
        final = d == 1
        cidx = _chunk_order(nc, nlc, reverse=final)
        row_spec = pl.BlockSpec((R, GROUP_W), lambda i: (cidx(i), 0))
        full = lambda shape: pl.BlockSpec(shape, lambda i: (0,) * len(shape))
        in_specs = [row_spec] + ([row_spec] if final else []) + [
            full((S5_STRIPS, 128, 2 * S5_STRIP_STATES)), full((S5_STRIPS, 2, S5_STRIP_STATES)),
            full((S5_STRIPS, 2 * S5_STRIP_STATES, 128))]
        args = [u2] + ([outs] if final else []) + [wd[d], a[d], wc[d]]
        scratch = [pltpu.VMEM((R, 2 * S5_STRIP_STATES), F32)]
        if final:
            in_specs += [full((1, GROUP_W)), full((GROUP_W, GROUP_W)), full((1, GROUP_W))]
            args += [d_skip.reshape(1, GROUP_W), glu_w, glu_b.reshape(1, GROUP_W)]
            scratch += [pltpu.VMEM((R, GROUP_W), F32)]
        scratch += [pltpu.VMEM((S5_STRIPS, nb, 2 * S5_STRIP_STATES), F32)]
        outs = pl.pallas_call(
            functools.partial(_s5_kernel, tc=tc, nb=nb, reverse=final, final=final),
            grid=(nc,), in_specs=in_specs, out_specs=row_spec,
            out_shape=jax.ShapeDtypeStruct((rows, GROUP_W), BF16 if final else F32),
            scratch_shapes=scratch,
            compiler_params=_cparams(("arbitrary",), 48), name="s5_rev" if final else "s5_fwd",
        )(*args)
    return outs


def _s5_params(lam_re, lam_im, log_dt, b_re, b_im, c_re, c_im):
    lr = jnp.minimum(lam_re.astype(F32), S5_MIN_NEG)
    li = lam_im.astype(F32)
    dt = jnp.exp(log_dt.astype(F32))[..., None]
    mag = jnp.exp(lr * dt)
    ar, ai = mag * jnp.cos(li * dt), mag * jnp.sin(li * dt)
    den = lr * lr + li * li
    fr = ((ar - 1.0) * lr + ai * li) / den
    fi = (ai * lr - (ar - 1.0) * li) / den
    bbr = fr[..., None] * b_re - fi[..., None] * b_im
    bbi = fr[..., None] * b_im + fi[..., None] * b_re
    gs = 128 // S5_CH
    eye = jnp.eye(gs, dtype=F32)

    def drive(bb):
        t = bb.reshape(2, S5_STRIPS, gs, S5_STATE, S5_CH)
        return jnp.einsum('dsgpc,gh->dsgchp', t, eye).reshape(2, S5_STRIPS, 128, S5_STRIP_STATES)

    def read(cc):
        t = cc.reshape(2, S5_STRIPS, gs, S5_CH, S5_STATE)
        return jnp.einsum('dsgcp,gh->dsgphc', t, eye).reshape(2, S5_STRIPS, S5_STRIP_STATES, 128)

    wd = jnp.concatenate([drive(bbr), drive(bbi)], axis=-1).astype(BF16)
    wc = jnp.concatenate([read(c_re.astype(F32)), read(-c_im.astype(F32))], axis=-2).astype(BF16)
    a = jnp.stack([ar.reshape(2, S5_STRIPS, S5_STRIP_STATES), ai.reshape(2, S5_STRIPS, S5_STRIP_STATES)], axis=2)
    return wd, a, wc


def _lru_kernel(*refs, tc, nb, nlc, nc, reverse, final):
    if final:
        (x_ref, xp_ref, xn_ref, hp_ref, g_ref, cw_ref, cb_ref, w_ref, bias_ref, lam_ref,
         o_ref, xpad, abuf, bbuf, st) = refs
    else:
        x_ref, xp_ref, xn_ref, cw_ref, cb_ref, w_ref, bias_ref, lam_ref, o_ref, xpad, abuf, bbuf, st = refs
    i = pl.program_id(0)
    c = (nc - 1 - i) if reverse else lax.rem(i + nlc, nc)
    R = tc * nb
    RB = min(R, 256)
    nh = nb // 8
    left = (LRU_CONV // 2) * nb

    @pl.when(i == 0)
    def _():
        st[...] = jnp.zeros_like(st)

    starts = jnp.logical_or(c == 0, c == nlc)
    ends = jnp.logical_or(c == nlc - 1, c == nc - 1)
    xpad[0:left, :] = jnp.where(starts, 0.0, xp_ref[...])
    xpad[left:left + R, :] = x_ref[...]
    xpad[left + R:left + R + nb, :] = jnp.where(ends, 0.0, xn_ref[...])
    sp = _softplus(-lam_ref[...])

    for rb in range(R // RB):
        r0 = rb * RB
        xs = cb_ref[...]
        for j in range(LRU_CONV):
            xs = xs + xpad[r0 + j * nb:r0 + j * nb + RB, :] * cw_ref[j:j + 1, :]
        xb = xs.astype(BF16)
        for n in range(LRU_BLOCKS):
            cs = slice(n * LRU_BLOCK_W, (n + 1) * LRU_BLOCK_W)
            z = jnp.dot(xb[:, cs], w_ref[n], preferred_element_type=F32)
            r = jax.nn.sigmoid(z[:, :LRU_BLOCK_W] + bias_ref[0:1, cs])
            ig = jax.nn.sigmoid(z[:, LRU_BLOCK_W:] + bias_ref[1:2, cs])
            log_a = (-LRU_C * r) * sp[:, cs]
            abuf[r0:r0 + RB, cs] = jnp.exp(log_a)
            th = jnp.tanh(log_a)
            bbuf[r0:r0 + RB, cs] = jnp.sqrt(-2.0 * th / (1.0 - th)) * (ig * xs[:, cs])

    init = tuple(st[hh * 8:(hh + 1) * 8, :] for hh in range(nh))

    def body(k, carry):
        t = (tc - 1 - k) if reverse else k
        out = []
        for hh in range(nh):
            row = pl.multiple_of(t * nb + hh * 8, 8)
            hnew = abuf[pl.ds(row, 8), :] * carry[hh] + bbuf[pl.ds(row, 8), :]
            bbuf[pl.ds(row, 8), :] = hnew
            out.append(hnew)
        return tuple(out)

    fin = lax.fori_loop(0, tc, body, init, unroll=4)
    for hh in range(nh):
        st[hh * 8:(hh + 1) * 8, :] = fin[hh]
    if final:
        o_ref[...] = ((hp_ref[...] + bbuf[...]) * jax.nn.gelu(g_ref[...])).astype(BF16)
    else:
        o_ref[...] = bbuf[...]


def _lru(lx2, lg2, conv_w, conv_b, wab, bias, lam, nb, n_lat):
    rows = lx2.shape[0]
    lt = rows // nb
    tc = S5_CHUNK
    R = tc * nb
    nc, nlc = lt // tc, n_lat // tc
    outs = None
    for d in range(2):
        final = d == 1
        cidx = _chunk_order(nc, nlc, reverse=final)
        row_spec = pl.BlockSpec((R, GROUP_W), lambda i: (cidx(i), 0))
        prev_spec = pl.BlockSpec((2 * nb, GROUP_W), lambda i: (jnp.maximum(cidx(i) * (tc // 2) - 1, 0), 0))
        next_spec = pl.BlockSpec((nb, GROUP_W), lambda i: (jnp.minimum((cidx(i) + 1) * tc, lt - 1), 0))
        full = lambda shape: pl.BlockSpec(shape, lambda i: (0,) * len(shape))
        in_specs = [row_spec, prev_spec, next_spec] + ([row_spec, row_spec] if final else []) + [
            full((LRU_CONV, GROUP_W)), full((1, GROUP_W)), full((LRU_BLOCKS, LRU_BLOCK_W, 2 * LRU_BLOCK_W)),
            full((2, GROUP_W)), full((1, GROUP_W))]
        args = [lx2, lx2, lx2] + ([outs, lg2] if final else []) + [
            conv_w, conv_b.reshape(1, GROUP_W), wab[d], bias[d], lam[d].reshape(1, GROUP_W)]
        outs = pl.pallas_call(
            functools.partial(_lru_kernel, tc=tc, nb=nb, nlc=nlc, nc=nc, reverse=final, final=final),
            grid=(nc,), in_specs=in_specs, out_specs=row_spec,
            out_shape=jax.ShapeDtypeStruct((rows, GROUP_W), BF16 if final else F32),
            scratch_shapes=[pltpu.VMEM((R + 3 * nb, GROUP_W), F32), pltpu.VMEM((R, GROUP_W), F32),
                            pltpu.VMEM((R, GROUP_W), F32), pltpu.VMEM((nb, GROUP_W), F32)],
            compiler_params=_cparams(("arbitrary",), 48), name="lru_rev" if final else "lru_fwd",
        )(*args)
    return outs


def _mlstm_kernel(q_ref, k_ref, vt_ref, o_ref, gr_ref, gc_ref, br_ref, bc_ref, nw_ref, out_ref,
                  xb_scr, row_scr, nt_scr, u_scr, vec_scr, *, nch, nlat, hp):
    T = ML_CHUNK
    nt_dims = (((1,), (1,)), ((), ()))
    rid = lax.broadcasted_iota(jnp.int32, (T, T), 0)
    cid = lax.broadcasted_iota(jnp.int32, (T, T), 1)
    incl_rows = (rid <= cid).astype(F32)
    incl_cols = (rid >= cid).astype(F32)

    for j in range(hp):
        g = (gr_ref[j] + br_ref[j]).reshape(nch * 8, T)
        kind = lax.broadcasted_iota(jnp.int32, (nch * 8, T), 0) % 8
        lf = _log_sigmoid(g)
        pre = jnp.dot(lf, incl_rows, precision=HIGHEST, preferred_element_type=F32)
        b = jnp.where(kind == 1, pre[:, T - 1:T] - pre + lf, pre)
        x = g - pltpu.roll(b, 2, axis=0)
        row_scr[j] = jnp.where(kind < 2, b, x).reshape(nch, 8, T)
        gl = gc_ref[j] + bc_ref[j]
        kind = lax.broadcasted_iota(jnp.int32, (T, 128), 1) % 4
        lf = _log_sigmoid(gl)
        pre = jnp.dot(incl_cols, lf, precision=HIGHEST, preferred_element_type=F32)
        b = jnp.where(kind == 1, pre[T - 1:T, :] - pre + lf, pre)
        x = gl - pltpu.roll(b, 2, axis=1)
        for c in range(nch):
            for d in range(2):
                lane = c * 4 + 2 + d
                xb_scr[j, d, c] = jnp.broadcast_to(x[:, lane:lane + 1], (T, T))

    def independent(c, carry):
        rows = pl.ds(pl.multiple_of(c * T, T), T)
        pairs = [(j, d) for j in range(hp) for d in range(2)]
        hs = lambda j: slice(j * ML_DH, (j + 1) * ML_DH)
        swept = lambda d: (rid <= cid) if d == 0 else (rid >= cid)
        kq = {j: lax.dot_general(k_ref[rows, hs(j)], q_ref[rows, hs(j)], nt_dims, preferred_element_type=F32)
              for j in range(hp)}
        xm = {p: jnp.where(swept(p[1]), xb_scr[p[0], p[1], c], -jnp.inf) for p in pairs}
        a_row = {p: jnp.max(xm[p], axis=0, keepdims=True) for p in pairs}
        w0 = {}
        for j, d in pairs:
            a_last = a_row[j, d][:, T - 1:T] if d == 0 else a_row[j, d][:, 0:1]
            w0[j, d] = jnp.exp(row_scr[j, c, 2 + d:3 + d, :] - a_last)
        s0 = {p: kq[p[0]] * jnp.exp(xm[p] - a_row[p]) for p in pairs}
        for j, d in pairs:
            vt, kc = vt_ref[c, hs(j), :], k_ref[rows, hs(j)]
            nt_scr[j, d, c] = jnp.dot(vt, s0[j, d].astype(BF16), preferred_element_type=F32)
            wv = (vt.astype(F32) * w0[j, d]).astype(BF16)
            u_scr[j, d, c] = jnp.dot(wv, kc, preferred_element_type=F32)
            vec_scr[j, d, c, 0:1, :] = a_row[j, d]
            vec_scr[j, d, c, 1:2, :] = jnp.sum(s0[j, d], axis=0, keepdims=True)
            vec_scr[j, d, c, 2:3, :] = jnp.dot(jnp.broadcast_to(w0[j, d], (8, T)).astype(BF16), kc,
                                               preferred_element_type=F32)[0:1, :]
        return carry

    lax.fori_loop(0, nch, independent, 0)

    def pass2(c, d, j, state):
        cm, n, m = state
        hs = slice(j * ML_DH, (j + 1) * ML_DH)
        qc = q_ref[pl.ds(pl.multiple_of(c * T, T), T), hs]
        a_row, d0, n0 = vec_scr[j, d, c, 0:1, :], vec_scr[j, d, c, 1:2, :], vec_scr[j, d, c, 2:3, :]
        b_row = row_scr[j, c, d:d + 1, :]
        mu = jnp.maximum(m, a_row)
        inter, r = jnp.exp(m - mu), jnp.exp(a_row - mu)
        qct = lax.dot_general(cm.astype(BF16), qc, nt_dims, preferred_element_type=F32)
        qn = lax.dot_general(jnp.broadcast_to(n, (8, ML_DH)).astype(BF16), qc, nt_dims,
                             preferred_element_type=F32)[0:1, :]
        den = inter * qn + r * d0
        inv = 1.0 / jnp.maximum(jnp.abs(den), jnp.exp(-b_row - mu))
        nt_scr[j, d, c] = (inter * qct + r * nt_scr[j, d, c]) * inv
        a_last = a_row[:, T - 1:T] if d == 0 else a_row[:, 0:1]
        b_last = b_row[:, T - 1:T] if d == 0 else b_row[:, 0:1]
        mul = jnp.maximum(m, a_last)
        decay, rl = jnp.exp(m - mul), jnp.exp(a_last - mul)
        return decay * cm + rl * u_scr[j, d, c], decay * n + rl * n0, b_last + mul

    def states(i, carry):
        out = []
        for j in range(hp):
            out.append(pass2(lax.rem(i + nlat, nch), 0, j, carry[2 * j]))
            out.append(pass2(nch - 1 - i, 1, j, carry[2 * j + 1]))
        return tuple(out)

    zero = (jnp.zeros((ML_DH, ML_DH), F32), jnp.zeros((1, ML_DH), F32), jnp.zeros((1, 1), F32))
    lax.fori_loop(0, nch, states, (zero,) * (2 * hp))

    for c in range(nch):
        rows = slice(c * T, (c + 1) * T)
        for j in range(hp):
            hs = slice(j * ML_DH, (j + 1) * ML_DH)
            ht = nt_scr[j, 0, c] + nt_scr[j, 1, c]
            hn = ht * lax.rsqrt(jnp.mean(ht * ht, axis=0, keepdims=True) + EPS) * nw_ref[j]
            out_ref[rows, hs] = (hn.T * jax.nn.sigmoid(o_ref[rows, hs])).astype(BF16)


def _mlstm(mq, mk, mvt, mo, misc, mgt, ig_bias, fg_bias, out_norm, n_lat):
    B, Lt, _ = mq.shape
    T = ML_CHUNK
    nch, nlat = Lt // T, n_lat // T
    hp = ML_HEADS_PER_STEP
    assert 4 * nch <= 128
    gr = jnp.transpose(mgt.reshape(B, ML_HEADS, 4, nch, T), (0, 1, 3, 2, 4))
    gr = jnp.pad(gr, ((0, 0), (0, 0), (0, 0), (0, 4), (0, 0)))
    gc = misc[:, :, MLA_ROPE:MLA_ROPE + 16].reshape(B, nch, T, ML_HEADS, 4)
    gc = jnp.transpose(gc, (0, 3, 2, 1, 4)).reshape(B, ML_HEADS, T, nch * 4)
    gc = jnp.pad(gc, ((0, 0), (0, 0), (0, 0), (0, 128 - nch * 4)))
    kinds = jnp.concatenate([fg_bias.astype(F32), ig_bias.astype(F32)], axis=0).T
    br = jnp.broadcast_to(jnp.pad(kinds, ((0, 0), (0, 4)))[:, :, None], (ML_HEADS, 8, T))
    bc = jnp.pad(jnp.tile(kinds, (1, nch)), ((0, 0), (0, 128 - nch * 4)))[:, None, :]
    nwb = jnp.broadcast_to(out_norm.astype(F32)[:, :, None], (ML_HEADS, ML_DH, T))
    head_spec = pl.BlockSpec((None, Lt, hp * ML_DH), lambda b, h: (b, 0, h))
    slab = lambda: pltpu.VMEM((hp, 2, nch, T, T), F32)
    return pl.pallas_call(
        functools.partial(_mlstm_kernel, nch=nch, nlat=nlat, hp=hp),
        grid=(B, ML_HEADS // hp),
        in_specs=[head_spec, head_spec,
                  pl.BlockSpec((None, nch, hp * ML_DH, T), lambda b, h: (b, 0, h, 0)),
                  head_spec,
                  pl.BlockSpec((None, hp, nch, 8, T), lambda b, h: (b, h, 0, 0, 0)),
                  pl.BlockSpec((None, hp, T, 128), lambda b, h: (b, h, 0, 0)),
                  pl.BlockSpec((hp, 8, T), lambda b, h: (h, 0, 0)),
                  pl.BlockSpec((hp, 1, 128), lambda b, h: (h, 0, 0)),
                  pl.BlockSpec((hp, ML_DH, T), lambda b, h: (h, 0, 0))],
        out_specs=head_spec,
        out_shape=jax.ShapeDtypeStruct((B, Lt, GROUP_W), BF16),
        scratch_shapes=[slab(), pltpu.VMEM((hp, nch, 8, T), F32), slab(), slab(),
                        pltpu.VMEM((hp, 2, nch, 8, T), F32)],
        compiler_params=_cparams(("parallel", "parallel"), 52), name="mlstm",
    )(mq, mk, mvt, mo, gr, gc, br, bc, nwb)


def _mla_prep_kernel(cq_ref, ckv_ref, misc_ref, cos_ref, sin_ref, qan_ref, wq_ref, kvan_ref, wkv_ref,
                     qn_ref, kn_ref, qp_ref, kp_ref, v_ref):
    def rms(x, w):
        return (x * lax.rsqrt(jnp.mean(x * x, axis=-1, keepdims=True) + EPS)) * w

    tm = cq_ref.shape[0]
    cos, sins = cos_ref[...], sin_ref[...]
    lane = lax.broadcasted_iota(jnp.int32, (tm, 128), 1)
    first = (lane % (MLA_ROPE // 2)) < (MLA_ROPE // 4)

    def rope(x):
        sw = jnp.where(first, pltpu.roll(x, 128 - MLA_ROPE // 4, axis=1), pltpu.roll(x, MLA_ROPE // 4, axis=1))
        return x * cos + sw * sins

    hw = MLA_HEADS * MLA_NOPE
    q_all = jnp.dot(rms(cq_ref[...], qan_ref[...]).astype(BF16), wq_ref[...], preferred_element_type=F32)
    kv = jnp.dot(rms(ckv_ref[...], kvan_ref[...]).astype(BF16), wkv_ref[...], preferred_element_type=F32)
    kr = jnp.where(lane < MLA_ROPE, misc_ref[...], 0.0)
    kr_sq = jnp.sum(kr * kr, axis=-1, keepdims=True)
    v_ref[...] = kv[:, hw:].astype(BF16)
    qw_a, qw_b = qn_ref[:, 0:128] * ATTN_SCALE, qn_ref[:, 128:256] * ATTN_SCALE
    kw_a, kw_b = kn_ref[:, 0:128], kn_ref[:, 128:256]
    for h in range(MLA_HEADS):
        base = h * MLA_PAD
        qa, qb = q_all[:, base:base + 128], q_all[:, base + 128:base + 256]
        inv = lax.rsqrt(jnp.sum(qa * qa + qb * qb, axis=-1, keepdims=True) / MLA_QK + EPS)
        qp_ref[:, base:base + 128] = ((qa * inv) * qw_a).astype(BF16)
        qp_ref[:, base + 128:base + 256] = rope((qb * inv) * qw_b).astype(BF16)
        ka = kv[:, h * MLA_NOPE:(h + 1) * MLA_NOPE]
        inv = lax.rsqrt((jnp.sum(ka * ka, axis=-1, keepdims=True) + kr_sq) / MLA_QK + EPS)
        kp_ref[:, base:base + 128] = ((ka * inv) * kw_a).astype(BF16)
        kp_ref[:, base + 128:base + 256] = rope((kr * inv) * kw_b).astype(BF16)


def _attn_kernel(q_ref, k_ref, v_ref, o_ref, *, n_lat, n_lat_tiles, sub):
    def attend(n_sub, k, v):
        blocks = [slice(i * sub, (i + 1) * sub) for i in range(n_sub)]
        s = [lax.dot_general(q_ref[r, :], k, (((1,), (1,)), ((), ())), preferred_element_type=F32)
             for r in blocks]
        p = [jnp.exp(si - jnp.max(si, axis=-1, keepdims=True)) for si in s]
        for r, pi in zip(blocks, p):
            den = jnp.sum(pi, axis=-1, keepdims=True)
            o_ref[r, :] = (jnp.dot(pi.astype(BF16), v, preferred_element_type=F32) / den).astype(BF16)

    qi = pl.program_id(2)

    @pl.when(qi < n_lat_tiles)
    def _():
        attend(q_ref.shape[0] // sub, k_ref[...], v_ref[...])

    @pl.when(qi >= n_lat_tiles)
    def _():
        attend((k_ref.shape[0] - n_lat) // sub, k_ref[n_lat:, :], v_ref[n_lat:, :])


def _rope_tables(n_lat, n_ctx):
    n_rows = n_lat // GRID_W
    rows = jnp.repeat(jnp.arange(n_rows, dtype=F32), GRID_W)
    cols = jnp.tile(jnp.arange(GRID_W, dtype=F32), n_rows)
    n_freq = MLA_ROPE // 4
    inv_freq = ROPE_BASE ** (-jnp.arange(n_freq, dtype=F32) / n_freq)
    ar, ac = rows[:, None] * inv_freq, cols[:, None] * inv_freq
    cos = jnp.concatenate([jnp.cos(ar), jnp.cos(ar), jnp.cos(ac), jnp.cos(ac)], axis=1)
    sins = jnp.concatenate([-jnp.sin(ar), jnp.sin(ar), -jnp.sin(ac), jnp.sin(ac)], axis=1)
    cos = jnp.pad(cos, ((0, n_ctx), (0, 128 - MLA_ROPE)), constant_values=1.0)
    sins = jnp.pad(sins, ((0, n_ctx), (0, 128 - MLA_ROPE)))
    return cos, sins


def _mla(cq, ckv, misc, tables, qan, wq, kvan, wkv, qn, kn, n_lat):
    B, Lt, _ = cq.shape
    tm = TOKEN_TILE
    bm = lambda b, t: (b, t, 0)
    full = lambda shape: pl.BlockSpec(shape, lambda b, t: (0,) * len(shape))
    hp = MLA_HEADS * MLA_PAD
    qp, kp, v = pl.pallas_call(
        _mla_prep_kernel, grid=(B, Lt // tm),
        in_specs=[pl.BlockSpec((None, tm, MLA_Q_LORA), bm), pl.BlockSpec((None, tm, MLA_KV_LORA), bm),
                  pl.BlockSpec((None, tm, MISC_W), bm),
                  pl.BlockSpec((tm, 128), lambda b, t: (t, 0)), pl.BlockSpec((tm, 128), lambda b, t: (t, 0)),
                  full((1, MLA_Q_LORA)), full((MLA_Q_LORA, hp)),
                  full((1, MLA_KV_LORA)), full((MLA_KV_LORA, MLA_HEADS * (MLA_NOPE + MLA_V))),
                  full((1, MLA_PAD)), full((1, MLA_PAD))],
        out_specs=[pl.BlockSpec((None, tm, hp), bm), pl.BlockSpec((None, tm, hp), bm),
                   pl.BlockSpec((None, tm, GROUP_W), bm)],
        out_shape=[jax.ShapeDtypeStruct((B, Lt, hp), BF16), jax.ShapeDtypeStruct((B, Lt, hp), BF16),
                   jax.ShapeDtypeStruct((B, Lt, GROUP_W), BF16)],
        compiler_params=_cparams(("parallel", "parallel"), 40), name="mla_prep",
    )(cq, ckv, misc, tables[0], tables[1], qan.reshape(1, -1), wq, kvan.reshape(1, -1), wkv,
      jnp.pad(qn.astype(F32), (0, MLA_PAD - MLA_QK)).reshape(1, -1),
      jnp.pad(kn.astype(F32), (0, MLA_PAD - MLA_QK)).reshape(1, -1))
    tq = Q_TILE
    return pl.pallas_call(
        functools.partial(_attn_kernel, n_lat=n_lat, n_lat_tiles=n_lat // tq, sub=Q_SUB),
        grid=(B, MLA_HEADS, pl.cdiv(Lt, tq)),
        in_specs=[pl.BlockSpec((None, tq, MLA_PAD), lambda b, h, t: (b, t, h)),
                  pl.BlockSpec((None, Lt, MLA_PAD), lambda b, h, t: (b, 0, h)),
                  pl.BlockSpec((None, Lt, MLA_V), lambda b, h, t: (b, 0, h))],
        out_specs=pl.BlockSpec((None, tq, MLA_V), lambda b, h, t: (b, t, h)),
        out_shape=jax.ShapeDtypeStruct((B, Lt, GROUP_W), BF16),
        compiler_params=_cparams(("parallel", "parallel", "arbitrary"), 40), name="mla_attn",
    )(qp, kp, v)


def _wout_kernel(x_ref, mod_ref, a_ref, b_ref, m_ref, r_ref, w_ref, o_ref):
    acc = jnp.dot(a_ref[...], w_ref[0:GROUP_W, :], preferred_element_type=F32)
    acc += jnp.dot(b_ref[...], w_ref[GROUP_W:2 * GROUP_W, :], preferred_element_type=F32)
    acc += jnp.dot(m_ref[...], w_ref[2 * GROUP_W:3 * GROUP_W, :], preferred_element_type=F32)
    acc += jnp.dot(r_ref[...], w_ref[3 * GROUP_W:4 * GROUP_W, :], preferred_element_type=F32)
    o_ref[...] = x_ref[...] + mod_ref[2:3, :] * acc


def _wout(X, mod_l, a_tm, b_bm, m_bm, r_tm, w, n_lat_tiles):
    B, Lt, D = X.shape
    tm = TOKEN_TILE
    bm = lambda b, t: (b, t, 0)
    tmaj = lambda b, t: (t, b)
    return pl.pallas_call(
        _wout_kernel, grid=(B, Lt // tm),
        in_specs=[pl.BlockSpec((None, tm, D), bm),
                  pl.BlockSpec((None, 6, D), lambda b, t: (jnp.where(t >= n_lat_tiles, B, b), 0, 0)),
                  pl.BlockSpec((tm, GROUP_W), tmaj), pl.BlockSpec((None, tm, GROUP_W), bm),
                  pl.BlockSpec((None, tm, GROUP_W), bm), pl.BlockSpec((tm, GROUP_W), tmaj),
                  pl.BlockSpec((4 * GROUP_W, D), lambda b, t: (0, 0))],
        out_specs=pl.BlockSpec((None, tm, D), bm),
        out_shape=jax.ShapeDtypeStruct((B, Lt, D), F32),
        compiler_params=_cparams(("parallel", "parallel"), 48), name="out_proj",
    )(X, mod_l, a_tm, b_bm, m_bm, r_tm, w)


def _mlp_kernel(x_ref, xc_ref, mod_ref, modc_ref, nw_ref, w1_ref, w2_ref, *rest, nf, rows, saxis):
    o_ref, h_ref, a_ref = rest[-3:]
    s = pl.program_id(saxis)

    @pl.when(s == 0)
    def _():
        x = x_ref[...].reshape(rows, x_ref.shape[-1])
        h_ref[...] = _norm_mod(x, nw_ref[...], mod_ref[3:4, :], mod_ref[4:5, :])

    @pl.when(s < nf)
    def _():
        a = jnp.maximum(jnp.dot(h_ref[...], w1_ref[...], preferred_element_type=F32), 0.0)
        a_ref[s] = (a * a).astype(BF16)

    @pl.when(s >= nf)
    def _():
        y = jnp.dot(a_ref[0], w2_ref[0], preferred_element_type=F32)
        for f in range(1, nf):
            y += jnp.dot(a_ref[f], w2_ref[f], preferred_element_type=F32)
        o_ref[...] = xc_ref[...] + (modc_ref[5:6, :] * y).reshape(o_ref.shape)


def _mlp(X, mod_l, nw, w1, w2, n_lat, with_ctx):
    B, Lt, D = X.shape
    n_ctx = Lt - n_lat
    ff = w1.shape[1]
    tm, tf, tn = MLP_ROWS, MLP_FF_TILE, MLP_OUT_TILE
    nf, nn = ff // tf, D // tn
    ns = nf + nn
    w2r = w2.reshape(nf, tf, D)
    nwr = nw.reshape(1, D)
    col = lambda s: jnp.maximum(s - nf, 0)
    scratch = [pltpu.VMEM((tm, D), BF16), pltpu.VMEM((nf, tm, tf), BF16)]
    lat = pl.pallas_call(
        functools.partial(_mlp_kernel, nf=nf, rows=tm, saxis=2), grid=(B, n_lat // tm, ns),
        in_specs=[pl.BlockSpec((None, tm, D), lambda b, t, s: (b, t, 0)),
                  pl.BlockSpec((None, tm, tn), lambda b, t, s: (b, t, col(s))),
                  pl.BlockSpec((None, 6, D), lambda b, t, s: (b, 0, 0)),
                  pl.BlockSpec((None, 6, tn), lambda b, t, s: (b, 0, col(s))),
                  pl.BlockSpec((1, D), lambda b, t, s: (0, 0)),
                  pl.BlockSpec((D, tf), lambda b, t, s: (0, jnp.minimum(s, nf - 1))),
                  pl.BlockSpec((nf, tf, tn), lambda b, t, s: (0, 0, col(s)))],
        out_specs=pl.BlockSpec((None, tm, tn), lambda b, t, s: (b, t, col(s))),
        out_shape=jax.ShapeDtypeStruct((B, Lt if with_ctx else n_lat, D), F32),
        scratch_shapes=scratch,
        compiler_params=_cparams(("parallel", "parallel", "arbitrary"), 52), name="mlp_lat",
    )(X, X, mod_l, mod_l, nwr, w1, w2r)
    if not with_ctx:
        return lat
    g = tm // n_ctx
    cblk = n_lat // n_ctx
    return pl.pallas_call(
        functools.partial(_mlp_kernel, nf=nf, rows=tm, saxis=1), grid=(B // g, ns),
        in_specs=[pl.BlockSpec((g, n_ctx, D), lambda i, s: (i, cblk, 0)),
                  pl.BlockSpec((g, n_ctx, tn), lambda i, s: (i, cblk, col(s))),
                  pl.BlockSpec((None, 6, D), lambda i, s: (B, 0, 0)),
                  pl.BlockSpec((None, 6, tn), lambda i, s: (B, 0, col(s))),
                  pl.BlockSpec((1, D), lambda i, s: (0, 0)),
                  pl.BlockSpec((D, tf), lambda i, s: (0, jnp.minimum(s, nf - 1))),
                  pl.BlockSpec((nf, tf, tn), lambda i, s: (0, 0, col(s))),
                  pl.BlockSpec(memory_space=pl.ANY)],
        out_specs=pl.BlockSpec((g, n_ctx, tn), lambda i, s: (i, cblk, col(s))),
        out_shape=jax.ShapeDtypeStruct((B, Lt, D), F32),
        input_output_aliases={7: 0},
        scratch_shapes=scratch,
        compiler_params=_cparams(("parallel", "arbitrary"), 52), name="mlp_ctx",
    )(X, X, mod_l, mod_l, nwr, w1, w2r, lat)


def _pack_w_in(w_in):
    offs = [0, 512, 1024, 1536, 2048, 2560, 2576, 2960, 3088, 3152, 3664, 4176]
    u, mq, mk, mv, mo, mg, cq, ckv, kr, lx, lg = [w_in[..., offs[i]:offs[i + 1]] for i in range(11)]
    depth, d = w_in.shape[:2]
    mg = jnp.transpose(mg.reshape(depth, d, 2, 2, ML_HEADS), (0, 1, 4, 3, 2))[:, :, :, ::-1, :].reshape(depth, d, 16)
    pad = jnp.zeros((depth, d, MISC_W - MLA_ROPE - 16), w_in.dtype)
    w_p = jnp.concatenate([u, mq, mk, mv, mo, cq, ckv, lx, lg, kr, mg, pad], axis=-1).astype(BF16)
    return w_p, jnp.transpose(mg, (0, 2, 1)).astype(BF16), jnp.transpose(mv, (0, 2, 1)).astype(BF16)


def _pack_mla(w_q_up, w_kv_up):
    depth = w_q_up.shape[0]
    wq = w_q_up.reshape(depth, MLA_Q_LORA, MLA_HEADS, MLA_QK)
    wq = jnp.pad(wq, ((0, 0), (0, 0), (0, 0), (0, MLA_PAD - MLA_QK))).reshape(depth, MLA_Q_LORA, -1).astype(BF16)
    wkv = w_kv_up.reshape(depth, MLA_KV_LORA, MLA_HEADS, MLA_NOPE + MLA_V)
    wkv = jnp.concatenate([wkv[..., :MLA_NOPE].reshape(depth, MLA_KV_LORA, -1),
                           wkv[..., MLA_NOPE:].reshape(depth, MLA_KV_LORA, -1)], axis=-1).astype(BF16)
    return wq, wkv


def _pack_lru(wa, ba, wx, bx):
    wab = jnp.concatenate([wa, wx], axis=-1).astype(BF16)
    bias = jnp.stack([ba, bx], axis=2).astype(F32)
    return wab, bias


def kernel(x, c, ctx, c_ctx, ada_w, ada_b, norm1_w, norm2_w, w_in, w_out, s5_lam_re, s5_lam_im, s5_log_dt, s5_b_re, s5_b_im, s5_c_re, s5_c_im, s5_d, s5_glu_w, s5_glu_b, ml_ig_bias, ml_fg_bias, ml_out_norm, mla_q_a_norm, mla_w_q_up, mla_kv_a_norm, mla_w_kv_up, mla_q_norm, mla_k_norm, lru_conv_w, lru_conv_b, lru_wa, lru_ba, lru_wx, lru_bx, lru_lam, mlp_w1, mlp_w2):
    B, L, D = x.shape
    Lc = ctx.shape[1]
    depth = ada_w.shape[0]
    n_lat_tiles = L // TOKEN_TILE
    X = jnp.concatenate([x, ctx], axis=1).astype(F32)
    mod = _ada_all(c, c_ctx, ada_w, ada_b)
    w_p, wgt, wvt = _pack_w_in(w_in)
    wq, wkv = _pack_mla(mla_w_q_up, mla_w_kv_up)
    wab, lru_bias = _pack_lru(lru_wa, lru_ba, lru_wx, lru_bx)
    tables = _rope_tables(L, Lc)
    w_out_b, glu_w_b = w_out.astype(BF16), s5_glu_w.astype(BF16)
    w1_b, w2_b = mlp_w1.astype(BF16), mlp_w2.astype(BF16)
    for l in range(depth):
        u, mq, mk, mv, mo, cq, ckv, lx, lg, misc, mgt = _win(X, mod[l], norm1_w[l], w_p[l], wgt[l], wvt[l],
                                                             n_lat_tiles)
        s5p = _s5_params(s5_lam_re[l], s5_lam_im[l], s5_log_dt[l], s5_b_re[l], s5_b_im[l], s5_c_re[l], s5_c_im[l])
        a_tm = _s5(u.reshape(-1, GROUP_W), s5p, s5_d[l], glu_w_b[l], s5_glu_b[l], B, L)
        r_tm = _lru(lx.reshape(-1, GROUP_W), lg.reshape(-1, GROUP_W), lru_conv_w[l], lru_conv_b[l],
                    wab[l], lru_bias[l], lru_lam[l], B, L)
        b_bm = _mlstm(mq, mk, mv, mo, misc, mgt, ml_ig_bias[l], ml_fg_bias[l], ml_out_norm[l], L)
        m_bm = _mla(cq, ckv, misc, tables, mla_q_a_norm[l], wq[l], mla_kv_a_norm[l], wkv[l],
                    mla_q_norm[l], mla_k_norm[l], L)
        X = _wout(X, mod[l], a_tm.reshape(-1, B * GROUP_W), b_bm, m_bm, r_tm.reshape(-1, B * GROUP_W),
                  w_out_b[l], n_lat_tiles)
        X = _mlp(X, mod[l], norm2_w[l], w1_b[l], w2_b[l], L, with_ctx=l < depth - 1)
    return X
```

```python
import functools

import jax
import jax.numpy as jnp
from jax import lax
from jax.experimental import pallas as pl
from jax.experimental.pallas import tpu as pltpu

F32 = jnp.float32
BF16 = jnp.bfloat16
EPS = 1e-6
HIGHEST = lax.Precision.HIGHEST

GROUP_W = 512
TOKEN_TILE = 256
S5_STATE = 64
S5_CH = 16
S5_STRIPS = 4
S5_STRIP_STATES = 512
S5_MIN_NEG = -1e-4
S5_CHUNK = 64
ML_HEADS = 4
ML_DH = 128
ML_CHUNK = 128
ML_HEADS_PER_STEP = 2
MLA_HEADS = 4
MLA_NOPE = 128
MLA_ROPE = 64
MLA_QK = MLA_NOPE + MLA_ROPE
MLA_V = 128
MLA_Q_LORA = 384
MLA_KV_LORA = 128
MLA_PAD = 256
ATTN_SCALE = MLA_QK ** -0.5
ROPE_BASE = 10000.0
GRID_W = 64
Q_TILE = 1024
Q_SUB = 256
LRU_BLOCKS = 4
LRU_BLOCK_W = 128
LRU_CONV = 4
LRU_C = 8.0
MLP_FF_TILE = 1024
MLP_OUT_TILE = 512
MLP_ROWS = 512

COL_U, COL_MQ, COL_MK, COL_MV, COL_MO = 0, 512, 1024, 1536, 2048
COL_CQ, COL_CKV, COL_LX, COL_LG, COL_MISC = 2560, 2944, 3072, 3584, 4096
IN_COLS_PACKED = 4224
MISC_W = 128


def _cparams(sem, vmem_mb):
    return pltpu.CompilerParams(dimension_semantics=sem, vmem_limit_bytes=vmem_mb << 20)


def _norm_mod(x, nw, shift, scale):
    ms = jnp.mean(x * x, axis=-1, keepdims=True)
    y = (x * lax.rsqrt(ms + EPS)) * nw
    return (y * (1.0 + scale) + shift).astype(BF16)


def _log_sigmoid(x):
    return jnp.minimum(x, 0.0) - jnp.log1p(jnp.exp(-jnp.abs(x)))


def _softplus(x):
    return jnp.maximum(x, 0.0) + jnp.log1p(jnp.exp(-jnp.abs(x)))


def _ada_kernel(c_ref, w_ref, b_ref, o_ref):
    c = c_ref[...]
    act = (c * jax.nn.sigmoid(c)).astype(BF16)
    o_ref[...] = jnp.dot(act, w_ref[...].astype(BF16), preferred_element_type=F32) + b_ref[...]


def _ada_all(c, c_ctx, ada_w, ada_b):
    depth, d, n6 = ada_w.shape
    nb = c.shape[0]
    rows = nb + 8
    cc = jnp.concatenate([c.astype(F32), jnp.broadcast_to(c_ctx.astype(F32)[None], (8, d))], axis=0)
    tn = 1024
    out = pl.pallas_call(
        _ada_kernel,
        grid=(depth, n6 // tn),
        in_specs=[pl.BlockSpec((rows, d), lambda l, n: (0, 0)),
                  pl.BlockSpec((None, d, tn), lambda l, n: (l, 0, n)),
                  pl.BlockSpec((None, 1, tn), lambda l, n: (l, 0, n))],
        out_specs=pl.BlockSpec((None, rows, tn), lambda l, n: (l, 0, n)),
        out_shape=jax.ShapeDtypeStruct((depth, rows, n6), F32),
        compiler_params=_cparams(("parallel", "parallel"), 40),
        name="ada_mod",
    )(cc, ada_w, ada_b.reshape(depth, 1, n6))
    return out.reshape(depth, rows, 6, d)


def _win_kernel(x_ref, mod_ref, nw_ref, w_ref, wgt_ref, wvt_ref,
                u_ref, mq_ref, mk_ref, mvt_ref, mo_ref, cq_ref, ckv_ref, lx_ref, lg_ref, misc_ref, mgt_ref):
    h = _norm_mod(x_ref[...], nw_ref[...], mod_ref[0:1, :], mod_ref[1:2, :])
    nt_dims = (((1,), (1,)), ((), ()))

    def mm(lo, width):
        return jnp.dot(h, w_ref[:, lo:lo + width], preferred_element_type=F32)

    u_ref[...] = mm(COL_U, GROUP_W)
    mq_ref[...] = mm(COL_MQ, GROUP_W).astype(BF16)
    mk_ref[...] = (mm(COL_MK, GROUP_W) * (ML_DH ** -0.5)).astype(BF16)
    mvt = lax.dot_general(wvt_ref[...], h, nt_dims, preferred_element_type=F32).astype(BF16)
    for cc in range(mvt_ref.shape[0]):
        mvt_ref[cc] = mvt[:, cc * ML_CHUNK:(cc + 1) * ML_CHUNK]
    mo_ref[...] = mm(COL_MO, GROUP_W)
    cq_ref[...] = mm(COL_CQ, MLA_Q_LORA)
    ckv_ref[...] = mm(COL_CKV, MLA_KV_LORA)
    lx_ref[...] = mm(COL_LX, GROUP_W)
    lg_ref[...] = mm(COL_LG, GROUP_W)
    misc_ref[...] = mm(COL_MISC, MISC_W)
    mgt_ref[...] = lax.dot_general(wgt_ref[...], h, nt_dims, preferred_element_type=F32)


def _win(X, mod_l, nw, w_p, wgt, wvt, n_lat_tiles):
    B, Lt, D = X.shape
    tm = TOKEN_TILE
    nt = Lt // tm
    bm = lambda b, t: (b, t, 0)
    in_specs = [
        pl.BlockSpec((None, tm, D), bm),
        pl.BlockSpec((None, 6, D), lambda b, t: (jnp.where(t >= n_lat_tiles, B, b), 0, 0)),
        pl.BlockSpec((1, D), lambda b, t: (0, 0)),
        pl.BlockSpec((D, IN_COLS_PACKED), lambda b, t: (0, 0), pipeline_mode=pl.Buffered(1)),
        pl.BlockSpec((16, D), lambda b, t: (0, 0)),
        pl.BlockSpec((GROUP_W, D), lambda b, t: (0, 0)),
    ]
    cpt = tm // ML_CHUNK
    out_specs = [
        pl.BlockSpec((None, tm, GROUP_W), bm),
        pl.BlockSpec((None, tm, GROUP_W), bm),
        pl.BlockSpec((None, tm, GROUP_W), bm),
        pl.BlockSpec((None, cpt, GROUP_W, ML_CHUNK), lambda b, t: (b, t, 0, 0)),
        pl.BlockSpec((None, tm, GROUP_W), bm),
        pl.BlockSpec((None, tm, MLA_Q_LORA), bm),
        pl.BlockSpec((None, tm, MLA_KV_LORA), bm),
        pl.BlockSpec((None, tm, GROUP_W), bm),
        pl.BlockSpec((None, tm, GROUP_W), bm),
        pl.BlockSpec((None, tm, MISC_W), bm),
        pl.BlockSpec((None, 16, tm), lambda b, t: (b, 0, t)),
    ]
    sds = jax.ShapeDtypeStruct
    out_shape = [
        sds((B, Lt, GROUP_W), F32),
        sds((B, Lt, GROUP_W), BF16), sds((B, Lt, GROUP_W), BF16),
        sds((B, Lt // ML_CHUNK, GROUP_W, ML_CHUNK), BF16),
        sds((B, Lt, GROUP_W), F32),
        sds((B, Lt, MLA_Q_LORA), F32), sds((B, Lt, MLA_KV_LORA), F32),
        sds((B, Lt, GROUP_W), F32), sds((B, Lt, GROUP_W), F32),
        sds((B, Lt, MISC_W), F32),
        sds((B, 16, Lt), F32),
    ]
    return pl.pallas_call(
        _win_kernel, grid=(B, nt), in_specs=in_specs, out_specs=out_specs, out_shape=out_shape,
        compiler_params=_cparams(("parallel", "parallel"), 52), name="in_proj",
    )(X, mod_l, nw.reshape(1, D), w_p, wgt, wvt)


def _s5_kernel(*refs, tc, nb, reverse, final):
    if final:
        ub_ref, yp_ref, wd_ref, a_ref, wc_ref, dsk_ref, gw_ref, gb_ref, o_ref, u_ref, buf, ybuf, o3, st = refs
    else:
        ub_ref, wd_ref, a_ref, wc_ref, o_ref, u_ref, buf, st = refs
    R = tc * nb
    RB = min(R, 512)
    NS = S5_STRIP_STATES
    nh = nb // 8

    @pl.when(pl.program_id(0) == 0)
    def _():
        st[...] = jnp.zeros_like(st)

    for t in range(tc):
        u_ref[t * nb:(t + 1) * nb, :] = ub_ref[:, t, :]

    for s in range(S5_STRIPS):
        cs = slice(s * 128, (s + 1) * 128)
        for rb in range(R // RB):
            rs = slice(rb * RB, (rb + 1) * RB)
            buf[rs, :] = jnp.dot(u_ref[rs, cs].astype(BF16), wd_ref[s], preferred_element_type=F32)
        ar = jnp.broadcast_to(a_ref[s, 0:1, :], (8, NS))
        ai = jnp.broadcast_to(a_ref[s, 1:2, :], (8, NS))
        init = []
        for hh in range(nh):
            init += [st[s, hh * 8:(hh + 1) * 8, 0:NS], st[s, hh * 8:(hh + 1) * 8, NS:2 * NS]]

        def body(k, carry, ar=ar, ai=ai):
            t = (tc - 1 - k) if reverse else k
            out = []
            for hh in range(nh):
                sr, si = carry[2 * hh], carry[2 * hh + 1]
                row = pl.multiple_of(t * nb + hh * 8, 8)
                br = buf[pl.ds(row, 8), 0:NS]
                bi = buf[pl.ds(row, 8), NS:2 * NS]
                nsr = ar * sr - ai * si + br
                nsi = ar * si + ai * sr + bi
                buf[pl.ds(row, 8), 0:NS] = nsr
                buf[pl.ds(row, 8), NS:2 * NS] = nsi
                out += [nsr, nsi]
            return tuple(out)

        fin = lax.fori_loop(0, tc, body, tuple(init), unroll=2)
        for hh in range(nh):
            st[s, hh * 8:(hh + 1) * 8, 0:NS] = fin[2 * hh]
            st[s, hh * 8:(hh + 1) * 8, NS:2 * NS] = fin[2 * hh + 1]
        for rb in range(R // RB):
            rs = slice(rb * RB, (rb + 1) * RB)
            y = jnp.dot(buf[rs, :].astype(BF16), wc_ref[s], preferred_element_type=F32)
            if final:
                ybuf[rs, cs] = y
            else:
                o_ref[rs, cs] = y

    if final:
        for rb in range(R // RB):
            rs = slice(rb * RB, (rb + 1) * RB)
            y = ybuf[rs, :] + yp_ref[rs, :] + dsk_ref[...] * u_ref[rs, :]
            g = jax.nn.gelu(y)
            z = jnp.dot(g.astype(BF16), gw_ref[...], preferred_element_type=F32) + gb_ref[...]
            res = g * jax.nn.sigmoid(z)
            for tt in range(RB // nb):
                o3[:, rb * (RB // nb) + tt, :] = res[tt * nb:(tt + 1) * nb, :]
        o_ref[...] = o3[...].astype(BF16)


def _chunk_order(nc, nlc, reverse):
    if reverse:
        return lambda i: nc - 1 - i
    return lambda i: lax.rem(i + nlc, nc)


def _s5(u, p, d_skip, glu_w, glu_b, n_lat):
    nb, lt, _ = u.shape
    rows = nb * lt
    tc = S5_CHUNK
    R = tc * nb
    nc, nlc = lt // tc, n_lat // tc
    wd, a, wc = p
    outs = None
    for d in range(2):
        final = d == 1
        cidx = _chunk_order(nc, nlc, reverse=final)
        row_spec = pl.BlockSpec((R, GROUP_W), lambda i: (cidx(i), 0))
        bt_spec = pl.BlockSpec((nb, tc, GROUP_W), lambda i: (0, cidx(i), 0))
        full = lambda shape: pl.BlockSpec(shape, lambda i: (0,) * len(shape))
        in_specs = [bt_spec] + ([row_spec] if final else []) + [
            full((S5_STRIPS, 128, 2 * S5_STRIP_STATES)), full((S5_STRIPS, 2, S5_STRIP_STATES)),
            full((S5_STRIPS, 2 * S5_STRIP_STATES, 128))]
        args = [u] + ([outs] if final else []) + [wd[d], a[d], wc[d]]
        scratch = [pltpu.VMEM((R, GROUP_W), F32), pltpu.VMEM((R, 2 * S5_STRIP_STATES), F32)]
        if final:
            in_specs += [full((1, GROUP_W)), full((GROUP_W, GROUP_W)), full((1, GROUP_W))]
            args += [d_skip.reshape(1, GROUP_W), glu_w, glu_b.reshape(1, GROUP_W)]
            scratch += [pltpu.VMEM((R, GROUP_W), F32), pltpu.VMEM((nb, tc, GROUP_W), F32)]
        scratch += [pltpu.VMEM((S5_STRIPS, nb, 2 * S5_STRIP_STATES), F32)]
        outs = pl.pallas_call(
            functools.partial(_s5_kernel, tc=tc, nb=nb, reverse=final, final=final),
            grid=(nc,), in_specs=in_specs, out_specs=bt_spec if final else row_spec,
            out_shape=jax.ShapeDtypeStruct((nb, lt, GROUP_W), BF16) if final
            else jax.ShapeDtypeStruct((rows, GROUP_W), F32),
            scratch_shapes=scratch,
            compiler_params=_cparams(("arbitrary",), 48), name="s5_rev" if final else "s5_fwd",
        )(*args)
    return outs


def _s5_params(lam_re, lam_im, log_dt, b_re, b_im, c_re, c_im):
    lr = jnp.minimum(lam_re.astype(F32), S5_MIN_NEG)
    li = lam_im.astype(F32)
    dt = jnp.exp(log_dt.astype(F32))[..., None]
    mag = jnp.exp(lr * dt)
    ar, ai = mag * jnp.cos(li * dt), mag * jnp.sin(li * dt)
    den = lr * lr + li * li
    fr = ((ar - 1.0) * lr + ai * li) / den
    fi = (ai * lr - (ar - 1.0) * li) / den
    bbr = fr[..., None] * b_re - fi[..., None] * b_im
    bbi = fr[..., None] * b_im + fi[..., None] * b_re
    gs = 128 // S5_CH
    eye = jnp.eye(gs, dtype=F32)

    def drive(bb):
        t = bb.reshape(2, S5_STRIPS, gs, S5_STATE, S5_CH)
        return jnp.einsum('dsgpc,gh->dsgchp', t, eye).reshape(2, S5_STRIPS, 128, S5_STRIP_STATES)

    def read(cc):
        t = cc.reshape(2, S5_STRIPS, gs, S5_CH, S5_STATE)
        return jnp.einsum('dsgcp,gh->dsgphc', t, eye).reshape(2, S5_STRIPS, S5_STRIP_STATES, 128)

    wd = jnp.concatenate([drive(bbr), drive(bbi)], axis=-1).astype(BF16)
    wc = jnp.concatenate([read(c_re.astype(F32)), read(-c_im.astype(F32))], axis=-2).astype(BF16)
    a = jnp.stack([ar.reshape(2, S5_STRIPS, S5_STRIP_STATES), ai.reshape(2, S5_STRIPS, S5_STRIP_STATES)], axis=2)
    return wd, a, wc


def _lru_kernel(*refs, tc, nb, nlc, nc, reverse, final):
    if final:
        (x_ref, xp_ref, xn_ref, hp_ref, g_ref, cw_ref, cb_ref, w_ref, bias_ref, lam_ref,
         o_ref, xpad, abuf, bbuf, o3, st) = refs
    else:
        x_ref, xp_ref, xn_ref, cw_ref, cb_ref, w_ref, bias_ref, lam_ref, o_ref, xpad, abuf, bbuf, st = refs
    i = pl.program_id(0)
    c = (nc - 1 - i) if reverse else lax.rem(i + nlc, nc)
    R = tc * nb
    RB = min(R, 256)
    nh = nb // 8
    left = LRU_CONV // 2
    halo = xp_ref.shape[1]

    @pl.when(i == 0)
    def _():
        st[...] = jnp.zeros_like(st)

    starts = jnp.logical_or(c == 0, c == nlc)
    ends = jnp.logical_or(c == nlc - 1, c == nc - 1)
    for j in range(left):
        xpad[j * nb:(j + 1) * nb, :] = jnp.where(starts, 0.0, xp_ref[:, halo - left + j, :])
    for t in range(tc):
        xpad[(left + t) * nb:(left + t + 1) * nb, :] = x_ref[:, t, :]
    for j in range(LRU_CONV - 1 - left):
        xpad[(left + tc + j) * nb:(left + tc + j + 1) * nb, :] = jnp.where(ends, 0.0, xn_ref[:, j, :])
    sp = _softplus(-lam_ref[...])

    for rb in range(R // RB):
        r0 = rb * RB
        xs = cb_ref[...]
        for j in range(LRU_CONV):
            xs = xs + xpad[r0 + j * nb:r0 + j * nb + RB, :] * cw_ref[j:j + 1, :]
        xb = xs.astype(BF16)
        for n in range(LRU_BLOCKS):
            cs = slice(n * LRU_BLOCK_W, (n + 1) * LRU_BLOCK_W)
            z = jnp.dot(xb[:, cs], w_ref[n], preferred_element_type=F32)
            r = jax.nn.sigmoid(z[:, :LRU_BLOCK_W] + bias_ref[0:1, cs])
            ig = jax.nn.sigmoid(z[:, LRU_BLOCK_W:] + bias_ref[1:2, cs])
            log_a = (-LRU_C * r) * sp[:, cs]
            abuf[r0:r0 + RB, cs] = jnp.exp(log_a)
            th = jnp.tanh(log_a)
            bbuf[r0:r0 + RB, cs] = jnp.sqrt(-2.0 * th / (1.0 - th)) * (ig * xs[:, cs])

    init = tuple(st[hh * 8:(hh + 1) * 8, :] for hh in range(nh))

    def body(k, carry):
        t = (tc - 1 - k) if reverse else k
        out = []
        for hh in range(nh):
            row = pl.multiple_of(t * nb + hh * 8, 8)
            hnew = abuf[pl.ds(row, 8), :] * carry[hh] + bbuf[pl.ds(row, 8), :]
            bbuf[pl.ds(row, 8), :] = hnew
            out.append(hnew)
        return tuple(out)

    fin = lax.fori_loop(0, tc, body, init, unroll=4)
    for hh in range(nh):
        st[hh * 8:(hh + 1) * 8, :] = fin[hh]
    if final:
        for t in range(tc):
            rows = slice(t * nb, (t + 1) * nb)
            o3[:, t, :] = hp_ref[rows, :] + bbuf[rows, :]
        o_ref[...] = (o3[...] * jax.nn.gelu(g_ref[...])).astype(BF16)
    else:
        o_ref[...] = bbuf[...]


def _lru(lx, lg, conv_w, conv_b, wab, bias, lam, n_lat):
    nb, lt, _ = lx.shape
    rows = nb * lt
    tc = S5_CHUNK
    R = tc * nb
    nc, nlc = lt // tc, n_lat // tc
    halo = 8
    outs = None
    for d in range(2):
        final = d == 1
        cidx = _chunk_order(nc, nlc, reverse=final)
        row_spec = pl.BlockSpec((R, GROUP_W), lambda i: (cidx(i), 0))
        bt_spec = pl.BlockSpec((nb, tc, GROUP_W), lambda i: (0, cidx(i), 0))
        prev_spec = pl.BlockSpec((nb, halo, GROUP_W),
                                 lambda i: (0, jnp.maximum(cidx(i) * (tc // halo) - 1, 0), 0))
        next_spec = pl.BlockSpec((nb, halo, GROUP_W),
                                 lambda i: (0, jnp.minimum((cidx(i) + 1) * (tc // halo), lt // halo - 1), 0))
        full = lambda shape: pl.BlockSpec(shape, lambda i: (0,) * len(shape))
        in_specs = [bt_spec, prev_spec, next_spec] + ([row_spec, bt_spec] if final else []) + [
            full((LRU_CONV, GROUP_W)), full((1, GROUP_W)), full((LRU_BLOCKS, LRU_BLOCK_W, 2 * LRU_BLOCK_W)),
            full((2, GROUP_W)), full((1, GROUP_W))]
        args = [lx, lx, lx] + ([outs, lg] if final else []) + [
            conv_w, conv_b.reshape(1, GROUP_W), wab[d], bias[d], lam[d].reshape(1, GROUP_W)]
        outs = pl.pallas_call(
            functools.partial(_lru_kernel, tc=tc, nb=nb, nlc=nlc, nc=nc, reverse=final, final=final),
            grid=(nc,), in_specs=in_specs, out_specs=bt_spec if final else row_spec,
            out_shape=jax.ShapeDtypeStruct((nb, lt, GROUP_W), BF16) if final
            else jax.ShapeDtypeStruct((rows, GROUP_W), F32),
            scratch_shapes=[pltpu.VMEM((R + 3 * nb, GROUP_W), F32), pltpu.VMEM((R, GROUP_W), F32),
                            pltpu.VMEM((R, GROUP_W), F32)]
            + ([pltpu.VMEM((nb, tc, GROUP_W), F32)] if final else [])
            + [pltpu.VMEM((nb, GROUP_W), F32)],
            compiler_params=_cparams(("arbitrary",), 48), name="lru_rev" if final else "lru_fwd",
        )(*args)
    return outs


def _mlstm_kernel(q_ref, k_ref, vt_ref, o_ref, gr_ref, gc_ref, br_ref, bc_ref, nw_ref, out_ref,
                  xb_scr, row_scr, nt_scr, u_scr, vec_scr, *, nch, nlat, hp):
    T = ML_CHUNK
    nt_dims = (((1,), (1,)), ((), ()))
    rid = lax.broadcasted_iota(jnp.int32, (T, T), 0)
    cid = lax.broadcasted_iota(jnp.int32, (T, T), 1)
    incl_rows = (rid <= cid).astype(F32)
    incl_cols = (rid >= cid).astype(F32)

    for j in range(hp):
        g = (gr_ref[j] + br_ref[j]).reshape(nch * 8, T)
        kind = lax.broadcasted_iota(jnp.int32, (nch * 8, T), 0) % 8
        lf = _log_sigmoid(g)
        pre = jnp.dot(lf, incl_rows, precision=HIGHEST, preferred_element_type=F32)
        b = jnp.where(kind == 1, pre[:, T - 1:T] - pre + lf, pre)
        x = g - pltpu.roll(b, 2, axis=0)
        row_scr[j] = jnp.where(kind < 2, b, x).reshape(nch, 8, T)
        gl = gc_ref[j] + bc_ref[j]
        kind = lax.broadcasted_iota(jnp.int32, (T, 128), 1) % 4
        lf = _log_sigmoid(gl)
        pre = jnp.dot(incl_cols, lf, precision=HIGHEST, preferred_element_type=F32)
        b = jnp.where(kind == 1, pre[T - 1:T, :] - pre + lf, pre)
        x = gl - pltpu.roll(b, 2, axis=1)
        for c in range(nch):
            for d in range(2):
                lane = c * 4 + 2 + d
                xb_scr[j, d, c] = jnp.broadcast_to(x[:, lane:lane + 1], (T, T))

    def independent(c, carry):
        rows = pl.ds(pl.multiple_of(c * T, T), T)
        pairs = [(j, d) for j in range(hp) for d in range(2)]
        hs = lambda j: slice(j * ML_DH, (j + 1) * ML_DH)
        swept = lambda d: (rid <= cid) if d == 0 else (rid >= cid)
        kq = {j: lax.dot_general(k_ref[rows, hs(j)], q_ref[rows, hs(j)], nt_dims, preferred_element_type=F32)
              for j in range(hp)}
        xm = {p: jnp.where(swept(p[1]), xb_scr[p[0], p[1], c], -jnp.inf) for p in pairs}
        a_row = {p: jnp.max(xm[p], axis=0, keepdims=True) for p in pairs}
        w0 = {}
        for j, d in pairs:
            a_last = a_row[j, d][:, T - 1:T] if d == 0 else a_row[j, d][:, 0:1]
            w0[j, d] = jnp.exp(row_scr[j, c, 2 + d:3 + d, :] - a_last)
        s0 = {p: kq[p[0]] * jnp.exp(xm[p] - a_row[p]) for p in pairs}
        for j, d in pairs:
            vt, kc = vt_ref[c, hs(j), :], k_ref[rows, hs(j)]
            nt_scr[j, d, c] = jnp.dot(vt, s0[j, d].astype(BF16), preferred_element_type=F32)
            wv = (vt.astype(F32) * w0[j, d]).astype(BF16)
            u_scr[j, d, c] = jnp.dot(wv, kc, preferred_element_type=F32)
            vec_scr[j, d, c, 0:1, :] = a_row[j, d]
            vec_scr[j, d, c, 1:2, :] = jnp.sum(s0[j, d], axis=0, keepdims=True)
            vec_scr[j, d, c, 2:3, :] = jnp.dot(jnp.broadcast_to(w0[j, d], (8, T)).astype(BF16), kc,
                                               preferred_element_type=F32)[0:1, :]
        return carry

    lax.fori_loop(0, nch, independent, 0)

    def pass2(c, d, j, state):
        cm, n, m = state
        hs = slice(j * ML_DH, (j + 1) * ML_DH)
        qc = q_ref[pl.ds(pl.multiple_of(c * T, T), T), hs]
        a_row, d0, n0 = vec_scr[j, d, c, 0:1, :], vec_scr[j, d, c, 1:2, :], vec_scr[j, d, c, 2:3, :]
        b_row = row_scr[j, c, d:d + 1, :]
        mu = jnp.maximum(m, a_row)
        inter, r = jnp.exp(m - mu), jnp.exp(a_row - mu)
        qct = lax.dot_general(cm.astype(BF16), qc, nt_dims, preferred_element_type=F32)
        qn = lax.dot_general(jnp.broadcast_to(n, (8, ML_DH)).astype(BF16), qc, nt_dims,
                             preferred_element_type=F32)[0:1, :]
        den = inter * qn + r * d0
        inv = 1.0 / jnp.maximum(jnp.abs(den), jnp.exp(-b_row - mu))
        nt_scr[j, d, c] = (inter * qct + r * nt_scr[j, d, c]) * inv
        a_last = a_row[:, T - 1:T] if d == 0 else a_row[:, 0:1]
        b_last = b_row[:, T - 1:T] if d == 0 else b_row[:, 0:1]
        mul = jnp.maximum(m, a_last)
        decay, rl = jnp.exp(m - mul), jnp.exp(a_last - mul)
        return decay * cm + rl * u_scr[j, d, c], decay * n + rl * n0, b_last + mul

    def states(i, carry):
        out = []
        for j in range(hp):
            out.append(pass2(lax.rem(i + nlat, nch), 0, j, carry[2 * j]))
            out.append(pass2(nch - 1 - i, 1, j, carry[2 * j + 1]))
        return tuple(out)

    zero = (jnp.zeros((ML_DH, ML_DH), F32), jnp.zeros((1, ML_DH), F32), jnp.zeros((1, 1), F32))
    lax.fori_loop(0, nch, states, (zero,) * (2 * hp))

    for c in range(nch):
        rows = slice(c * T, (c + 1) * T)
        for j in range(hp):
            hs = slice(j * ML_DH, (j + 1) * ML_DH)
            ht = nt_scr[j, 0, c] + nt_scr[j, 1, c]
            hn = ht * lax.rsqrt(jnp.mean(ht * ht, axis=0, keepdims=True) + EPS) * nw_ref[j]
            out_ref[rows, hs] = (hn.T * jax.nn.sigmoid(o_ref[rows, hs])).astype(BF16)


def _mlstm(mq, mk, mvt, mo, misc, mgt, ig_bias, fg_bias, out_norm, n_lat):
    B, Lt, _ = mq.shape
    T = ML_CHUNK
    nch, nlat = Lt // T, n_lat // T
    hp = ML_HEADS_PER_STEP
    assert 4 * nch <= 128
    gr = jnp.transpose(mgt.reshape(B, ML_HEADS, 4, nch, T), (0, 1, 3, 2, 4))
    gr = jnp.pad(gr, ((0, 0), (0, 0), (0, 0), (0, 4), (0, 0)))
    gc = misc[:, :, MLA_ROPE:MLA_ROPE + 16].reshape(B, nch, T, ML_HEADS, 4)
    gc = jnp.transpose(gc, (0, 3, 2, 1, 4)).reshape(B, ML_HEADS, T, nch * 4)
    gc = jnp.pad(gc, ((0, 0), (0, 0), (0, 0), (0, 128 - nch * 4)))
    kinds = jnp.concatenate([fg_bias.astype(F32), ig_bias.astype(F32)], axis=0).T
    br = jnp.broadcast_to(jnp.pad(kinds, ((0, 0), (0, 4)))[:, :, None], (ML_HEADS, 8, T))
    bc = jnp.pad(jnp.tile(kinds, (1, nch)), ((0, 0), (0, 128 - nch * 4)))[:, None, :]
    nwb = jnp.broadcast_to(out_norm.astype(F32)[:, :, None], (ML_HEADS, ML_DH, T))
    head_spec = pl.BlockSpec((None, Lt, hp * ML_DH), lambda b, h: (b, 0, h))
    slab = lambda: pltpu.VMEM((hp, 2, nch, T, T), F32)
    return pl.pallas_call(
        functools.partial(_mlstm_kernel, nch=nch, nlat=nlat, hp=hp),
        grid=(B, ML_HEADS // hp),
        in_specs=[head_spec, head_spec,
                  pl.BlockSpec((None, nch, hp * ML_DH, T), lambda b, h: (b, 0, h, 0)),
                  head_spec,
                  pl.BlockSpec((None, hp, nch, 8, T), lambda b, h: (b, h, 0, 0, 0)),
                  pl.BlockSpec((None, hp, T, 128), lambda b, h: (b, h, 0, 0)),
                  pl.BlockSpec((hp, 8, T), lambda b, h: (h, 0, 0)),
                  pl.BlockSpec((hp, 1, 128), lambda b, h: (h, 0, 0)),
                  pl.BlockSpec((hp, ML_DH, T), lambda b, h: (h, 0, 0))],
        out_specs=head_spec,
        out_shape=jax.ShapeDtypeStruct((B, Lt, GROUP_W), BF16),
        scratch_shapes=[slab(), pltpu.VMEM((hp, nch, 8, T), F32), slab(), slab(),
                        pltpu.VMEM((hp, 2, nch, 8, T), F32)],
        compiler_params=_cparams(("parallel", "parallel"), 52), name="mlstm",
    )(mq, mk, mvt, mo, gr, gc, br, bc, nwb)


def _mla_prep_kernel(cq_ref, ckv_ref, misc_ref, cos_ref, sin_ref, qan_ref, wq_ref, kvan_ref, wkv_ref,
                     qn_ref, kn_ref, qp_ref, kp_ref, v_ref):
    def rms(x, w):
        return (x * lax.rsqrt(jnp.mean(x * x, axis=-1, keepdims=True) + EPS)) * w

    tm = cq_ref.shape[0]
    cos, sins = cos_ref[...], sin_ref[...]
    lane = lax.broadcasted_iota(jnp.int32, (tm, 128), 1)
    first = (lane % (MLA_ROPE // 2)) < (MLA_ROPE // 4)

    def rope(x):
        sw = jnp.where(first, pltpu.roll(x, 128 - MLA_ROPE // 4, axis=1), pltpu.roll(x, MLA_ROPE // 4, axis=1))
        return x * cos + sw * sins

    hw = MLA_HEADS * MLA_NOPE
    q_all = jnp.dot(rms(cq_ref[...], qan_ref[...]).astype(BF16), wq_ref[...], preferred_element_type=F32)
    kv = jnp.dot(rms(ckv_ref[...], kvan_ref[...]).astype(BF16), wkv_ref[...], preferred_element_type=F32)
    kr = jnp.where(lane < MLA_ROPE, misc_ref[...], 0.0)
    kr_sq = jnp.sum(kr * kr, axis=-1, keepdims=True)
    v_ref[...] = kv[:, hw:].astype(BF16)
    qw_a, qw_b = qn_ref[:, 0:128] * ATTN_SCALE, qn_ref[:, 128:256] * ATTN_SCALE
    kw_a, kw_b = kn_ref[:, 0:128], kn_ref[:, 128:256]
    for h in range(MLA_HEADS):
        base = h * MLA_PAD
        qa, qb = q_all[:, base:base + 128], q_all[:, base + 128:base + 256]
        inv = lax.rsqrt(jnp.sum(qa * qa + qb * qb, axis=-1, keepdims=True) / MLA_QK + EPS)
        qp_ref[:, base:base + 128] = ((qa * inv) * qw_a).astype(BF16)
        qp_ref[:, base + 128:base + 256] = rope((qb * inv) * qw_b).astype(BF16)
        ka = kv[:, h * MLA_NOPE:(h + 1) * MLA_NOPE]
        inv = lax.rsqrt((jnp.sum(ka * ka, axis=-1, keepdims=True) + kr_sq) / MLA_QK + EPS)
        kp_ref[:, base:base + 128] = ((ka * inv) * kw_a).astype(BF16)
        kp_ref[:, base + 128:base + 256] = rope((kr * inv) * kw_b).astype(BF16)


def _attn_kernel(q_ref, k_ref, v_ref, o_ref, *, n_lat, n_lat_tiles, sub):
    def attend(n_sub, k, v):
        for first in range(0, n_sub, 2):
            blocks = [slice(i * sub, (i + 1) * sub) for i in range(first, min(first + 2, n_sub))]
            s = [lax.dot_general(q_ref[r, :], k, (((1,), (1,)), ((), ())), preferred_element_type=F32)
                 for r in blocks]
            p = [jnp.exp(si - jnp.max(si, axis=-1, keepdims=True)) for si in s]
            for r, pi in zip(blocks, p):
                den = jnp.sum(pi, axis=-1, keepdims=True)
                o_ref[r, :] = (jnp.dot(pi.astype(BF16), v, preferred_element_type=F32) / den).astype(BF16)

    qi = pl.program_id(2)

    @pl.when(qi < n_lat_tiles)
    def _():
        attend(q_ref.shape[0] // sub, k_ref[...], v_ref[...])

    @pl.when(qi >= n_lat_tiles)
    def _():
        attend((k_ref.shape[0] - n_lat) // sub, k_ref[n_lat:, :], v_ref[n_lat:, :])


def _rope_tables(n_lat, n_ctx):
    n_rows = n_lat // GRID_W
    rows = jnp.repeat(jnp.arange(n_rows, dtype=F32), GRID_W)
    cols = jnp.tile(jnp.arange(GRID_W, dtype=F32), n_rows)
    n_freq = MLA_ROPE // 4
    inv_freq = ROPE_BASE ** (-jnp.arange(n_freq, dtype=F32) / n_freq)
    ar, ac = rows[:, None] * inv_freq, cols[:, None] * inv_freq
    cos = jnp.concatenate([jnp.cos(ar), jnp.cos(ar), jnp.cos(ac), jnp.cos(ac)], axis=1)
    sins = jnp.concatenate([-jnp.sin(ar), jnp.sin(ar), -jnp.sin(ac), jnp.sin(ac)], axis=1)
    cos = jnp.pad(cos, ((0, n_ctx), (0, 128 - MLA_ROPE)), constant_values=1.0)
    sins = jnp.pad(sins, ((0, n_ctx), (0, 128 - MLA_ROPE)))
    return cos, sins


def _mla(cq, ckv, misc, tables, qan, wq, kvan, wkv, qn, kn, n_lat):
    B, Lt, _ = cq.shape
    tm = TOKEN_TILE
    bm = lambda b, t: (b, t, 0)
    full = lambda shape: pl.BlockSpec(shape, lambda b, t: (0,) * len(shape))
    hp = MLA_HEADS * MLA_PAD
    qp, kp, v = pl.pallas_call(
        _mla_prep_kernel, grid=(B, Lt // tm),
        in_specs=[pl.BlockSpec((None, tm, MLA_Q_LORA), bm), pl.BlockSpec((None, tm, MLA_KV_LORA), bm),
                  pl.BlockSpec((None, tm, MISC_W), bm),
                  pl.BlockSpec((tm, 128), lambda b, t: (t, 0)), pl.BlockSpec((tm, 128), lambda b, t: (t, 0)),
                  full((1, MLA_Q_LORA)), full((MLA_Q_LORA, hp)),
                  full((1, MLA_KV_LORA)), full((MLA_KV_LORA, MLA_HEADS * (MLA_NOPE + MLA_V))),
                  full((1, MLA_PAD)), full((1, MLA_PAD))],
        out_specs=[pl.BlockSpec((None, tm, hp), bm), pl.BlockSpec((None, tm, hp), bm),
                   pl.BlockSpec((None, tm, GROUP_W), bm)],
        out_shape=[jax.ShapeDtypeStruct((B, Lt, hp), BF16), jax.ShapeDtypeStruct((B, Lt, hp), BF16),
                   jax.ShapeDtypeStruct((B, Lt, GROUP_W), BF16)],
        compiler_params=_cparams(("parallel", "parallel"), 40), name="mla_prep",
    )(cq, ckv, misc, tables[0], tables[1], qan.reshape(1, -1), wq, kvan.reshape(1, -1), wkv,
      jnp.pad(qn.astype(F32), (0, MLA_PAD - MLA_QK)).reshape(1, -1),
      jnp.pad(kn.astype(F32), (0, MLA_PAD - MLA_QK)).reshape(1, -1))
    tq = min(Q_TILE, n_lat)
    return pl.pallas_call(
        functools.partial(_attn_kernel, n_lat=n_lat, n_lat_tiles=n_lat // tq, sub=Q_SUB),
        grid=(B, MLA_HEADS, pl.cdiv(Lt, tq)),
        in_specs=[pl.BlockSpec((None, tq, MLA_PAD), lambda b, h, t: (b, t, h)),
                  pl.BlockSpec((None, Lt, MLA_PAD), lambda b, h, t: (b, 0, h)),
                  pl.BlockSpec((None, Lt, MLA_V), lambda b, h, t: (b, 0, h))],
        out_specs=pl.BlockSpec((None, tq, MLA_V), lambda b, h, t: (b, t, h)),
        out_shape=jax.ShapeDtypeStruct((B, Lt, GROUP_W), BF16),
        compiler_params=_cparams(("parallel", "parallel", "arbitrary"), 40), name="mla_attn",
    )(qp, kp, v)


def _wout_kernel(x_ref, mod_ref, a_ref, b_ref, m_ref, r_ref, w_ref, o_ref):
    acc = jnp.dot(a_ref[...], w_ref[0:GROUP_W, :], preferred_element_type=F32)
    acc += jnp.dot(b_ref[...], w_ref[GROUP_W:2 * GROUP_W, :], preferred_element_type=F32)
    acc += jnp.dot(m_ref[...], w_ref[2 * GROUP_W:3 * GROUP_W, :], preferred_element_type=F32)
    acc += jnp.dot(r_ref[...], w_ref[3 * GROUP_W:4 * GROUP_W, :], preferred_element_type=F32)
    o_ref[...] = x_ref[...] + mod_ref[2:3, :] * acc


def _wout(X, mod_l, a, b, m, r, w, n_lat_tiles):
    B, Lt, D = X.shape
    tm = TOKEN_TILE
    bm = lambda b, t: (b, t, 0)
    mix = pl.BlockSpec((None, tm, GROUP_W), bm)
    return pl.pallas_call(
        _wout_kernel, grid=(B, Lt // tm),
        in_specs=[pl.BlockSpec((None, tm, D), bm),
                  pl.BlockSpec((None, 6, D), lambda b, t: (jnp.where(t >= n_lat_tiles, B, b), 0, 0)),
                  mix, mix, mix, mix,
                  pl.BlockSpec((4 * GROUP_W, D), lambda b, t: (0, 0))],
        out_specs=pl.BlockSpec((None, tm, D), bm),
        out_shape=jax.ShapeDtypeStruct((B, Lt, D), F32),
        compiler_params=_cparams(("parallel", "parallel"), 48), name="out_proj",
    )(X, mod_l, a, b, m, r, w)


def _mlp_kernel(x_ref, xc_ref, mod_ref, modc_ref, nw_ref, w1_ref, w2_ref, *rest, nf, rows, saxis):
    o_ref, h_ref, a_ref = rest[-3:]
    s = pl.program_id(saxis)

    @pl.when(s == 0)
    def _():
        x = x_ref[...].reshape(rows, x_ref.shape[-1])
        h_ref[...] = _norm_mod(x, nw_ref[...], mod_ref[3:4, :], mod_ref[4:5, :])

    @pl.when(s < nf)
    def _():
        a = jnp.maximum(jnp.dot(h_ref[...], w1_ref[...], preferred_element_type=F32), 0.0)
        a_ref[s] = (a * a).astype(BF16)

    @pl.when(s >= nf)
    def _():
        y = jnp.dot(a_ref[0], w2_ref[0], preferred_element_type=F32)
        for f in range(1, nf):
            y += jnp.dot(a_ref[f], w2_ref[f], preferred_element_type=F32)
        o_ref[...] = xc_ref[...] + (modc_ref[5:6, :] * y).reshape(o_ref.shape)


def _mlp(X, mod_l, nw, w1, w2, n_lat, with_ctx):
    B, Lt, D = X.shape
    n_ctx = Lt - n_lat
    ff = w1.shape[1]
    tm, tf, tn = MLP_ROWS, MLP_FF_TILE, MLP_OUT_TILE
    nf, nn = ff // tf, D // tn
    ns = nf + nn
    w2r = jnp.transpose(w2.reshape(nf, tf, nn, tn), (2, 0, 1, 3))
    nwr = nw.reshape(1, D)
    col = lambda s: jnp.maximum(s - nf, 0)
    scratch = [pltpu.VMEM((tm, D), BF16), pltpu.VMEM((nf, tm, tf), BF16)]
    lat = pl.pallas_call(
        functools.partial(_mlp_kernel, nf=nf, rows=tm, saxis=2), grid=(B, n_lat // tm, ns),
        in_specs=[pl.BlockSpec((None, tm, D), lambda b, t, s: (b, t, 0)),
                  pl.BlockSpec((None, tm, tn), lambda b, t, s: (b, t, col(s))),
                  pl.BlockSpec((None, 6, D), lambda b, t, s: (b, 0, 0)),
                  pl.BlockSpec((None, 6, tn), lambda b, t, s: (b, 0, col(s))),
                  pl.BlockSpec((1, D), lambda b, t, s: (0, 0)),
                  pl.BlockSpec((D, tf), lambda b, t, s: (0, jnp.minimum(s, nf - 1))),
                  pl.BlockSpec((None, nf, tf, tn), lambda b, t, s: (col(s), 0, 0, 0))],
        out_specs=pl.BlockSpec((None, tm, tn), lambda b, t, s: (b, t, col(s))),
        out_shape=jax.ShapeDtypeStruct((B, Lt if with_ctx else n_lat, D), F32),
        scratch_shapes=scratch,
        compiler_params=_cparams(("parallel", "parallel", "arbitrary"), 52), name="mlp_lat",
    )(X, X, mod_l, mod_l, nwr, w1, w2r)
    if not with_ctx:
        return lat
    g = tm // n_ctx
    cblk = n_lat // n_ctx
    return pl.pallas_call(
        functools.partial(_mlp_kernel, nf=nf, rows=tm, saxis=1), grid=(B // g, ns),
        in_specs=[pl.BlockSpec((g, n_ctx, D), lambda i, s: (i, cblk, 0)),
                  pl.BlockSpec((g, n_ctx, tn), lambda i, s: (i, cblk, col(s))),
                  pl.BlockSpec((None, 6, D), lambda i, s: (B, 0, 0)),
                  pl.BlockSpec((None, 6, tn), lambda i, s: (B, 0, col(s))),
                  pl.BlockSpec((1, D), lambda i, s: (0, 0)),
                  pl.BlockSpec((D, tf), lambda i, s: (0, jnp.minimum(s, nf - 1))),
                  pl.BlockSpec((None, nf, tf, tn), lambda i, s: (col(s), 0, 0, 0)),
                  pl.BlockSpec(memory_space=pl.ANY)],
        out_specs=pl.BlockSpec((g, n_ctx, tn), lambda i, s: (i, cblk, col(s))),
        out_shape=jax.ShapeDtypeStruct((B, Lt, D), F32),
        input_output_aliases={7: 0},
        scratch_shapes=scratch,
        compiler_params=_cparams(("parallel", "arbitrary"), 52), name="mlp_ctx",
    )(X, X, mod_l, mod_l, nwr, w1, w2r, lat)


def _pack_w_in(w_in):
    offs = [0, 512, 1024, 1536, 2048, 2560, 2576, 2960, 3088, 3152, 3664, 4176]
    u, mq, mk, mv, mo, mg, cq, ckv, kr, lx, lg = [w_in[..., offs[i]:offs[i + 1]] for i in range(11)]
    depth, d = w_in.shape[:2]
    mg = jnp.transpose(mg.reshape(depth, d, 2, 2, ML_HEADS), (0, 1, 4, 3, 2))[:, :, :, ::-1, :].reshape(depth, d, 16)
    pad = jnp.zeros((depth, d, MISC_W - MLA_ROPE - 16), w_in.dtype)
    w_p = jnp.concatenate([u, mq, mk, mv, mo, cq, ckv, lx, lg, kr, mg, pad], axis=-1).astype(BF16)
    return w_p, jnp.transpose(mg, (0, 2, 1)).astype(BF16), jnp.transpose(mv, (0, 2, 1)).astype(BF16)


def _pack_mla(w_q_up, w_kv_up):
    depth = w_q_up.shape[0]
    wq = w_q_up.reshape(depth, MLA_Q_LORA, MLA_HEADS, MLA_QK)
    wq = jnp.pad(wq, ((0, 0), (0, 0), (0, 0), (0, MLA_PAD - MLA_QK))).reshape(depth, MLA_Q_LORA, -1).astype(BF16)
    wkv = w_kv_up.reshape(depth, MLA_KV_LORA, MLA_HEADS, MLA_NOPE + MLA_V)
    wkv = jnp.concatenate([wkv[..., :MLA_NOPE].reshape(depth, MLA_KV_LORA, -1),
                           wkv[..., MLA_NOPE:].reshape(depth, MLA_KV_LORA, -1)], axis=-1).astype(BF16)
    return wq, wkv


def _pack_lru(wa, ba, wx, bx):
    wab = jnp.concatenate([wa, wx], axis=-1).astype(BF16)
    bias = jnp.stack([ba, bx], axis=2).astype(F32)
    return wab, bias


def kernel(x, c, ctx, c_ctx, ada_w, ada_b, norm1_w, norm2_w, w_in, w_out, s5_lam_re, s5_lam_im, s5_log_dt, s5_b_re, s5_b_im, s5_c_re, s5_c_im, s5_d, s5_glu_w, s5_glu_b, ml_ig_bias, ml_fg_bias, ml_out_norm, mla_q_a_norm, mla_w_q_up, mla_kv_a_norm, mla_w_kv_up, mla_q_norm, mla_k_norm, lru_conv_w, lru_conv_b, lru_wa, lru_ba, lru_wx, lru_bx, lru_lam, mlp_w1, mlp_w2):
    B, L, D = x.shape
    Lc = ctx.shape[1]
    depth = ada_w.shape[0]
    n_lat_tiles = L // TOKEN_TILE
    X = jnp.concatenate([x, ctx], axis=1).astype(F32)
    mod = _ada_all(c, c_ctx, ada_w, ada_b)
    w_p, wgt, wvt = _pack_w_in(w_in)
    wq, wkv = _pack_mla(mla_w_q_up, mla_w_kv_up)
    wab, lru_bias = _pack_lru(lru_wa, lru_ba, lru_wx, lru_bx)
    tables = _rope_tables(L, Lc)
    w_out_b, glu_w_b = w_out.astype(BF16), s5_glu_w.astype(BF16)
    w1_b, w2_b = mlp_w1.astype(BF16), mlp_w2.astype(BF16)
    for l in range(depth):
        u, mq, mk, mv, mo, cq, ckv, lx, lg, misc, mgt = _win(X, mod[l], norm1_w[l], w_p[l], wgt[l], wvt[l],
                                                             n_lat_tiles)
        s5p = _s5_params(s5_lam_re[l], s5_lam_im[l], s5_log_dt[l], s5_b_re[l], s5_b_im[l], s5_c_re[l], s5_c_im[l])
        a_mix = _s5(u, s5p, s5_d[l], glu_w_b[l], s5_glu_b[l], L)
        r_mix = _lru(lx, lg, lru_conv_w[l], lru_conv_b[l], wab[l], lru_bias[l], lru_lam[l], L)
        b_mix = _mlstm(mq, mk, mv, mo, misc, mgt, ml_ig_bias[l], ml_fg_bias[l], ml_out_norm[l], L)
        m_mix = _mla(cq, ckv, misc, tables, mla_q_a_norm[l], wq[l], mla_kv_a_norm[l], wkv[l],
                     mla_q_norm[l], mla_k_norm[l], L)
        X = _wout(X, mod[l], a_mix, b_mix, m_mix, r_mix, w_out_b[l], n_lat_tiles)
        X = _mlp(X, mod[l], norm2_w[l], w1_b[l], w2_b[l], L, with_ctx=l < depth - 1)
    return X
```

```python
import functools

import jax
import jax.numpy as jnp
from jax import lax
from jax.experimental import pallas as pl
from jax.experimental.pallas import tpu as pltpu

F32 = jnp.float32
BF16 = jnp.bfloat16
EPS = 1e-6
HIGHEST = lax.Precision.HIGHEST

GROUP_W = 512
TOKEN_TILE = 256
S5_STATE = 64
S5_CH = 16
S5_STRIPS = 4
S5_STRIP_STATES = 512
S5_MIN_NEG = -1e-4
S5_CHUNK = 64
ML_HEADS = 4
ML_DH = 128
ML_CHUNK = 128
ML_HEADS_PER_STEP = 2
MLA_HEADS = 4
MLA_NOPE = 128
MLA_ROPE = 64
MLA_QK = MLA_NOPE + MLA_ROPE
MLA_V = 128
MLA_Q_LORA = 384
MLA_KV_LORA = 128
MLA_PAD = 256
ATTN_SCALE = MLA_QK ** -0.5
ROPE_BASE = 10000.0
GRID_W = 64
Q_TILE = 1024
Q_SUB = 256
LRU_BLOCKS = 4
LRU_BLOCK_W = 128
LRU_CONV = 4
LRU_C = 8.0
MLP_FF_TILE = 2048
MLP_OUT_TILE = 512
MLP_ROWS = 512

COL_U, COL_MQ, COL_MK, COL_MV, COL_MO = 0, 512, 1024, 1536, 2048
COL_CQ, COL_CKV, COL_LX, COL_LG, COL_MISC = 2560, 2944, 3072, 3584, 4096
IN_COLS_PACKED = 4224
MISC_W = 128


def _cparams(sem, vmem_mb):
    return pltpu.CompilerParams(dimension_semantics=sem, vmem_limit_bytes=vmem_mb << 20)


def _norm_mod(x, nw, shift, scale):
    ms = jnp.mean(x * x, axis=-1, keepdims=True)
    y = (x * lax.rsqrt(ms + EPS)) * nw
    return (y * (1.0 + scale) + shift).astype(BF16)


def _log_sigmoid(x):
    return jnp.minimum(x, 0.0) - jnp.log1p(jnp.exp(-jnp.abs(x)))


def _softplus(x):
    return jnp.maximum(x, 0.0) + jnp.log1p(jnp.exp(-jnp.abs(x)))


def _ada_kernel(c_ref, w_ref, b_ref, o_ref):
    c = c_ref[...]
    act = (c * jax.nn.sigmoid(c)).astype(BF16)
    o_ref[...] = jnp.dot(act, w_ref[...].astype(BF16), preferred_element_type=F32) + b_ref[...]


def _ada_all(c, c_ctx, ada_w, ada_b):
    depth, d, n6 = ada_w.shape
    nb = c.shape[0]
    rows = nb + 8
    cc = jnp.concatenate([c.astype(F32), jnp.broadcast_to(c_ctx.astype(F32)[None], (8, d))], axis=0)
    tn = 1024
    out = pl.pallas_call(
        _ada_kernel,
        grid=(depth, n6 // tn),
        in_specs=[pl.BlockSpec((rows, d), lambda l, n: (0, 0)),
                  pl.BlockSpec((None, d, tn), lambda l, n: (l, 0, n)),
                  pl.BlockSpec((None, 1, tn), lambda l, n: (l, 0, n))],
        out_specs=pl.BlockSpec((None, rows, tn), lambda l, n: (l, 0, n)),
        out_shape=jax.ShapeDtypeStruct((depth, rows, n6), F32),
        compiler_params=_cparams(("parallel", "parallel"), 40),
        name="ada_mod",
    )(cc, ada_w, ada_b.reshape(depth, 1, n6))
    return out.reshape(depth, rows, 6, d)


def _win_kernel(x_ref, mod_ref, nw_ref, w_ref, wgt_ref, wvt_ref,
                u_ref, mq_ref, mk_ref, mvt_ref, mo_ref, cq_ref, ckv_ref, lx_ref, lg_ref, misc_ref, mgt_ref):
    h = _norm_mod(x_ref[...], nw_ref[...], mod_ref[0:1, :], mod_ref[1:2, :])
    nt_dims = (((1,), (1,)), ((), ()))

    def mm(lo, width):
        return jnp.dot(h, w_ref[:, lo:lo + width], preferred_element_type=F32)

    u_ref[...] = mm(COL_U, GROUP_W)
    mq_ref[...] = mm(COL_MQ, GROUP_W).astype(BF16)
    mk_ref[...] = (mm(COL_MK, GROUP_W) * (ML_DH ** -0.5)).astype(BF16)
    mvt = lax.dot_general(wvt_ref[...], h, nt_dims, preferred_element_type=F32).astype(BF16)
    for cc in range(mvt_ref.shape[0]):
        mvt_ref[cc] = mvt[:, cc * ML_CHUNK:(cc + 1) * ML_CHUNK]
    mo_ref[...] = mm(COL_MO, GROUP_W)
    cq_ref[...] = mm(COL_CQ, MLA_Q_LORA)
    ckv_ref[...] = mm(COL_CKV, MLA_KV_LORA)
    lx_ref[...] = mm(COL_LX, GROUP_W)
    lg_ref[...] = mm(COL_LG, GROUP_W)
    misc_ref[...] = mm(COL_MISC, MISC_W)
    mgt_ref[...] = lax.dot_general(wgt_ref[...], h, nt_dims, preferred_element_type=F32)


def _win(X, mod_l, nw, w_p, wgt, wvt, n_lat_tiles):
    B, Lt, D = X.shape
    tm = TOKEN_TILE
    nt = Lt // tm
    bm = lambda b, t: (b, t, 0)
    in_specs = [
        pl.BlockSpec((None, tm, D), bm),
        pl.BlockSpec((None, 6, D), lambda b, t: (jnp.where(t >= n_lat_tiles, B, b), 0, 0)),
        pl.BlockSpec((1, D), lambda b, t: (0, 0)),
        pl.BlockSpec((D, IN_COLS_PACKED), lambda b, t: (0, 0), pipeline_mode=pl.Buffered(1)),
        pl.BlockSpec((16, D), lambda b, t: (0, 0)),
        pl.BlockSpec((GROUP_W, D), lambda b, t: (0, 0)),
    ]
    cpt = tm // ML_CHUNK
    out_specs = [
        pl.BlockSpec((None, tm, GROUP_W), bm),
        pl.BlockSpec((None, tm, GROUP_W), bm),
        pl.BlockSpec((None, tm, GROUP_W), bm),
        pl.BlockSpec((None, cpt, GROUP_W, ML_CHUNK), lambda b, t: (b, t, 0, 0)),
        pl.BlockSpec((None, tm, GROUP_W), bm),
        pl.BlockSpec((None, tm, MLA_Q_LORA), bm),
        pl.BlockSpec((None, tm, MLA_KV_LORA), bm),
        pl.BlockSpec((None, tm, GROUP_W), bm),
        pl.BlockSpec((None, tm, GROUP_W), bm),
        pl.BlockSpec((None, tm, MISC_W), bm),
        pl.BlockSpec((None, 16, tm), lambda b, t: (b, 0, t)),
    ]
    sds = jax.ShapeDtypeStruct
    out_shape = [
        sds((B, Lt, GROUP_W), F32),
        sds((B, Lt, GROUP_W), BF16), sds((B, Lt, GROUP_W), BF16),
        sds((B, Lt // ML_CHUNK, GROUP_W, ML_CHUNK), BF16),
        sds((B, Lt, GROUP_W), F32),
        sds((B, Lt, MLA_Q_LORA), F32), sds((B, Lt, MLA_KV_LORA), F32),
        sds((B, Lt, GROUP_W), F32), sds((B, Lt, GROUP_W), F32),
        sds((B, Lt, MISC_W), F32),
        sds((B, 16, Lt), F32),
    ]
    return pl.pallas_call(
        _win_kernel, grid=(B, nt), in_specs=in_specs, out_specs=out_specs, out_shape=out_shape,
        compiler_params=_cparams(("parallel", "parallel"), 52), name="in_proj",
    )(X, mod_l, nw.reshape(1, D), w_p, wgt, wvt)


def _rec_kernel(*refs, tc, nb, nlc, nc, reverse, final):
    if final:
        (ub_ref, yp_ref, wd_ref, a_ref, wc_ref, dsk_ref, gw_ref, gb_ref,
         x_ref, xp_ref, xn_ref, hp_ref, g_ref, cw_ref, cb_ref, w_ref, bias_ref, lam_ref,
         oa_ref, or_ref, u_ref, buf, st, xpad, abuf, bbuf, lst, ybuf, o3) = refs
    else:
        (ub_ref, wd_ref, a_ref, wc_ref,
         x_ref, xp_ref, xn_ref, cw_ref, cb_ref, w_ref, bias_ref, lam_ref,
         oa_ref, or_ref, u_ref, buf, st, xpad, abuf, bbuf, lst) = refs
    i = pl.program_id(0)
    c = (nc - 1 - i) if reverse else lax.rem(i + nlc, nc)
    R = tc * nb
    RB = min(R, 512)
    RL = min(R, 256)
    NS = S5_STRIP_STATES
    nh = nb // 8
    left = LRU_CONV // 2
    halo = xp_ref.shape[1]

    @pl.when(i == 0)
    def _():
        st[...] = jnp.zeros_like(st)
        lst[...] = jnp.zeros_like(lst)

    starts = jnp.logical_or(c == 0, c == nlc)
    ends = jnp.logical_or(c == nlc - 1, c == nc - 1)
    for j in range(left):
        xpad[j * nb:(j + 1) * nb, :] = jnp.where(starts, 0.0, xp_ref[:, halo - left + j, :])
    for t in range(tc):
        u_ref[t * nb:(t + 1) * nb, :] = ub_ref[:, t, :]
        xpad[(left + t) * nb:(left + t + 1) * nb, :] = x_ref[:, t, :]
    for j in range(LRU_CONV - 1 - left):
        xpad[(left + tc + j) * nb:(left + tc + j + 1) * nb, :] = jnp.where(ends, 0.0, xn_ref[:, j, :])
    sp = _softplus(-lam_ref[...])

    def lru_gate_parts(rb):
        r0 = rb * RL
        cache = {}

        def conv():
            if "xs" not in cache:
                xs = cb_ref[...]
                for j in range(LRU_CONV):
                    xs = xs + xpad[r0 + j * nb:r0 + j * nb + RL, :] * cw_ref[j:j + 1, :]
                cache["xs"] = xs
            return cache["xs"]

        def block(n):
            cs = slice(n * LRU_BLOCK_W, (n + 1) * LRU_BLOCK_W)
            xs = conv()[:, cs]
            z = jnp.dot(xs.astype(BF16), w_ref[n], preferred_element_type=F32)
            r = jax.nn.sigmoid(z[:, :LRU_BLOCK_W] + bias_ref[0:1, cs])
            ig = jax.nn.sigmoid(z[:, LRU_BLOCK_W:] + bias_ref[1:2, cs])
            log_a = (-LRU_C * r) * sp[:, cs]
            abuf[r0:r0 + RL, cs] = jnp.exp(log_a)
            th = jnp.tanh(log_a)
            bbuf[r0:r0 + RL, cs] = jnp.sqrt(-2.0 * th / (1.0 - th)) * (ig * xs)

        return [functools.partial(block, n) for n in range(LRU_BLOCKS)]

    def s5_drive_parts(s):
        cs = slice(s * 128, (s + 1) * 128)

        def part(rb):
            rs = slice(rb * RB, (rb + 1) * RB)
            buf[s % 2, rs, :] = jnp.dot(u_ref[rs, cs].astype(BF16), wd_ref[s], preferred_element_type=F32)

        return [functools.partial(part, rb) for rb in range(R // RB)]

    def s5_scan(s):
        sb = buf.at[s % 2]
        ar = jnp.broadcast_to(a_ref[s, 0:1, :], (8, NS))
        ai = jnp.broadcast_to(a_ref[s, 1:2, :], (8, NS))
        init = []
        for hh in range(nh):
            init += [st[s, hh * 8:(hh + 1) * 8, 0:NS], st[s, hh * 8:(hh + 1) * 8, NS:2 * NS]]

        def body(k, carry):
            t = (tc - 1 - k) if reverse else k
            out = []
            for hh in range(nh):
                sr, si = carry[2 * hh], carry[2 * hh + 1]
                row = pl.multiple_of(t * nb + hh * 8, 8)
                br = sb[pl.ds(row, 8), 0:NS]
                bi = sb[pl.ds(row, 8), NS:2 * NS]
                nsr = ar * sr - ai * si + br
                nsi = ar * si + ai * sr + bi
                sb[pl.ds(row, 8), 0:NS] = nsr
                sb[pl.ds(row, 8), NS:2 * NS] = nsi
                out += [nsr, nsi]
            return tuple(out)

        fin = lax.fori_loop(0, tc, body, tuple(init), unroll=2)
        for hh in range(nh):
            st[s, hh * 8:(hh + 1) * 8, 0:NS] = fin[2 * hh]
            st[s, hh * 8:(hh + 1) * 8, NS:2 * NS] = fin[2 * hh + 1]

    def s5_readout_parts(s):
        cs = slice(s * 128, (s + 1) * 128)

        def part(rb):
            rs = slice(rb * RB, (rb + 1) * RB)
            y = jnp.dot(buf[s % 2, rs, :].astype(BF16), wc_ref[s], preferred_element_type=F32)
            if final:
                ybuf[rs, cs] = y
            else:
                oa_ref[rs, cs] = y

        return [functools.partial(part, rb) for rb in range(R // RB)]

    def emit_interleaved(mxu_parts, vpu_parts):
        n = max(len(mxu_parts), len(vpu_parts))
        for k in range(n):
            if k < len(mxu_parts):
                mxu_parts[k]()
            if k < len(vpu_parts):
                vpu_parts[k]()

    n_gate_blocks = R // RL
    for s in range(S5_STRIPS + 1):
        mxu = (s5_readout_parts(s - 1) if s > 0 else []) + (s5_drive_parts(s) if s < S5_STRIPS else [])
        vpu = []
        for rb in range(n_gate_blocks):
            if rb % S5_STRIPS == s:
                vpu += lru_gate_parts(rb)
        emit_interleaved(mxu, vpu)
        if s < S5_STRIPS:
            s5_scan(s)

    init = tuple(lst[hh * 8:(hh + 1) * 8, :] for hh in range(nh))

    def lru_body(k, carry):
        t = (tc - 1 - k) if reverse else k
        out = []
        for hh in range(nh):
            row = pl.multiple_of(t * nb + hh * 8, 8)
            hnew = abuf[pl.ds(row, 8), :] * carry[hh] + bbuf[pl.ds(row, 8), :]
            bbuf[pl.ds(row, 8), :] = hnew
            out.append(hnew)
        return tuple(out)

    fin = lax.fori_loop(0, tc, lru_body, init, unroll=4)
    for hh in range(nh):
        lst[hh * 8:(hh + 1) * 8, :] = fin[hh]

    if not final:
        or_ref[...] = bbuf[...]
        return
    for rb in range(R // RB):
        rs = slice(rb * RB, (rb + 1) * RB)
        y = ybuf[rs, :] + yp_ref[rs, :] + dsk_ref[...] * u_ref[rs, :]
        g = jax.nn.gelu(y)
        z = jnp.dot(g.astype(BF16), gw_ref[...], preferred_element_type=F32) + gb_ref[...]
        res = g * jax.nn.sigmoid(z)
        for tt in range(RB // nb):
            o3[:, rb * (RB // nb) + tt, :] = res[tt * nb:(tt + 1) * nb, :]
    oa_ref[...] = o3[...].astype(BF16)
    for t in range(tc):
        rows = slice(t * nb, (t + 1) * nb)
        o3[:, t, :] = hp_ref[rows, :] + bbuf[rows, :]
    or_ref[...] = (o3[...] * jax.nn.gelu(g_ref[...])).astype(BF16)


def _chunk_order(nc, nlc, reverse):
    if reverse:
        return lambda i: nc - 1 - i
    return lambda i: lax.rem(i + nlc, nc)


def _recurrent(u, lx, lg, s5p, d_skip, glu_w, glu_b, conv_w, conv_b, wab, bias, lam, n_lat):
    nb, lt, _ = u.shape
    rows = nb * lt
    tc = S5_CHUNK
    R = tc * nb
    nc, nlc = lt // tc, n_lat // tc
    halo = 8
    wd, a, wc = s5p
    fwd = None
    for d in range(2):
        final = d == 1
        cidx = _chunk_order(nc, nlc, reverse=final)
        row_spec = pl.BlockSpec((R, GROUP_W), lambda i: (cidx(i), 0))
        bt_spec = pl.BlockSpec((nb, tc, GROUP_W), lambda i: (0, cidx(i), 0))
        prev_spec = pl.BlockSpec((nb, halo, GROUP_W),
                                 lambda i: (0, jnp.maximum(cidx(i) * (tc // halo) - 1, 0), 0))
        next_spec = pl.BlockSpec((nb, halo, GROUP_W),
                                 lambda i: (0, jnp.minimum((cidx(i) + 1) * (tc // halo), lt // halo - 1), 0))
        full = lambda shape: pl.BlockSpec(shape, lambda i: (0,) * len(shape))
        s5_w = [full((S5_STRIPS, 128, 2 * S5_STRIP_STATES)), full((S5_STRIPS, 2, S5_STRIP_STATES)),
                full((S5_STRIPS, 2 * S5_STRIP_STATES, 128))]
        lru_w = [full((LRU_CONV, GROUP_W)), full((1, GROUP_W)), full((LRU_BLOCKS, LRU_BLOCK_W, 2 * LRU_BLOCK_W)),
                 full((2, GROUP_W)), full((1, GROUP_W))]
        lru_args = [conv_w, conv_b.reshape(1, GROUP_W), wab[d], bias[d], lam[d].reshape(1, GROUP_W)]
        if final:
            in_specs = ([bt_spec, row_spec] + s5_w + [full((1, GROUP_W)), full((GROUP_W, GROUP_W)), full((1, GROUP_W))]
                        + [bt_spec, prev_spec, next_spec, row_spec, bt_spec] + lru_w)
            args = ([u, fwd[0], wd[d], a[d], wc[d], d_skip.reshape(1, GROUP_W), glu_w, glu_b.reshape(1, GROUP_W)]
                    + [lx, lx, lx, fwd[1], lg] + lru_args)
            out_specs = [bt_spec, bt_spec]
            out_shape = [jax.ShapeDtypeStruct((nb, lt, GROUP_W), BF16)] * 2
        else:
            in_specs = [bt_spec] + s5_w + [bt_spec, prev_spec, next_spec] + lru_w
            args = [u, wd[d], a[d], wc[d], lx, lx, lx] + lru_args
            out_specs = [row_spec, row_spec]
            out_shape = [jax.ShapeDtypeStruct((rows, GROUP_W), F32)] * 2
        scratch = [pltpu.VMEM((R, GROUP_W), F32), pltpu.VMEM((2, R, 2 * S5_STRIP_STATES), F32),
                   pltpu.VMEM((S5_STRIPS, nb, 2 * S5_STRIP_STATES), F32),
                   pltpu.VMEM((R + 3 * nb, GROUP_W), F32), pltpu.VMEM((R, GROUP_W), F32),
                   pltpu.VMEM((R, GROUP_W), F32), pltpu.VMEM((nb, GROUP_W), F32)]
        if final:
            scratch += [pltpu.VMEM((R, GROUP_W), F32), pltpu.VMEM((nb, tc, GROUP_W), F32)]
        fwd = pl.pallas_call(
            functools.partial(_rec_kernel, tc=tc, nb=nb, nlc=nlc, nc=nc, reverse=final, final=final),
            grid=(nc,), in_specs=in_specs, out_specs=out_specs, out_shape=out_shape,
            scratch_shapes=scratch,
            compiler_params=_cparams(("arbitrary",), 56), name="rec_rev" if final else "rec_fwd",
        )(*args)
    return fwd


def _s5_params(lam_re, lam_im, log_dt, b_re, b_im, c_re, c_im):
    lr = jnp.minimum(lam_re.astype(F32), S5_MIN_NEG)
    li = lam_im.astype(F32)
    dt = jnp.exp(log_dt.astype(F32))[..., None]
    mag = jnp.exp(lr * dt)
    ar, ai = mag * jnp.cos(li * dt), mag * jnp.sin(li * dt)
    den = lr * lr + li * li
    fr = ((ar - 1.0) * lr + ai * li) / den
    fi = (ai * lr - (ar - 1.0) * li) / den
    bbr = fr[..., None] * b_re - fi[..., None] * b_im
    bbi = fr[..., None] * b_im + fi[..., None] * b_re
    gs = 128 // S5_CH
    eye = jnp.eye(gs, dtype=F32)

    def drive(bb):
        t = bb.reshape(2, S5_STRIPS, gs, S5_STATE, S5_CH)
        return jnp.einsum('dsgpc,gh->dsgchp', t, eye).reshape(2, S5_STRIPS, 128, S5_STRIP_STATES)

    def read(cc):
        t = cc.reshape(2, S5_STRIPS, gs, S5_CH, S5_STATE)
        return jnp.einsum('dsgcp,gh->dsgphc', t, eye).reshape(2, S5_STRIPS, S5_STRIP_STATES, 128)

    wd = jnp.concatenate([drive(bbr), drive(bbi)], axis=-1).astype(BF16)
    wc = jnp.concatenate([read(c_re.astype(F32)), read(-c_im.astype(F32))], axis=-2).astype(BF16)
    a = jnp.stack([ar.reshape(2, S5_STRIPS, S5_STRIP_STATES), ai.reshape(2, S5_STRIPS, S5_STRIP_STATES)], axis=2)
    return wd, a, wc


def _mlstm_kernel(q_ref, k_ref, vt_ref, o_ref, gr_ref, gc_ref, br_ref, bc_ref, nw_ref, out_ref,
                  xb_scr, row_scr, nt_scr, u_scr, vec_scr, *, nch, nlat, hp):
    T = ML_CHUNK
    nt_dims = (((1,), (1,)), ((), ()))
    rid = lax.broadcasted_iota(jnp.int32, (T, T), 0)
    cid = lax.broadcasted_iota(jnp.int32, (T, T), 1)
    incl_rows = (rid <= cid).astype(F32)
    incl_cols = (rid >= cid).astype(F32)

    for j in range(hp):
        g = (gr_ref[j] + br_ref[j]).reshape(nch * 8, T)
        kind = lax.broadcasted_iota(jnp.int32, (nch * 8, T), 0) % 8
        lf = _log_sigmoid(g)
        pre = jnp.dot(lf, incl_rows, precision=HIGHEST, preferred_element_type=F32)
        b = jnp.where(kind == 1, pre[:, T - 1:T] - pre + lf, pre)
        x = g - pltpu.roll(b, 2, axis=0)
        row_scr[j] = jnp.where(kind < 2, b, x).reshape(nch, 8, T)
        gl = gc_ref[j] + bc_ref[j]
        kind = lax.broadcasted_iota(jnp.int32, (T, 128), 1) % 4
        lf = _log_sigmoid(gl)
        pre = jnp.dot(incl_cols, lf, precision=HIGHEST, preferred_element_type=F32)
        b = jnp.where(kind == 1, pre[T - 1:T, :] - pre + lf, pre)
        x = gl - pltpu.roll(b, 2, axis=1)
        for c in range(nch):
            for d in range(2):
                lane = c * 4 + 2 + d
                xb_scr[j, d, c] = jnp.broadcast_to(x[:, lane:lane + 1], (T, T))

    def independent(c, carry):
        rows = pl.ds(pl.multiple_of(c * T, T), T)
        pairs = [(j, d) for j in range(hp) for d in range(2)]
        hs = lambda j: slice(j * ML_DH, (j + 1) * ML_DH)
        swept = lambda d: (rid <= cid) if d == 0 else (rid >= cid)
        kq = {j: lax.dot_general(k_ref[rows, hs(j)], q_ref[rows, hs(j)], nt_dims, preferred_element_type=F32)
              for j in range(hp)}
        xm = {p: jnp.where(swept(p[1]), xb_scr[p[0], p[1], c], -jnp.inf) for p in pairs}
        a_row = {p: jnp.max(xm[p], axis=0, keepdims=True) for p in pairs}
        w0 = {}
        for j, d in pairs:
            a_last = a_row[j, d][:, T - 1:T] if d == 0 else a_row[j, d][:, 0:1]
            w0[j, d] = jnp.exp(row_scr[j, c, 2 + d:3 + d, :] - a_last)
        s0 = {p: kq[p[0]] * jnp.exp(xm[p] - a_row[p]) for p in pairs}
        for j, d in pairs:
            vt, kc = vt_ref[c, hs(j), :], k_ref[rows, hs(j)]
            nt_scr[j, d, c] = jnp.dot(vt, s0[j, d].astype(BF16), preferred_element_type=F32)
            wv = (vt.astype(F32) * w0[j, d]).astype(BF16)
            u_scr[j, d, c] = jnp.dot(wv, kc, preferred_element_type=F32)
            vec_scr[j, d, c, 0:1, :] = a_row[j, d]
            vec_scr[j, d, c, 1:2, :] = jnp.sum(s0[j, d], axis=0, keepdims=True)
            vec_scr[j, d, c, 2:3, :] = jnp.dot(jnp.broadcast_to(w0[j, d], (8, T)).astype(BF16), kc,
                                               preferred_element_type=F32)[0:1, :]
        return carry

    lax.fori_loop(0, nch, independent, 0)

    def pass2(c, d, j, state):
        cm, n, m = state
        hs = slice(j * ML_DH, (j + 1) * ML_DH)
        qc = q_ref[pl.ds(pl.multiple_of(c * T, T), T), hs]
        a_row, d0, n0 = vec_scr[j, d, c, 0:1, :], vec_scr[j, d, c, 1:2, :], vec_scr[j, d, c, 2:3, :]
        b_row = row_scr[j, c, d:d + 1, :]
        mu = jnp.maximum(m, a_row)
        inter, r = jnp.exp(m - mu), jnp.exp(a_row - mu)
        qct = lax.dot_general(cm.astype(BF16), qc, nt_dims, preferred_element_type=F32)
        qn = lax.dot_general(jnp.broadcast_to(n, (8, ML_DH)).astype(BF16), qc, nt_dims,
                             preferred_element_type=F32)[0:1, :]
        den = inter * qn + r * d0
        inv = 1.0 / jnp.maximum(jnp.abs(den), jnp.exp(-b_row - mu))
        nt_scr[j, d, c] = (inter * qct + r * nt_scr[j, d, c]) * inv
        a_last = a_row[:, T - 1:T] if d == 0 else a_row[:, 0:1]
        b_last = b_row[:, T - 1:T] if d == 0 else b_row[:, 0:1]
        mul = jnp.maximum(m, a_last)
        decay, rl = jnp.exp(m - mul), jnp.exp(a_last - mul)
        return decay * cm + rl * u_scr[j, d, c], decay * n + rl * n0, b_last + mul

    def states(i, carry):
        out = []
        for j in range(hp):
            out.append(pass2(lax.rem(i + nlat, nch), 0, j, carry[2 * j]))
            out.append(pass2(nch - 1 - i, 1, j, carry[2 * j + 1]))
        return tuple(out)

    zero = (jnp.zeros((ML_DH, ML_DH), F32), jnp.zeros((1, ML_DH), F32), jnp.zeros((1, 1), F32))
    lax.fori_loop(0, nch, states, (zero,) * (2 * hp))

    for c in range(nch):
        rows = slice(c * T, (c + 1) * T)
        for j in range(hp):
            hs = slice(j * ML_DH, (j + 1) * ML_DH)
            ht = nt_scr[j, 0, c] + nt_scr[j, 1, c]
            hn = ht * lax.rsqrt(jnp.mean(ht * ht, axis=0, keepdims=True) + EPS) * nw_ref[j]
            out_ref[rows, hs] = (hn.T * jax.nn.sigmoid(o_ref[rows, hs])).astype(BF16)


def _mlstm(mq, mk, mvt, mo, misc, mgt, ig_bias, fg_bias, out_norm, n_lat):
    B, Lt, _ = mq.shape
    T = ML_CHUNK
    nch, nlat = Lt // T, n_lat // T
    hp = ML_HEADS_PER_STEP
    assert 4 * nch <= 128
    gr = jnp.transpose(mgt.reshape(B, ML_HEADS, 4, nch, T), (0, 1, 3, 2, 4))
    gr = jnp.pad(gr, ((0, 0), (0, 0), (0, 0), (0, 4), (0, 0)))
    gc = misc[:, :, MLA_ROPE:MLA_ROPE + 16].reshape(B, nch, T, ML_HEADS, 4)
    gc = jnp.transpose(gc, (0, 3, 2, 1, 4)).reshape(B, ML_HEADS, T, nch * 4)
    gc = jnp.pad(gc, ((0, 0), (0, 0), (0, 0), (0, 128 - nch * 4)))
    kinds = jnp.concatenate([fg_bias.astype(F32), ig_bias.astype(F32)], axis=0).T
    br = jnp.broadcast_to(jnp.pad(kinds, ((0, 0), (0, 4)))[:, :, None], (ML_HEADS, 8, T))
    bc = jnp.pad(jnp.tile(kinds, (1, nch)), ((0, 0), (0, 128 - nch * 4)))[:, None, :]
    nwb = jnp.broadcast_to(out_norm.astype(F32)[:, :, None], (ML_HEADS, ML_DH, T))
    head_spec = pl.BlockSpec((None, Lt, hp * ML_DH), lambda b, h: (b, 0, h))
    slab = lambda: pltpu.VMEM((hp, 2, nch, T, T), F32)
    return pl.pallas_call(
        functools.partial(_mlstm_kernel, nch=nch, nlat=nlat, hp=hp),
        grid=(B, ML_HEADS // hp),
        in_specs=[head_spec, head_spec,
                  pl.BlockSpec((None, nch, hp * ML_DH, T), lambda b, h: (b, 0, h, 0)),
                  head_spec,
                  pl.BlockSpec((None, hp, nch, 8, T), lambda b, h: (b, h, 0, 0, 0)),
                  pl.BlockSpec((None, hp, T, 128), lambda b, h: (b, h, 0, 0)),
                  pl.BlockSpec((hp, 8, T), lambda b, h: (h, 0, 0)),
                  pl.BlockSpec((hp, 1, 128), lambda b, h: (h, 0, 0)),
                  pl.BlockSpec((hp, ML_DH, T), lambda b, h: (h, 0, 0))],
        out_specs=head_spec,
        out_shape=jax.ShapeDtypeStruct((B, Lt, GROUP_W), BF16),
        scratch_shapes=[slab(), pltpu.VMEM((hp, nch, 8, T), F32), slab(), slab(),
                        pltpu.VMEM((hp, 2, nch, 8, T), F32)],
        compiler_params=_cparams(("parallel", "parallel"), 52), name="mlstm",
    )(mq, mk, mvt, mo, gr, gc, br, bc, nwb)


def _mla_prep_kernel(cq_ref, ckv_ref, misc_ref, cos_ref, sin_ref, qan_ref, wq_ref, kvan_ref, wkv_ref,
                     qn_ref, kn_ref, qp_ref, kp_ref, v_ref):
    def rms(x, w):
        return (x * lax.rsqrt(jnp.mean(x * x, axis=-1, keepdims=True) + EPS)) * w

    tm = cq_ref.shape[0]
    cos, sins = cos_ref[...], sin_ref[...]
    lane = lax.broadcasted_iota(jnp.int32, (tm, 128), 1)
    first = (lane % (MLA_ROPE // 2)) < (MLA_ROPE // 4)

    def rope(x):
        sw = jnp.where(first, pltpu.roll(x, 128 - MLA_ROPE // 4, axis=1), pltpu.roll(x, MLA_ROPE // 4, axis=1))
        return x * cos + sw * sins

    hw = MLA_HEADS * MLA_NOPE
    q_all = jnp.dot(rms(cq_ref[...], qan_ref[...]).astype(BF16), wq_ref[...], preferred_element_type=F32)
    kv = jnp.dot(rms(ckv_ref[...], kvan_ref[...]).astype(BF16), wkv_ref[...], preferred_element_type=F32)
    kr = jnp.where(lane < MLA_ROPE, misc_ref[...], 0.0)
    kr_sq = jnp.sum(kr * kr, axis=-1, keepdims=True)
    v_ref[...] = kv[:, hw:].astype(BF16)
    qw_a, qw_b = qn_ref[:, 0:128] * ATTN_SCALE, qn_ref[:, 128:256] * ATTN_SCALE
    kw_a, kw_b = kn_ref[:, 0:128], kn_ref[:, 128:256]
    for h in range(MLA_HEADS):
        base = h * MLA_PAD
        qa, qb = q_all[:, base:base + 128], q_all[:, base + 128:base + 256]
        inv = lax.rsqrt(jnp.sum(qa * qa + qb * qb, axis=-1, keepdims=True) / MLA_QK + EPS)
        qp_ref[:, base:base + 128] = ((qa * inv) * qw_a).astype(BF16)
        qp_ref[:, base + 128:base + 256] = rope((qb * inv) * qw_b).astype(BF16)
        ka = kv[:, h * MLA_NOPE:(h + 1) * MLA_NOPE]
        inv = lax.rsqrt((jnp.sum(ka * ka, axis=-1, keepdims=True) + kr_sq) / MLA_QK + EPS)
        kp_ref[:, base:base + 128] = ((ka * inv) * kw_a).astype(BF16)
        kp_ref[:, base + 128:base + 256] = rope((kr * inv) * kw_b).astype(BF16)


def _attn_kernel(q_ref, k_ref, v_ref, o_ref, *, n_lat, n_lat_tiles, sub):
    def attend(n_sub, k, v):
        for first in range(0, n_sub, 2):
            blocks = [slice(i * sub, (i + 1) * sub) for i in range(first, min(first + 2, n_sub))]
            s = [lax.dot_general(q_ref[r, :], k, (((1,), (1,)), ((), ())), preferred_element_type=F32)
                 for r in blocks]
            p = [jnp.exp(si - jnp.max(si, axis=-1, keepdims=True)) for si in s]
            for r, pi in zip(blocks, p):
                den = jnp.sum(pi, axis=-1, keepdims=True)
                o_ref[r, :] = (jnp.dot(pi.astype(BF16), v, preferred_element_type=F32) / den).astype(BF16)

    qi = pl.program_id(2)

    @pl.when(qi < n_lat_tiles)
    def _():
        attend(q_ref.shape[0] // sub, k_ref[...], v_ref[...])

    @pl.when(qi >= n_lat_tiles)
    def _():
        attend((k_ref.shape[0] - n_lat) // sub, k_ref[n_lat:, :], v_ref[n_lat:, :])


def _rope_tables(n_lat, n_ctx):
    n_rows = n_lat // GRID_W
    rows = jnp.repeat(jnp.arange(n_rows, dtype=F32), GRID_W)
    cols = jnp.tile(jnp.arange(GRID_W, dtype=F32), n_rows)
    n_freq = MLA_ROPE // 4
    inv_freq = ROPE_BASE ** (-jnp.arange(n_freq, dtype=F32) / n_freq)
    ar, ac = rows[:, None] * inv_freq, cols[:, None] * inv_freq
    cos = jnp.concatenate([jnp.cos(ar), jnp.cos(ar), jnp.cos(ac), jnp.cos(ac)], axis=1)
    sins = jnp.concatenate([-jnp.sin(ar), jnp.sin(ar), -jnp.sin(ac), jnp.sin(ac)], axis=1)
    cos = jnp.pad(cos, ((0, n_ctx), (0, 128 - MLA_ROPE)), constant_values=1.0)
    sins = jnp.pad(sins, ((0, n_ctx), (0, 128 - MLA_ROPE)))
    return cos, sins


def _mla(cq, ckv, misc, tables, qan, wq, kvan, wkv, qn, kn, n_lat):
    B, Lt, _ = cq.shape
    tm = TOKEN_TILE
    bm = lambda b, t: (b, t, 0)
    full = lambda shape: pl.BlockSpec(shape, lambda b, t: (0,) * len(shape))
    hp = MLA_HEADS * MLA_PAD
    qp, kp, v = pl.pallas_call(
        _mla_prep_kernel, grid=(B, Lt // tm),
        in_specs=[pl.BlockSpec((None, tm, MLA_Q_LORA), bm), pl.BlockSpec((None, tm, MLA_KV_LORA), bm),
                  pl.BlockSpec((None, tm, MISC_W), bm),
                  pl.BlockSpec((tm, 128), lambda b, t: (t, 0)), pl.BlockSpec((tm, 128), lambda b, t: (t, 0)),
                  full((1, MLA_Q_LORA)), full((MLA_Q_LORA, hp)),
                  full((1, MLA_KV_LORA)), full((MLA_KV_LORA, MLA_HEADS * (MLA_NOPE + MLA_V))),
                  full((1, MLA_PAD)), full((1, MLA_PAD))],
        out_specs=[pl.BlockSpec((None, tm, hp), bm), pl.BlockSpec((None, tm, hp), bm),
                   pl.BlockSpec((None, tm, GROUP_W), bm)],
        out_shape=[jax.ShapeDtypeStruct((B, Lt, hp), BF16), jax.ShapeDtypeStruct((B, Lt, hp), BF16),
                   jax.ShapeDtypeStruct((B, Lt, GROUP_W), BF16)],
        compiler_params=_cparams(("parallel", "parallel"), 40), name="mla_prep",
    )(cq, ckv, misc, tables[0], tables[1], qan.reshape(1, -1), wq, kvan.reshape(1, -1), wkv,
      jnp.pad(qn.astype(F32), (0, MLA_PAD - MLA_QK)).reshape(1, -1),
      jnp.pad(kn.astype(F32), (0, MLA_PAD - MLA_QK)).reshape(1, -1))
    tq = min(Q_TILE, n_lat)
    return pl.pallas_call(
        functools.partial(_attn_kernel, n_lat=n_lat, n_lat_tiles=n_lat // tq, sub=Q_SUB),
        grid=(B, MLA_HEADS, pl.cdiv(Lt, tq)),
        in_specs=[pl.BlockSpec((None, tq, MLA_PAD), lambda b, h, t: (b, t, h)),
                  pl.BlockSpec((None, Lt, MLA_PAD), lambda b, h, t: (b, 0, h)),
                  pl.BlockSpec((None, Lt, MLA_V), lambda b, h, t: (b, 0, h))],
        out_specs=pl.BlockSpec((None, tq, MLA_V), lambda b, h, t: (b, t, h)),
        out_shape=jax.ShapeDtypeStruct((B, Lt, GROUP_W), BF16),
        compiler_params=_cparams(("parallel", "parallel", "arbitrary"), 40), name="mla_attn",
    )(qp, kp, v)


def _wout_kernel(x_ref, mod_ref, a_ref, b_ref, m_ref, r_ref, w_ref, o_ref):
    acc = jnp.dot(a_ref[...], w_ref[0:GROUP_W, :], preferred_element_type=F32)
    acc += jnp.dot(b_ref[...], w_ref[GROUP_W:2 * GROUP_W, :], preferred_element_type=F32)
    acc += jnp.dot(m_ref[...], w_ref[2 * GROUP_W:3 * GROUP_W, :], preferred_element_type=F32)
    acc += jnp.dot(r_ref[...], w_ref[3 * GROUP_W:4 * GROUP_W, :], preferred_element_type=F32)
    o_ref[...] = x_ref[...] + mod_ref[2:3, :] * acc


def _wout(X, mod_l, a, b, m, r, w, n_lat_tiles):
    B, Lt, D = X.shape
    tm = TOKEN_TILE
    bm = lambda b, t: (b, t, 0)
    mix = pl.BlockSpec((None, tm, GROUP_W), bm)
    return pl.pallas_call(
        _wout_kernel, grid=(B, Lt // tm),
        in_specs=[pl.BlockSpec((None, tm, D), bm),
                  pl.BlockSpec((None, 6, D), lambda b, t: (jnp.where(t >= n_lat_tiles, B, b), 0, 0)),
                  mix, mix, mix, mix,
                  pl.BlockSpec((4 * GROUP_W, D), lambda b, t: (0, 0))],
        out_specs=pl.BlockSpec((None, tm, D), bm),
        out_shape=jax.ShapeDtypeStruct((B, Lt, D), F32),
        compiler_params=_cparams(("parallel", "parallel"), 48), name="out_proj",
    )(X, mod_l, a, b, m, r, w)


def _mlp_kernel(x_ref, xc_ref, mod_ref, modc_ref, nw_ref, w1_ref, w2_ref, *rest, nf, rows, saxis):
    o_ref, h_ref, a_ref = rest[-3:]
    s = pl.program_id(saxis)

    @pl.when(s == 0)
    def _():
        x = x_ref[...].reshape(rows, x_ref.shape[-1])
        half = rows // 2
        for r in (slice(0, half), slice(half, rows)):
            h = _norm_mod(x[r], nw_ref[...], mod_ref[3:4, :], mod_ref[4:5, :])
            h_ref[r, :] = h
            a = jnp.maximum(jnp.dot(h, w1_ref[...], preferred_element_type=F32), 0.0)
            a_ref[0, r, :] = (a * a).astype(BF16)

    @pl.when(jnp.logical_and(s > 0, s < nf))
    def _():
        a = jnp.maximum(jnp.dot(h_ref[...], w1_ref[...], preferred_element_type=F32), 0.0)
        a_ref[s] = (a * a).astype(BF16)

    @pl.when(s >= nf)
    def _():
        y = jnp.dot(a_ref[0], w2_ref[0], preferred_element_type=F32)
        for f in range(1, nf):
            y += jnp.dot(a_ref[f], w2_ref[f], preferred_element_type=F32)
        o_ref[...] = xc_ref[...] + (modc_ref[5:6, :] * y).reshape(o_ref.shape)


def _mlp(X, mod_l, nw, w1, w2, n_lat, with_ctx):
    B, Lt, D = X.shape
    n_ctx = Lt - n_lat
    ff = w1.shape[1]
    tm, tf, tn = MLP_ROWS, MLP_FF_TILE, MLP_OUT_TILE
    nf, nn = ff // tf, D // tn
    ns = nf + nn
    w2r = w2
    nwr = nw.reshape(1, D)
    col = lambda s: jnp.maximum(s - nf, 0)
    scratch = [pltpu.VMEM((tm, D), BF16), pltpu.VMEM((nf, tm, tf), BF16)]
    lat = pl.pallas_call(
        functools.partial(_mlp_kernel, nf=nf, rows=tm, saxis=2), grid=(B, n_lat // tm, ns),
        in_specs=[pl.BlockSpec((None, tm, D), lambda b, t, s: (b, t, 0)),
                  pl.BlockSpec((None, tm, tn), lambda b, t, s: (b, t, col(s))),
                  pl.BlockSpec((None, 6, D), lambda b, t, s: (b, 0, 0)),
                  pl.BlockSpec((None, 6, tn), lambda b, t, s: (b, 0, col(s))),
                  pl.BlockSpec((1, D), lambda b, t, s: (0, 0)),
                  pl.BlockSpec((D, tf), lambda b, t, s: (0, jnp.minimum(s, nf - 1))),
                  pl.BlockSpec((None, nf, tf, tn), lambda b, t, s: (col(s), 0, 0, 0))],
        out_specs=pl.BlockSpec((None, tm, tn), lambda b, t, s: (b, t, col(s))),
        out_shape=jax.ShapeDtypeStruct((B, Lt if with_ctx else n_lat, D), F32),
        scratch_shapes=scratch,
        compiler_params=_cparams(("parallel", "parallel", "arbitrary"), 58), name="mlp_lat",
    )(X, X, mod_l, mod_l, nwr, w1, w2r)
    if not with_ctx:
        return lat
    g = tm // n_ctx
    cblk = n_lat // n_ctx
    return pl.pallas_call(
        functools.partial(_mlp_kernel, nf=nf, rows=tm, saxis=1), grid=(B // g, ns),
        in_specs=[pl.BlockSpec((g, n_ctx, D), lambda i, s: (i, cblk, 0)),
                  pl.BlockSpec((g, n_ctx, tn), lambda i, s: (i, cblk, col(s))),
                  pl.BlockSpec((None, 6, D), lambda i, s: (B, 0, 0)),
                  pl.BlockSpec((None, 6, tn), lambda i, s: (B, 0, col(s))),
                  pl.BlockSpec((1, D), lambda i, s: (0, 0)),
                  pl.BlockSpec((D, tf), lambda i, s: (0, jnp.minimum(s, nf - 1))),
                  pl.BlockSpec((None, nf, tf, tn), lambda i, s: (col(s), 0, 0, 0)),
                  pl.BlockSpec(memory_space=pl.ANY)],
        out_specs=pl.BlockSpec((g, n_ctx, tn), lambda i, s: (i, cblk, col(s))),
        out_shape=jax.ShapeDtypeStruct((B, Lt, D), F32),
        input_output_aliases={7: 0},
        scratch_shapes=scratch,
        compiler_params=_cparams(("parallel", "arbitrary"), 58), name="mlp_ctx",
    )(X, X, mod_l, mod_l, nwr, w1, w2r, lat)


def _pack_w_in(w_in):
    offs = [0, 512, 1024, 1536, 2048, 2560, 2576, 2960, 3088, 3152, 3664, 4176]
    u, mq, mk, mv, mo, mg, cq, ckv, kr, lx, lg = [w_in[..., offs[i]:offs[i + 1]] for i in range(11)]
    depth, d = w_in.shape[:2]
    mg = jnp.transpose(mg.reshape(depth, d, 2, 2, ML_HEADS), (0, 1, 4, 3, 2))[:, :, :, ::-1, :].reshape(depth, d, 16)
    pad = jnp.zeros((depth, d, MISC_W - MLA_ROPE - 16), w_in.dtype)
    w_p = jnp.concatenate([u, mq, mk, mv, mo, cq, ckv, lx, lg, kr, mg, pad], axis=-1).astype(BF16)
    return w_p, jnp.transpose(mg, (0, 2, 1)).astype(BF16), jnp.transpose(mv, (0, 2, 1)).astype(BF16)


def _pack_mla(w_q_up, w_kv_up):
    depth = w_q_up.shape[0]
    wq = w_q_up.reshape(depth, MLA_Q_LORA, MLA_HEADS, MLA_QK)
    wq = jnp.pad(wq, ((0, 0), (0, 0), (0, 0), (0, MLA_PAD - MLA_QK))).reshape(depth, MLA_Q_LORA, -1).astype(BF16)
    wkv = w_kv_up.reshape(depth, MLA_KV_LORA, MLA_HEADS, MLA_NOPE + MLA_V)
    wkv = jnp.concatenate([wkv[..., :MLA_NOPE].reshape(depth, MLA_KV_LORA, -1),
                           wkv[..., MLA_NOPE:].reshape(depth, MLA_KV_LORA, -1)], axis=-1).astype(BF16)
    return wq, wkv


def _pack_lru(wa, ba, wx, bx):
    wab = jnp.concatenate([wa, wx], axis=-1).astype(BF16)
    bias = jnp.stack([ba, bx], axis=2).astype(F32)
    return wab, bias


def kernel(x, c, ctx, c_ctx, ada_w, ada_b, norm1_w, norm2_w, w_in, w_out, s5_lam_re, s5_lam_im, s5_log_dt, s5_b_re, s5_b_im, s5_c_re, s5_c_im, s5_d, s5_glu_w, s5_glu_b, ml_ig_bias, ml_fg_bias, ml_out_norm, mla_q_a_norm, mla_w_q_up, mla_kv_a_norm, mla_w_kv_up, mla_q_norm, mla_k_norm, lru_conv_w, lru_conv_b, lru_wa, lru_ba, lru_wx, lru_bx, lru_lam, mlp_w1, mlp_w2):
    B, L, D = x.shape
    Lc = ctx.shape[1]
    depth = ada_w.shape[0]
    n_lat_tiles = L // TOKEN_TILE
    X = jnp.concatenate([x, ctx], axis=1).astype(F32)
    mod = _ada_all(c, c_ctx, ada_w, ada_b)
    w_p, wgt, wvt = _pack_w_in(w_in)
    wq, wkv = _pack_mla(mla_w_q_up, mla_w_kv_up)
    wab, lru_bias = _pack_lru(lru_wa, lru_ba, lru_wx, lru_bx)
    tables = _rope_tables(L, Lc)
    w_out_b, glu_w_b = w_out.astype(BF16), s5_glu_w.astype(BF16)
    w1_b = mlp_w1.astype(BF16)
    ff = mlp_w1.shape[2]
    w2_b = jnp.transpose(mlp_w2.reshape(depth, ff // MLP_FF_TILE, MLP_FF_TILE, D // MLP_OUT_TILE, MLP_OUT_TILE),
                         (0, 3, 1, 2, 4)).astype(BF16)
    for l in range(depth):
        u, mq, mk, mv, mo, cq, ckv, lx, lg, misc, mgt = _win(X, mod[l], norm1_w[l], w_p[l], wgt[l], wvt[l],
                                                             n_lat_tiles)
        s5p = _s5_params(s5_lam_re[l], s5_lam_im[l], s5_log_dt[l], s5_b_re[l], s5_b_im[l], s5_c_re[l], s5_c_im[l])
        a_mix, r_mix = _recurrent(u, lx, lg, s5p, s5_d[l], glu_w_b[l], s5_glu_b[l], lru_conv_w[l], lru_conv_b[l],
                                  wab[l], lru_bias[l], lru_lam[l], L)
        b_mix = _mlstm(mq, mk, mv, mo, misc, mgt, ml_ig_bias[l], ml_fg_bias[l], ml_out_norm[l], L)
        m_mix = _mla(cq, ckv, misc, tables, mla_q_a_norm[l], wq[l], mla_kv_a_norm[l], wkv[l],
                     mla_q_norm[l], mla_k_norm[l], L)
        X = _wout(X, mod[l], a_mix, b_mix, m_mix, r_mix, w_out_b[l], n_lat_tiles)
        X = _mlp(X, mod[l], norm2_w[l], w1_b[l], w2_b[l], L, with_ctx=l < depth - 1)
    return X
```

```python
import functools

import jax
import jax.numpy as jnp
from jax import lax
from jax.experimental import pallas as pl
from jax.experimental.pallas import tpu as pltpu

F32 = jnp.float32
BF16 = jnp.bfloat16
EPS = 1e-6
HIGHEST = lax.Precision.HIGHEST

GROUP_W = 512
TOKEN_TILE = 256
S5_STATE = 64
S5_CH = 16
S5_STRIPS = 4
S5_STRIP_STATES = 512
S5_MIN_NEG = -1e-4
S5_CHUNK = 64
ML_HEADS = 4
ML_DH = 128
ML_CHUNK = 128
ML_HEADS_PER_STEP = 2
MLA_HEADS = 4
MLA_NOPE = 128
MLA_ROPE = 64
MLA_QK = MLA_NOPE + MLA_ROPE
MLA_V = 128
MLA_Q_LORA = 384
MLA_KV_LORA = 128
MLA_PAD = 256
ATTN_SCALE = MLA_QK ** -0.5
ROPE_BASE = 10000.0
GRID_W = 64
Q_TILE = 1024
Q_SUB = 256
LRU_BLOCKS = 4
LRU_BLOCK_W = 128
LRU_CONV = 4
LRU_C = 8.0
MLP_FF_TILE = 2048
MLP_OUT_TILE = 512
MLP_ROWS = 512

COL_U, COL_MQ, COL_MK, COL_MV, COL_MO = 0, 512, 1024, 1536, 2048
COL_CQ, COL_CKV, COL_LX, COL_LG, COL_MISC = 2560, 2944, 3072, 3584, 4096
IN_COLS_PACKED = 4224
MISC_W = 128


def _cparams(sem, vmem_mb):
    return pltpu.CompilerParams(dimension_semantics=sem, vmem_limit_bytes=vmem_mb << 20)


def _norm_mod(x, nw, shift, scale):
    ms = jnp.mean(x * x, axis=-1, keepdims=True)
    y = (x * lax.rsqrt(ms + EPS)) * nw
    return (y * (1.0 + scale) + shift).astype(BF16)


def _log_sigmoid(x):
    return jnp.minimum(x, 0.0) - jnp.log1p(jnp.exp(-jnp.abs(x)))


def _softplus(x):
    return jnp.maximum(x, 0.0) + jnp.log1p(jnp.exp(-jnp.abs(x)))


def _ada_kernel(c_ref, w_ref, b_ref, o_ref):
    c = c_ref[...]
    act = (c * jax.nn.sigmoid(c)).astype(BF16)
    o_ref[...] = jnp.dot(act, w_ref[...].astype(BF16), preferred_element_type=F32) + b_ref[...]


def _ada_all(c, c_ctx, ada_w, ada_b):
    depth, d, n6 = ada_w.shape
    nb = c.shape[0]
    rows = nb + 8
    cc = jnp.concatenate([c.astype(F32), jnp.broadcast_to(c_ctx.astype(F32)[None], (8, d))], axis=0)
    tn = 1024
    out = pl.pallas_call(
        _ada_kernel,
        grid=(depth, n6 // tn),
        in_specs=[pl.BlockSpec((rows, d), lambda l, n: (0, 0)),
                  pl.BlockSpec((None, d, tn), lambda l, n: (l, 0, n)),
                  pl.BlockSpec((None, 1, tn), lambda l, n: (l, 0, n))],
        out_specs=pl.BlockSpec((None, rows, tn), lambda l, n: (l, 0, n)),
        out_shape=jax.ShapeDtypeStruct((depth, rows, n6), F32),
        compiler_params=_cparams(("parallel", "parallel"), 40),
        name="ada_mod",
    )(cc, ada_w, ada_b.reshape(depth, 1, n6))
    return out.reshape(depth, rows, 6, d)


def _tile_rows(x_refs, n_lat_tiles):
    if len(x_refs) == 1:
        return x_refs[0][...]
    return jnp.where(pl.program_id(1) >= n_lat_tiles, x_refs[1][...], x_refs[0][...])


def _win_kernel(*refs, n_src, n_lat_tiles):
    x_refs, refs = refs[:n_src], refs[n_src:]
    (mod_ref, nw_ref, w_ref, wgt_ref, wvt_ref, cos_ref, sin_ref, qan_ref, wq_ref, kvan_ref, wkv_ref, qn_ref, kn_ref,
     u_ref, mq_ref, mk_ref, mvt_ref, mo_ref, lx_ref, lg_ref, misc_ref, mgt_ref, qp_ref, kp_ref, v_ref) = refs
    h = _norm_mod(_tile_rows(x_refs, n_lat_tiles), nw_ref[...], mod_ref[0:1, :], mod_ref[1:2, :])
    nt_dims = (((1,), (1,)), ((), ()))

    def mm(lo, width):
        return jnp.dot(h, w_ref[:, lo:lo + width], preferred_element_type=F32)

    def store(ref, lo, scale=None):
        def run():
            y = mm(lo, ref.shape[-1])
            ref[...] = (y if scale is None else y * scale).astype(ref.dtype)
        return run

    def values_t():
        mvt = lax.dot_general(wvt_ref[...], h, nt_dims, preferred_element_type=F32).astype(BF16)
        for cc in range(mvt_ref.shape[0]):
            mvt_ref[cc] = mvt[:, cc * ML_CHUNK:(cc + 1) * ML_CHUNK]

    def gates_t():
        mgt_ref[...] = lax.dot_general(wgt_ref[...], h, nt_dims, preferred_element_type=F32)

    misc = mm(COL_MISC, MISC_W)
    misc_ref[...] = misc
    prep = _mla_prep_parts(mm(COL_CQ, MLA_Q_LORA), mm(COL_CKV, MLA_KV_LORA), misc, cos_ref, sin_ref, qan_ref,
                           wq_ref, kvan_ref, wkv_ref, qn_ref, kn_ref, qp_ref, kp_ref, v_ref)
    proj = [store(u_ref, COL_U), store(mq_ref, COL_MQ), store(mk_ref, COL_MK, ML_DH ** -0.5), values_t,
            store(mo_ref, COL_MO), store(lx_ref, COL_LX), store(lg_ref, COL_LG), gates_t]
    for k in range(max(len(prep), len(proj))):
        if k < len(proj):
            proj[k]()
        if k < len(prep):
            prep[k]()


def _residual_specs(src, tm, n_lat_tiles):
    D = src[0].shape[-1]
    if len(src) == 1:
        return [pl.BlockSpec((None, tm, D), lambda b, t: (b, t, 0))]
    return [pl.BlockSpec((None, tm, D), lambda b, t: (b, jnp.minimum(t, n_lat_tiles - 1), 0)),
            pl.BlockSpec((None, tm, D), lambda b, t: (b, jnp.maximum(t - n_lat_tiles, 0), 0))]


def _win(src, mod_l, nw, w_p, wgt, wvt, mla_w, tables, layer, n_lat_tiles):
    B, _, D = src[0].shape
    Lt = sum(a.shape[1] for a in src)
    tm = TOKEN_TILE
    nt = Lt // tm
    hp = MLA_HEADS * MLA_PAD
    bm = lambda b, t: (b, t, 0)
    per_layer = lambda *shape: pl.BlockSpec((None,) + shape, lambda b, t: (layer,) + (0,) * len(shape))
    table = pl.BlockSpec((tm, 128), lambda b, t: (t, 0))
    in_specs = _residual_specs(src, tm, n_lat_tiles) + [
        pl.BlockSpec((None, 6, D), lambda b, t: (jnp.where(t >= n_lat_tiles, B, b), 0, 0)),
        pl.BlockSpec((1, D), lambda b, t: (0, 0)),
        pl.BlockSpec((None, D, IN_COLS_PACKED), lambda b, t: (layer, 0, 0), pipeline_mode=pl.Buffered(1)),
        per_layer(16, D), per_layer(GROUP_W, D), table, table,
        per_layer(1, MLA_Q_LORA), per_layer(MLA_Q_LORA, hp),
        per_layer(1, MLA_KV_LORA), per_layer(MLA_KV_LORA, MLA_HEADS * (MLA_NOPE + MLA_V)),
        per_layer(1, MLA_PAD), per_layer(1, MLA_PAD),
    ]
    cpt = tm // ML_CHUNK
    out_specs = [
        pl.BlockSpec((None, tm, GROUP_W), bm),
        pl.BlockSpec((None, tm, GROUP_W), bm),
        pl.BlockSpec((None, tm, GROUP_W), bm),
        pl.BlockSpec((None, cpt, GROUP_W, ML_CHUNK), lambda b, t: (b, t, 0, 0)),
        pl.BlockSpec((None, tm, GROUP_W), bm),
        pl.BlockSpec((None, tm, GROUP_W), bm),
        pl.BlockSpec((None, tm, GROUP_W), bm),
        pl.BlockSpec((None, tm, MISC_W), bm),
        pl.BlockSpec((None, 16, tm), lambda b, t: (b, 0, t)),
        pl.BlockSpec((None, tm, hp), bm),
        pl.BlockSpec((None, tm, hp), bm),
        pl.BlockSpec((None, tm, GROUP_W), bm),
    ]
    sds = jax.ShapeDtypeStruct
    out_shape = [
        sds((B, Lt, GROUP_W), F32),
        sds((B, Lt, GROUP_W), BF16), sds((B, Lt, GROUP_W), BF16),
        sds((B, Lt // ML_CHUNK, GROUP_W, ML_CHUNK), BF16),
        sds((B, Lt, GROUP_W), F32),
        sds((B, Lt, GROUP_W), F32), sds((B, Lt, GROUP_W), F32),
        sds((B, Lt, MISC_W), F32),
        sds((B, 16, Lt), F32),
        sds((B, Lt, hp), BF16), sds((B, Lt, hp), BF16), sds((B, Lt, GROUP_W), BF16),
    ]
    return pl.pallas_call(
        functools.partial(_win_kernel, n_src=len(src), n_lat_tiles=n_lat_tiles),
        grid=(B, nt), in_specs=in_specs, out_specs=out_specs, out_shape=out_shape,
        compiler_params=_cparams(("parallel", "parallel"), 56), name="in_proj",
    )(*src, mod_l, nw.reshape(1, D), w_p, wgt, wvt, tables[0], tables[1], *mla_w)


def _rec_kernel(*refs, tc, nb, nlc, nc, reverse, final):
    if final:
        (ub_ref, yp_ref, wd_ref, a_ref, wc_ref, dsk_ref, gw_ref, gb_ref,
         x_ref, xp_ref, xn_ref, hp_ref, g_ref, cw_ref, cb_ref, w_ref, bias_ref, lam_ref,
         oa_ref, or_ref, u_ref, buf, st, xpad, abuf, bbuf, lst, ybuf, o3) = refs
    else:
        (ub_ref, wd_ref, a_ref, wc_ref,
         x_ref, xp_ref, xn_ref, cw_ref, cb_ref, w_ref, bias_ref, lam_ref,
         oa_ref, or_ref, u_ref, buf, st, xpad, abuf, bbuf, lst) = refs
    i = pl.program_id(0)
    c = (nc - 1 - i) if reverse else lax.rem(i + nlc, nc)
    R = tc * nb
    RB = min(R, 512)
    RL = min(R, 256)
    NS = S5_STRIP_STATES
    nh = nb // 8
    left = LRU_CONV // 2
    halo = xp_ref.shape[1]

    @pl.when(i == 0)
    def _():
        st[...] = jnp.zeros_like(st)
        lst[...] = jnp.zeros_like(lst)

    starts = jnp.logical_or(c == 0, c == nlc)
    ends = jnp.logical_or(c == nlc - 1, c == nc - 1)
    for j in range(left):
        xpad[j * nb:(j + 1) * nb, :] = jnp.where(starts, 0.0, xp_ref[:, halo - left + j, :])
    for t in range(tc):
        u_ref[t * nb:(t + 1) * nb, :] = ub_ref[:, t, :]
        xpad[(left + t) * nb:(left + t + 1) * nb, :] = x_ref[:, t, :]
    for j in range(LRU_CONV - 1 - left):
        xpad[(left + tc + j) * nb:(left + tc + j + 1) * nb, :] = jnp.where(ends, 0.0, xn_ref[:, j, :])
    sp = _softplus(-lam_ref[...])

    def lru_gate_parts(rb):
        r0 = rb * RL
        cache = {}

        def conv():
            if "xs" not in cache:
                xs = cb_ref[...]
                for j in range(LRU_CONV):
                    xs = xs + xpad[r0 + j * nb:r0 + j * nb + RL, :] * cw_ref[j:j + 1, :]
                cache["xs"] = xs
            return cache["xs"]

        def block(n):
            cs = slice(n * LRU_BLOCK_W, (n + 1) * LRU_BLOCK_W)
            xs = conv()[:, cs]
            z = jnp.dot(xs.astype(BF16), w_ref[n], preferred_element_type=F32)
            r = jax.nn.sigmoid(z[:, :LRU_BLOCK_W] + bias_ref[0:1, cs])
            ig = jax.nn.sigmoid(z[:, LRU_BLOCK_W:] + bias_ref[1:2, cs])
            log_a = (-LRU_C * r) * sp[:, cs]
            abuf[r0:r0 + RL, cs] = jnp.exp(log_a)
            th = jnp.tanh(log_a)
            bbuf[r0:r0 + RL, cs] = jnp.sqrt(-2.0 * th / (1.0 - th)) * (ig * xs)

        return [functools.partial(block, n) for n in range(LRU_BLOCKS)]

    def s5_drive_parts(s):
        cs = slice(s * 128, (s + 1) * 128)

        def part(rb):
            rs = slice(rb * RB, (rb + 1) * RB)
            buf[s % 2, rs, :] = jnp.dot(u_ref[rs, cs].astype(BF16), wd_ref[s], preferred_element_type=F32)

        return [functools.partial(part, rb) for rb in range(R // RB)]

    def s5_scan(s):
        sb = buf.at[s % 2]
        ar = jnp.broadcast_to(a_ref[s, 0:1, :], (8, NS))
        ai = jnp.broadcast_to(a_ref[s, 1:2, :], (8, NS))
        init = []
        for hh in range(nh):
            init += [st[s, hh * 8:(hh + 1) * 8, 0:NS], st[s, hh * 8:(hh + 1) * 8, NS:2 * NS]]

        def body(k, carry):
            t = (tc - 1 - k) if reverse else k
            out = []
            for hh in range(nh):
                sr, si = carry[2 * hh], carry[2 * hh + 1]
                row = pl.multiple_of(t * nb + hh * 8, 8)
                br = sb[pl.ds(row, 8), 0:NS]
                bi = sb[pl.ds(row, 8), NS:2 * NS]
                nsr = ar * sr - ai * si + br
                nsi = ar * si + ai * sr + bi
                sb[pl.ds(row, 8), 0:NS] = nsr
                sb[pl.ds(row, 8), NS:2 * NS] = nsi
                out += [nsr, nsi]
            return tuple(out)

        fin = lax.fori_loop(0, tc, body, tuple(init), unroll=2)
        for hh in range(nh):
            st[s, hh * 8:(hh + 1) * 8, 0:NS] = fin[2 * hh]
            st[s, hh * 8:(hh + 1) * 8, NS:2 * NS] = fin[2 * hh + 1]

    def s5_readout_parts(s):
        cs = slice(s * 128, (s + 1) * 128)

        def part(rb):
            rs = slice(rb * RB, (rb + 1) * RB)
            y = jnp.dot(buf[s % 2, rs, :].astype(BF16), wc_ref[s], preferred_element_type=F32)
            if final:
                ybuf[rs, cs] = y
            else:
                oa_ref[rs, cs] = y

        return [functools.partial(part, rb) for rb in range(R // RB)]

    def emit_interleaved(mxu_parts, vpu_parts):
        n = max(len(mxu_parts), len(vpu_parts))
        for k in range(n):
            if k < len(mxu_parts):
                mxu_parts[k]()
            if k < len(vpu_parts):
                vpu_parts[k]()

    n_gate_blocks = R // RL
    for s in range(S5_STRIPS + 1):
        mxu = (s5_readout_parts(s - 1) if s > 0 else []) + (s5_drive_parts(s) if s < S5_STRIPS else [])
        vpu = []
        for rb in range(n_gate_blocks):
            if rb % S5_STRIPS == s:
                vpu += lru_gate_parts(rb)
        emit_interleaved(mxu, vpu)
        if s < S5_STRIPS:
            s5_scan(s)

    init = tuple(lst[hh * 8:(hh + 1) * 8, :] for hh in range(nh))

    def lru_body(k, carry):
        t = (tc - 1 - k) if reverse else k
        out = []
        for hh in range(nh):
            row = pl.multiple_of(t * nb + hh * 8, 8)
            hnew = abuf[pl.ds(row, 8), :] * carry[hh] + bbuf[pl.ds(row, 8), :]
            bbuf[pl.ds(row, 8), :] = hnew
            out.append(hnew)
        return tuple(out)

    fin = lax.fori_loop(0, tc, lru_body, init, unroll=4)
    for hh in range(nh):
        lst[hh * 8:(hh + 1) * 8, :] = fin[hh]

    if not final:
        or_ref[...] = bbuf[...]
        return
    for rb in range(R // RB):
        rs = slice(rb * RB, (rb + 1) * RB)
        y = ybuf[rs, :] + yp_ref[rs, :] + dsk_ref[...] * u_ref[rs, :]
        g = jax.nn.gelu(y)
        z = jnp.dot(g.astype(BF16), gw_ref[...], preferred_element_type=F32) + gb_ref[...]
        res = g * jax.nn.sigmoid(z)
        for tt in range(RB // nb):
            o3[:, rb * (RB // nb) + tt, :] = res[tt * nb:(tt + 1) * nb, :]
    oa_ref[...] = o3[...].astype(BF16)
    for t in range(tc):
        rows = slice(t * nb, (t + 1) * nb)
        o3[:, t, :] = hp_ref[rows, :] + bbuf[rows, :]
    or_ref[...] = (o3[...] * jax.nn.gelu(g_ref[...])).astype(BF16)


def _chunk_order(nc, nlc, reverse):
    if reverse:
        return lambda i: nc - 1 - i
    return lambda i: lax.rem(i + nlc, nc)


def _recurrent(u, lx, lg, s5p, d_skip, glu_w, glu_b, conv_w, conv_b, wab, bias, lam, n_lat):
    nb, lt, _ = u.shape
    rows = nb * lt
    tc = S5_CHUNK
    R = tc * nb
    nc, nlc = lt // tc, n_lat // tc
    halo = 8
    wd, a, wc = s5p
    fwd = None
    for d in range(2):
        final = d == 1
        cidx = _chunk_order(nc, nlc, reverse=final)
        row_spec = pl.BlockSpec((R, GROUP_W), lambda i: (cidx(i), 0))
        bt_spec = pl.BlockSpec((nb, tc, GROUP_W), lambda i: (0, cidx(i), 0))
        prev_spec = pl.BlockSpec((nb, halo, GROUP_W),
                                 lambda i: (0, jnp.maximum(cidx(i) * (tc // halo) - 1, 0), 0))
        next_spec = pl.BlockSpec((nb, halo, GROUP_W),
                                 lambda i: (0, jnp.minimum((cidx(i) + 1) * (tc // halo), lt // halo - 1), 0))
        full = lambda shape: pl.BlockSpec(shape, lambda i: (0,) * len(shape))
        s5_w = [full((S5_STRIPS, 128, 2 * S5_STRIP_STATES)), full((S5_STRIPS, 2, S5_STRIP_STATES)),
                full((S5_STRIPS, 2 * S5_STRIP_STATES, 128))]
        lru_w = [full((LRU_CONV, GROUP_W)), full((1, GROUP_W)), full((LRU_BLOCKS, LRU_BLOCK_W, 2 * LRU_BLOCK_W)),
                 full((2, GROUP_W)), full((1, GROUP_W))]
        lru_args = [conv_w, conv_b.reshape(1, GROUP_W), wab[d], bias[d], lam[d].reshape(1, GROUP_W)]
        if final:
            in_specs = ([bt_spec, row_spec] + s5_w + [full((1, GROUP_W)), full((GROUP_W, GROUP_W)), full((1, GROUP_W))]
                        + [bt_spec, prev_spec, next_spec, row_spec, bt_spec] + lru_w)
            args = ([u, fwd[0], wd[d], a[d], wc[d], d_skip.reshape(1, GROUP_W), glu_w, glu_b.reshape(1, GROUP_W)]
                    + [lx, lx, lx, fwd[1], lg] + lru_args)
            out_specs = [bt_spec, bt_spec]
            out_shape = [jax.ShapeDtypeStruct((nb, lt, GROUP_W), BF16)] * 2
        else:
            in_specs = [bt_spec] + s5_w + [bt_spec, prev_spec, next_spec] + lru_w
            args = [u, wd[d], a[d], wc[d], lx, lx, lx] + lru_args
            out_specs = [row_spec, row_spec]
            out_shape = [jax.ShapeDtypeStruct((rows, GROUP_W), F32)] * 2
        scratch = [pltpu.VMEM((R, GROUP_W), F32), pltpu.VMEM((2, R, 2 * S5_STRIP_STATES), F32),
                   pltpu.VMEM((S5_STRIPS, nb, 2 * S5_STRIP_STATES), F32),
                   pltpu.VMEM((R + 3 * nb, GROUP_W), F32), pltpu.VMEM((R, GROUP_W), F32),
                   pltpu.VMEM((R, GROUP_W), F32), pltpu.VMEM((nb, GROUP_W), F32)]
        if final:
            scratch += [pltpu.VMEM((R, GROUP_W), F32), pltpu.VMEM((nb, tc, GROUP_W), F32)]
        fwd = pl.pallas_call(
            functools.partial(_rec_kernel, tc=tc, nb=nb, nlc=nlc, nc=nc, reverse=final, final=final),
            grid=(nc,), in_specs=in_specs, out_specs=out_specs, out_shape=out_shape,
            scratch_shapes=scratch,
            compiler_params=_cparams(("arbitrary",), 56), name="rec_rev" if final else "rec_fwd",
        )(*args)
    return fwd


def _s5_params(lam_re, lam_im, log_dt, b_re, b_im, c_re, c_im):
    lr = jnp.minimum(lam_re.astype(F32), S5_MIN_NEG)
    li = lam_im.astype(F32)
    dt = jnp.exp(log_dt.astype(F32))[..., None]
    mag = jnp.exp(lr * dt)
    ar, ai = mag * jnp.cos(li * dt), mag * jnp.sin(li * dt)
    den = lr * lr + li * li
    fr = ((ar - 1.0) * lr + ai * li) / den
    fi = (ai * lr - (ar - 1.0) * li) / den
    bbr = fr[..., None] * b_re - fi[..., None] * b_im
    bbi = fr[..., None] * b_im + fi[..., None] * b_re
    gs = 128 // S5_CH
    eye = jnp.eye(gs, dtype=F32)

    def drive(bb):
        t = bb.reshape(2, S5_STRIPS, gs, S5_STATE, S5_CH)
        return jnp.einsum('dsgpc,gh->dsgchp', t, eye).reshape(2, S5_STRIPS, 128, S5_STRIP_STATES)

    def read(cc):
        t = cc.reshape(2, S5_STRIPS, gs, S5_CH, S5_STATE)
        return jnp.einsum('dsgcp,gh->dsgphc', t, eye).reshape(2, S5_STRIPS, S5_STRIP_STATES, 128)

    wd = jnp.concatenate([drive(bbr), drive(bbi)], axis=-1).astype(BF16)
    wc = jnp.concatenate([read(c_re.astype(F32)), read(-c_im.astype(F32))], axis=-2).astype(BF16)
    a = jnp.stack([ar.reshape(2, S5_STRIPS, S5_STRIP_STATES), ai.reshape(2, S5_STRIPS, S5_STRIP_STATES)], axis=2)
    return wd, a, wc


def _mlstm_kernel(q_ref, k_ref, vt_ref, o_ref, gr_ref, gc_ref, br_ref, bc_ref, nw_ref, out_ref,
                  xb_scr, row_scr, nt_scr, u_scr, vec_scr, *, nch, nlat, hp):
    T = ML_CHUNK
    nt_dims = (((1,), (1,)), ((), ()))
    rid = lax.broadcasted_iota(jnp.int32, (T, T), 0)
    cid = lax.broadcasted_iota(jnp.int32, (T, T), 1)
    incl_rows = (rid <= cid).astype(F32)
    incl_cols = (rid >= cid).astype(F32)

    for j in range(hp):
        g = (gr_ref[j] + br_ref[j]).reshape(nch * 8, T)
        kind = lax.broadcasted_iota(jnp.int32, (nch * 8, T), 0) % 8
        lf = _log_sigmoid(g)
        pre = jnp.dot(lf, incl_rows, precision=HIGHEST, preferred_element_type=F32)
        b = jnp.where(kind == 1, pre[:, T - 1:T] - pre + lf, pre)
        x = g - pltpu.roll(b, 2, axis=0)
        row_scr[j] = jnp.where(kind < 2, b, x).reshape(nch, 8, T)
        gl = gc_ref[j] + bc_ref[j]
        kind = lax.broadcasted_iota(jnp.int32, (T, 128), 1) % 4
        lf = _log_sigmoid(gl)
        pre = jnp.dot(incl_cols, lf, precision=HIGHEST, preferred_element_type=F32)
        b = jnp.where(kind == 1, pre[T - 1:T, :] - pre + lf, pre)
        x = gl - pltpu.roll(b, 2, axis=1)
        for c in range(nch):
            for d in range(2):
                lane = c * 4 + 2 + d
                xb_scr[j, d, c] = jnp.broadcast_to(x[:, lane:lane + 1], (T, T))

    def independent(c, carry):
        rows = pl.ds(pl.multiple_of(c * T, T), T)
        pairs = [(j, d) for j in range(hp) for d in range(2)]
        hs = lambda j: slice(j * ML_DH, (j + 1) * ML_DH)
        swept = lambda d: (rid <= cid) if d == 0 else (rid >= cid)
        kq = {j: lax.dot_general(k_ref[rows, hs(j)], q_ref[rows, hs(j)], nt_dims, preferred_element_type=F32)
              for j in range(hp)}
        xm = {p: jnp.where(swept(p[1]), xb_scr[p[0], p[1], c], -jnp.inf) for p in pairs}
        a_row = {p: jnp.max(xm[p], axis=0, keepdims=True) for p in pairs}
        w0 = {}
        for j, d in pairs:
            a_last = a_row[j, d][:, T - 1:T] if d == 0 else a_row[j, d][:, 0:1]
            w0[j, d] = jnp.exp(row_scr[j, c, 2 + d:3 + d, :] - a_last)
        s0 = {p: kq[p[0]] * jnp.exp(xm[p] - a_row[p]) for p in pairs}
        for j, d in pairs:
            vt, kc = vt_ref[c, hs(j), :], k_ref[rows, hs(j)]
            nt_scr[j, d, c] = jnp.dot(vt, s0[j, d].astype(BF16), preferred_element_type=F32)
            wv = (vt.astype(F32) * w0[j, d]).astype(BF16)
            u_scr[j, d, c] = jnp.dot(wv, kc, preferred_element_type=F32)
            vec_scr[j, d, c, 0:1, :] = a_row[j, d]
            vec_scr[j, d, c, 1:2, :] = jnp.sum(s0[j, d], axis=0, keepdims=True)
            vec_scr[j, d, c, 2:3, :] = jnp.dot(jnp.broadcast_to(w0[j, d], (8, T)).astype(BF16), kc,
                                               preferred_element_type=F32)[0:1, :]
        return carry

    lax.fori_loop(0, nch, independent, 0)

    def pass2(c, d, j, state):
        cm, n, m = state
        hs = slice(j * ML_DH, (j + 1) * ML_DH)
        qc = q_ref[pl.ds(pl.multiple_of(c * T, T), T), hs]
        a_row, d0, n0 = vec_scr[j, d, c, 0:1, :], vec_scr[j, d, c, 1:2, :], vec_scr[j, d, c, 2:3, :]
        b_row = row_scr[j, c, d:d + 1, :]
        mu = jnp.maximum(m, a_row)
        inter, r = jnp.exp(m - mu), jnp.exp(a_row - mu)
        qct = lax.dot_general(cm.astype(BF16), qc, nt_dims, preferred_element_type=F32)
        qn = lax.dot_general(jnp.broadcast_to(n, (8, ML_DH)).astype(BF16), qc, nt_dims,
                             preferred_element_type=F32)[0:1, :]
        den = inter * qn + r * d0
        inv = 1.0 / jnp.maximum(jnp.abs(den), jnp.exp(-b_row - mu))
        nt_scr[j, d, c] = (inter * qct + r * nt_scr[j, d, c]) * inv
        a_last = a_row[:, T - 1:T] if d == 0 else a_row[:, 0:1]
        b_last = b_row[:, T - 1:T] if d == 0 else b_row[:, 0:1]
        mul = jnp.maximum(m, a_last)
        decay, rl = jnp.exp(m - mul), jnp.exp(a_last - mul)
        return decay * cm + rl * u_scr[j, d, c], decay * n + rl * n0, b_last + mul

    def states(i, carry):
        out = []
        for j in range(hp):
            out.append(pass2(lax.rem(i + nlat, nch), 0, j, carry[2 * j]))
            out.append(pass2(nch - 1 - i, 1, j, carry[2 * j + 1]))
        return tuple(out)

    zero = (jnp.zeros((ML_DH, ML_DH), F32), jnp.zeros((1, ML_DH), F32), jnp.zeros((1, 1), F32))
    lax.fori_loop(0, nch, states, (zero,) * (2 * hp))

    for c in range(nch):
        rows = slice(c * T, (c + 1) * T)
        for j in range(hp):
            hs = slice(j * ML_DH, (j + 1) * ML_DH)
            ht = nt_scr[j, 0, c] + nt_scr[j, 1, c]
            hn = ht * lax.rsqrt(jnp.mean(ht * ht, axis=0, keepdims=True) + EPS) * nw_ref[j]
            out_ref[rows, hs] = (hn.T * jax.nn.sigmoid(o_ref[rows, hs])).astype(BF16)


def _mlstm(mq, mk, mvt, mo, misc, mgt, ig_bias, fg_bias, out_norm, n_lat):
    B, Lt, _ = mq.shape
    T = ML_CHUNK
    nch, nlat = Lt // T, n_lat // T
    hp = ML_HEADS_PER_STEP
    assert 4 * nch <= 128
    gr = jnp.transpose(mgt.reshape(B, ML_HEADS, 4, nch, T), (0, 1, 3, 2, 4))
    gr = jnp.pad(gr, ((0, 0), (0, 0), (0, 0), (0, 4), (0, 0)))
    gc = misc[:, :, MLA_ROPE:MLA_ROPE + 16].reshape(B, nch, T, ML_HEADS, 4)
    gc = jnp.transpose(gc, (0, 3, 2, 1, 4)).reshape(B, ML_HEADS, T, nch * 4)
    gc = jnp.pad(gc, ((0, 0), (0, 0), (0, 0), (0, 128 - nch * 4)))
    kinds = jnp.concatenate([fg_bias.astype(F32), ig_bias.astype(F32)], axis=0).T
    br = jnp.broadcast_to(jnp.pad(kinds, ((0, 0), (0, 4)))[:, :, None], (ML_HEADS, 8, T))
    bc = jnp.pad(jnp.tile(kinds, (1, nch)), ((0, 0), (0, 128 - nch * 4)))[:, None, :]
    nwb = jnp.broadcast_to(out_norm.astype(F32)[:, :, None], (ML_HEADS, ML_DH, T))
    head_spec = pl.BlockSpec((None, Lt, hp * ML_DH), lambda b, h: (b, 0, h))
    slab = lambda: pltpu.VMEM((hp, 2, nch, T, T), F32)
    return pl.pallas_call(
        functools.partial(_mlstm_kernel, nch=nch, nlat=nlat, hp=hp),
        grid=(B, ML_HEADS // hp),
        in_specs=[head_spec, head_spec,
                  pl.BlockSpec((None, nch, hp * ML_DH, T), lambda b, h: (b, 0, h, 0)),
                  head_spec,
                  pl.BlockSpec((None, hp, nch, 8, T), lambda b, h: (b, h, 0, 0, 0)),
                  pl.BlockSpec((None, hp, T, 128), lambda b, h: (b, h, 0, 0)),
                  pl.BlockSpec((hp, 8, T), lambda b, h: (h, 0, 0)),
                  pl.BlockSpec((hp, 1, 128), lambda b, h: (h, 0, 0)),
                  pl.BlockSpec((hp, ML_DH, T), lambda b, h: (h, 0, 0))],
        out_specs=head_spec,
        out_shape=jax.ShapeDtypeStruct((B, Lt, GROUP_W), BF16),
        scratch_shapes=[slab(), pltpu.VMEM((hp, nch, 8, T), F32), slab(), slab(),
                        pltpu.VMEM((hp, 2, nch, 8, T), F32)],
        compiler_params=_cparams(("parallel", "parallel"), 52), name="mlstm",
    )(mq, mk, mvt, mo, gr, gc, br, bc, nwb)


def _mla_prep_parts(cq, ckv, misc, cos_ref, sin_ref, qan_ref, wq_ref, kvan_ref, wkv_ref, qn_ref, kn_ref,
                    qp_ref, kp_ref, v_ref):
    def rms(x, w):
        return (x * lax.rsqrt(jnp.mean(x * x, axis=-1, keepdims=True) + EPS)) * w

    tm = cq.shape[0]
    lane = lax.broadcasted_iota(jnp.int32, (tm, 128), 1)
    first = (lane % (MLA_ROPE // 2)) < (MLA_ROPE // 4)
    hw = MLA_HEADS * MLA_NOPE
    vals = {}

    def rope(x):
        sw = jnp.where(first, pltpu.roll(x, 128 - MLA_ROPE // 4, axis=1), pltpu.roll(x, MLA_ROPE // 4, axis=1))
        return x * cos_ref[...] + sw * sin_ref[...]

    def project():
        vals["q"] = jnp.dot(rms(cq, qan_ref[...]).astype(BF16), wq_ref[...], preferred_element_type=F32)
        kv = jnp.dot(rms(ckv, kvan_ref[...]).astype(BF16), wkv_ref[...], preferred_element_type=F32)
        vals["k"] = kv[:, :hw]
        v_ref[...] = kv[:, hw:].astype(BF16)
        kr = jnp.where(lane < MLA_ROPE, misc, 0.0)
        vals["kr_sq"] = jnp.sum(kr * kr, axis=-1, keepdims=True)
        vals["kr_rot"] = rope(kr * kn_ref[:, 128:256])

    def head(h):
        base = h * MLA_PAD
        qa, qb = vals["q"][:, base:base + 128], vals["q"][:, base + 128:base + 256]
        inv = lax.rsqrt(jnp.sum(qa * qa + qb * qb, axis=-1, keepdims=True) / MLA_QK + EPS)
        qp_ref[:, base:base + 128] = ((qa * inv) * (qn_ref[:, 0:128] * ATTN_SCALE)).astype(BF16)
        qp_ref[:, base + 128:base + 256] = rope((qb * inv) * (qn_ref[:, 128:256] * ATTN_SCALE)).astype(BF16)
        ka = vals["k"][:, h * MLA_NOPE:(h + 1) * MLA_NOPE]
        inv = lax.rsqrt((jnp.sum(ka * ka, axis=-1, keepdims=True) + vals["kr_sq"]) / MLA_QK + EPS)
        kp_ref[:, base:base + 128] = ((ka * inv) * kn_ref[:, 0:128]).astype(BF16)
        kp_ref[:, base + 128:base + 256] = (vals["kr_rot"] * inv).astype(BF16)

    return [project] + [functools.partial(head, h) for h in range(MLA_HEADS)]


def _attn_kernel(q_ref, k_ref, v_ref, o_ref, *, n_lat, n_lat_tiles, sub):
    def attend(n_sub, k, v):
        for first in range(0, n_sub, 2):
            blocks = [slice(i * sub, (i + 1) * sub) for i in range(first, min(first + 2, n_sub))]
            s = [lax.dot_general(q_ref[r, :], k, (((1,), (1,)), ((), ())), preferred_element_type=F32)
                 for r in blocks]
            p = [jnp.exp(si - jnp.max(si, axis=-1, keepdims=True)) for si in s]
            for r, pi in zip(blocks, p):
                den = jnp.sum(pi, axis=-1, keepdims=True)
                o_ref[r, :] = (jnp.dot(pi.astype(BF16), v, preferred_element_type=F32) / den).astype(BF16)

    qi = pl.program_id(2)

    @pl.when(qi < n_lat_tiles)
    def _():
        attend(q_ref.shape[0] // sub, k_ref[...], v_ref[...])

    @pl.when(qi >= n_lat_tiles)
    def _():
        attend((k_ref.shape[0] - n_lat) // sub, k_ref[n_lat:, :], v_ref[n_lat:, :])


def _rope_tables(n_lat, n_ctx):
    n_rows = n_lat // GRID_W
    rows = jnp.repeat(jnp.arange(n_rows, dtype=F32), GRID_W)
    cols = jnp.tile(jnp.arange(GRID_W, dtype=F32), n_rows)
    n_freq = MLA_ROPE // 4
    inv_freq = ROPE_BASE ** (-jnp.arange(n_freq, dtype=F32) / n_freq)
    ar, ac = rows[:, None] * inv_freq, cols[:, None] * inv_freq
    cos = jnp.concatenate([jnp.cos(ar), jnp.cos(ar), jnp.cos(ac), jnp.cos(ac)], axis=1)
    sins = jnp.concatenate([-jnp.sin(ar), jnp.sin(ar), -jnp.sin(ac), jnp.sin(ac)], axis=1)
    cos = jnp.pad(cos, ((0, n_ctx), (0, 128 - MLA_ROPE)), constant_values=1.0)
    sins = jnp.pad(sins, ((0, n_ctx), (0, 128 - MLA_ROPE)))
    return cos, sins


def _mla_attn(qp, kp, v, n_lat):
    B, Lt, _ = v.shape
    tq = min(Q_TILE, n_lat)
    return pl.pallas_call(
        functools.partial(_attn_kernel, n_lat=n_lat, n_lat_tiles=n_lat // tq, sub=Q_SUB),
        grid=(B, MLA_HEADS, pl.cdiv(Lt, tq)),
        in_specs=[pl.BlockSpec((None, tq, MLA_PAD), lambda b, h, t: (b, t, h)),
                  pl.BlockSpec((None, Lt, MLA_PAD), lambda b, h, t: (b, 0, h)),
                  pl.BlockSpec((None, Lt, MLA_V), lambda b, h, t: (b, 0, h))],
        out_specs=pl.BlockSpec((None, tq, MLA_V), lambda b, h, t: (b, t, h)),
        out_shape=jax.ShapeDtypeStruct((B, Lt, GROUP_W), BF16),
        compiler_params=_cparams(("parallel", "parallel", "arbitrary"), 40), name="mla_attn",
    )(qp, kp, v)


def _wout_kernel(*refs, n_src, n_lat_tiles):
    x_refs, (mod_ref, a_ref, b_ref, m_ref, r_ref, w_ref, o_ref) = refs[:n_src], refs[n_src:]
    acc = jnp.dot(a_ref[...], w_ref[0:GROUP_W, :], preferred_element_type=F32)
    acc += jnp.dot(b_ref[...], w_ref[GROUP_W:2 * GROUP_W, :], preferred_element_type=F32)
    acc += jnp.dot(m_ref[...], w_ref[2 * GROUP_W:3 * GROUP_W, :], preferred_element_type=F32)
    acc += jnp.dot(r_ref[...], w_ref[3 * GROUP_W:4 * GROUP_W, :], preferred_element_type=F32)
    o_ref[...] = _tile_rows(x_refs, n_lat_tiles) + mod_ref[2:3, :] * acc


def _wout(src, mod_l, a, b, m, r, w, layer, n_lat_tiles):
    B, _, D = src[0].shape
    Lt = sum(s.shape[1] for s in src)
    tm = TOKEN_TILE
    bm = lambda b, t: (b, t, 0)
    mix = pl.BlockSpec((None, tm, GROUP_W), bm)
    return pl.pallas_call(
        functools.partial(_wout_kernel, n_src=len(src), n_lat_tiles=n_lat_tiles), grid=(B, Lt // tm),
        in_specs=_residual_specs(src, tm, n_lat_tiles) + [
            pl.BlockSpec((None, 6, D), lambda b, t: (jnp.where(t >= n_lat_tiles, B, b), 0, 0)),
            mix, mix, mix, mix,
            pl.BlockSpec((None, 4 * GROUP_W, D), lambda b, t: (layer, 0, 0))],
        out_specs=pl.BlockSpec((None, tm, D), bm),
        out_shape=jax.ShapeDtypeStruct((B, Lt, D), F32),
        compiler_params=_cparams(("parallel", "parallel"), 48), name="out_proj",
    )(*src, mod_l, a, b, m, r, w)


def _mlp_kernel(x_ref, xc_ref, mod_ref, modc_ref, nw_ref, w1_ref, w2_ref, *rest, nf, rows, saxis):
    o_ref, h_ref, a_ref = rest[-3:]
    s = pl.program_id(saxis)

    @pl.when(s == 0)
    def _():
        x = x_ref[...].reshape(rows, x_ref.shape[-1])
        half = rows // 2
        for r in (slice(0, half), slice(half, rows)):
            h = _norm_mod(x[r], nw_ref[...], mod_ref[3:4, :], mod_ref[4:5, :])
            h_ref[r, :] = h
            a = jnp.maximum(jnp.dot(h, w1_ref[...], preferred_element_type=F32), 0.0)
            a_ref[0, r, :] = (a * a).astype(BF16)

    @pl.when(jnp.logical_and(s > 0, s < nf))
    def _():
        a = jnp.maximum(jnp.dot(h_ref[...], w1_ref[...], preferred_element_type=F32), 0.0)
        a_ref[s] = (a * a).astype(BF16)

    @pl.when(s >= nf)
    def _():
        y = jnp.dot(a_ref[0], w2_ref[0], preferred_element_type=F32)
        for f in range(1, nf):
            y += jnp.dot(a_ref[f], w2_ref[f], preferred_element_type=F32)
        o_ref[...] = xc_ref[...] + (modc_ref[5:6, :] * y).reshape(o_ref.shape)


def _mlp(X, mod_l, nw, w1, w2, layer, n_lat, with_ctx):
    B, Lt, D = X.shape
    n_ctx = Lt - n_lat
    ff = w1.shape[2]
    tm, tf, tn = MLP_ROWS, MLP_FF_TILE, MLP_OUT_TILE
    nf, nn = ff // tf, D // tn
    ns = nf + nn
    w2r = w2.reshape(w2.shape[0], nf, tf, D)
    nwr = nw.reshape(1, D)
    col = lambda s: jnp.maximum(s - nf, 0)
    scratch = [pltpu.VMEM((tm, D), BF16), pltpu.VMEM((nf, tm, tf), BF16)]
    lat = pl.pallas_call(
        functools.partial(_mlp_kernel, nf=nf, rows=tm, saxis=2), grid=(B, n_lat // tm, ns),
        in_specs=[pl.BlockSpec((None, tm, D), lambda b, t, s: (b, t, 0)),
                  pl.BlockSpec((None, tm, tn), lambda b, t, s: (b, t, col(s))),
                  pl.BlockSpec((None, 6, D), lambda b, t, s: (b, 0, 0)),
                  pl.BlockSpec((None, 6, tn), lambda b, t, s: (b, 0, col(s))),
                  pl.BlockSpec((1, D), lambda b, t, s: (0, 0)),
                  pl.BlockSpec((None, D, tf), lambda b, t, s: (layer, 0, jnp.minimum(s, nf - 1))),
                  pl.BlockSpec((None, nf, tf, tn), lambda b, t, s: (layer, 0, 0, col(s)))],
        out_specs=pl.BlockSpec((None, tm, tn), lambda b, t, s: (b, t, col(s))),
        out_shape=jax.ShapeDtypeStruct((B, Lt if with_ctx else n_lat, D), F32),
        scratch_shapes=scratch,
        compiler_params=_cparams(("parallel", "parallel", "arbitrary"), 58), name="mlp_lat",
    )(X, X, mod_l, mod_l, nwr, w1, w2r)
    if not with_ctx:
        return lat
    g = tm // n_ctx
    cblk = n_lat // n_ctx
    return pl.pallas_call(
        functools.partial(_mlp_kernel, nf=nf, rows=tm, saxis=1), grid=(B // g, ns),
        in_specs=[pl.BlockSpec((g, n_ctx, D), lambda i, s: (i, cblk, 0)),
                  pl.BlockSpec((g, n_ctx, tn), lambda i, s: (i, cblk, col(s))),
                  pl.BlockSpec((None, 6, D), lambda i, s: (B, 0, 0)),
                  pl.BlockSpec((None, 6, tn), lambda i, s: (B, 0, col(s))),
                  pl.BlockSpec((1, D), lambda i, s: (0, 0)),
                  pl.BlockSpec((None, D, tf), lambda i, s: (layer, 0, jnp.minimum(s, nf - 1))),
                  pl.BlockSpec((None, nf, tf, tn), lambda i, s: (layer, 0, 0, col(s))),
                  pl.BlockSpec(memory_space=pl.ANY)],
        out_specs=pl.BlockSpec((g, n_ctx, tn), lambda i, s: (i, cblk, col(s))),
        out_shape=jax.ShapeDtypeStruct((B, Lt, D), F32),
        input_output_aliases={7: 0},
        scratch_shapes=scratch,
        compiler_params=_cparams(("parallel", "arbitrary"), 58), name="mlp_ctx",
    )(X, X, mod_l, mod_l, nwr, w1, w2r, lat)


def _pack_w_in(w_in):
    offs = [0, 512, 1024, 1536, 2048, 2560, 2576, 2960, 3088, 3152, 3664, 4176]
    u, mq, mk, mv, mo, mg, cq, ckv, kr, lx, lg = [w_in[..., offs[i]:offs[i + 1]] for i in range(11)]
    depth, d = w_in.shape[:2]
    mg = jnp.transpose(mg.reshape(depth, d, 2, 2, ML_HEADS), (0, 1, 4, 3, 2))[:, :, :, ::-1, :].reshape(depth, d, 16)
    pad = jnp.zeros((depth, d, MISC_W - MLA_ROPE - 16), w_in.dtype)
    w_p = jnp.concatenate([u, mq, mk, mv, mo, cq, ckv, lx, lg, kr, mg, pad], axis=-1).astype(BF16)
    return w_p, jnp.transpose(mg, (0, 2, 1)).astype(BF16), jnp.transpose(mv, (0, 2, 1)).astype(BF16)


def _pack_mla(q_a_norm, w_q_up, kv_a_norm, w_kv_up, q_norm, k_norm):
    depth = w_q_up.shape[0]
    wq = w_q_up.reshape(depth, MLA_Q_LORA, MLA_HEADS, MLA_QK)
    wq = jnp.pad(wq, ((0, 0), (0, 0), (0, 0), (0, MLA_PAD - MLA_QK))).reshape(depth, MLA_Q_LORA, -1).astype(BF16)
    wkv = w_kv_up.reshape(depth, MLA_KV_LORA, MLA_HEADS, MLA_NOPE + MLA_V)
    wkv = jnp.concatenate([wkv[..., :MLA_NOPE].reshape(depth, MLA_KV_LORA, -1),
                           wkv[..., MLA_NOPE:].reshape(depth, MLA_KV_LORA, -1)], axis=-1).astype(BF16)
    row = lambda a: a.astype(F32)[:, None, :]
    head = lambda a: jnp.pad(a.astype(F32), ((0, 0), (0, MLA_PAD - MLA_QK)))[:, None, :]
    return row(q_a_norm), wq, row(kv_a_norm), wkv, head(q_norm), head(k_norm)


def _pack_lru(wa, ba, wx, bx):
    wab = jnp.concatenate([wa, wx], axis=-1).astype(BF16)
    bias = jnp.stack([ba, bx], axis=2).astype(F32)
    return wab, bias


def kernel(x, c, ctx, c_ctx, ada_w, ada_b, norm1_w, norm2_w, w_in, w_out, s5_lam_re, s5_lam_im, s5_log_dt, s5_b_re, s5_b_im, s5_c_re, s5_c_im, s5_d, s5_glu_w, s5_glu_b, ml_ig_bias, ml_fg_bias, ml_out_norm, mla_q_a_norm, mla_w_q_up, mla_kv_a_norm, mla_w_kv_up, mla_q_norm, mla_k_norm, lru_conv_w, lru_conv_b, lru_wa, lru_ba, lru_wx, lru_bx, lru_lam, mlp_w1, mlp_w2):
    B, L, D = x.shape
    Lc = ctx.shape[1]
    depth = ada_w.shape[0]
    n_lat_tiles = L // TOKEN_TILE
    src = (x.astype(F32), ctx.astype(F32))
    mod = _ada_all(c, c_ctx, ada_w, ada_b)
    w_p, wgt, wvt = _pack_w_in(w_in)
    mla_w = _pack_mla(mla_q_a_norm, mla_w_q_up, mla_kv_a_norm, mla_w_kv_up, mla_q_norm, mla_k_norm)
    wab, lru_bias = _pack_lru(lru_wa, lru_ba, lru_wx, lru_bx)
    tables = _rope_tables(L, Lc)
    w_out_b, glu_w_b = w_out.astype(BF16), s5_glu_w.astype(BF16)
    w1_b, w2_b = mlp_w1.astype(BF16), mlp_w2.astype(BF16)
    for l in range(depth):
        u, mq, mk, mv, mo, lx, lg, misc, mgt, qp, kp, v = _win(src, mod[l], norm1_w[l], w_p, wgt, wvt, mla_w, tables,
                                                               l, n_lat_tiles)
        s5p = _s5_params(s5_lam_re[l], s5_lam_im[l], s5_log_dt[l], s5_b_re[l], s5_b_im[l], s5_c_re[l], s5_c_im[l])
        a_mix, r_mix = _recurrent(u, lx, lg, s5p, s5_d[l], glu_w_b[l], s5_glu_b[l], lru_conv_w[l], lru_conv_b[l],
                                  wab[l], lru_bias[l], lru_lam[l], L)
        b_mix = _mlstm(mq, mk, mv, mo, misc, mgt, ml_ig_bias[l], ml_fg_bias[l], ml_out_norm[l], L)
        m_mix = _mla_attn(qp, kp, v, L)
        X = _wout(src, mod[l], a_mix, b_mix, m_mix, r_mix, w_out_b, l, n_lat_tiles)
        X = _mlp(X, mod[l], norm2_w[l], w1_b, w2_b, l, L, with_ctx=l < depth - 1)
        src = (X,)
    return X
```

```python
import functools

import jax
import jax.numpy as jnp
from jax import lax
from jax.experimental import pallas as pl
from jax.experimental.pallas import tpu as pltpu

F32 = jnp.float32
BF16 = jnp.bfloat16
EPS = 1e-6
HIGHEST = lax.Precision.HIGHEST

GROUP_W = 512
TOKEN_TILE = 256
S5_STATE = 64
S5_CH = 16
S5_STRIPS = 4
S5_STRIP_STATES = 512
S5_MIN_NEG = -1e-4
S5_CHUNK = 64
ML_HEADS = 4
ML_DH = 128
ML_CHUNK = 128
ML_HEADS_PER_STEP = 4
MLA_HEADS = 4
MLA_NOPE = 128
MLA_ROPE = 64
MLA_QK = MLA_NOPE + MLA_ROPE
MLA_V = 128
MLA_Q_LORA = 384
MLA_KV_LORA = 128
MLA_PAD = 256
ATTN_SCALE = MLA_QK ** -0.5
ROPE_BASE = 10000.0
GRID_W = 64
Q_TILE = 1024
Q_SUB = 256
LRU_BLOCKS = 4
LRU_BLOCK_W = 128
LRU_CONV = 4
LRU_C = 8.0
MLP_FF_TILE = 2048
MLP_OUT_TILE = 512
MLP_ROWS = 512

COL_U, COL_MQ, COL_MK, COL_MV, COL_MO = 0, 512, 1024, 1536, 2048
COL_CQ, COL_CKV, COL_LX, COL_LG, COL_MISC = 2560, 2944, 3072, 3584, 4096
IN_COLS_PACKED = 4224
MISC_W = 128


def _cparams(sem, vmem_mb):
    return pltpu.CompilerParams(dimension_semantics=sem, vmem_limit_bytes=vmem_mb << 20)


def _norm_mod(x, nw, shift, scale):
    ms = jnp.mean(x * x, axis=-1, keepdims=True)
    y = (x * lax.rsqrt(ms + EPS)) * nw
    return (y * (1.0 + scale) + shift).astype(BF16)


def _log_sigmoid(x):
    return jnp.minimum(x, 0.0) - jnp.log1p(jnp.exp(-jnp.abs(x)))


def _softplus(x):
    return jnp.maximum(x, 0.0) + jnp.log1p(jnp.exp(-jnp.abs(x)))


def _ada_kernel(c_ref, w_ref, b_ref, o_ref):
    c = c_ref[...]
    act = (c * jax.nn.sigmoid(c)).astype(BF16)
    o_ref[...] = jnp.dot(act, w_ref[...].astype(BF16), preferred_element_type=F32) + b_ref[...]


def _ada_all(c, c_ctx, ada_w, ada_b):
    depth, d, n6 = ada_w.shape
    nb = c.shape[0]
    rows = nb + 8
    cc = jnp.concatenate([c.astype(F32), jnp.broadcast_to(c_ctx.astype(F32)[None], (8, d))], axis=0)
    tn = 1024
    out = pl.pallas_call(
        _ada_kernel,
        grid=(depth, n6 // tn),
        in_specs=[pl.BlockSpec((rows, d), lambda l, n: (0, 0)),
                  pl.BlockSpec((None, d, tn), lambda l, n: (l, 0, n)),
                  pl.BlockSpec((None, 1, tn), lambda l, n: (l, 0, n))],
        out_specs=pl.BlockSpec((None, rows, tn), lambda l, n: (l, 0, n)),
        out_shape=jax.ShapeDtypeStruct((depth, rows, n6), F32),
        compiler_params=_cparams(("parallel", "parallel"), 40),
        name="ada_mod",
    )(cc, ada_w, ada_b.reshape(depth, 1, n6))
    return out.reshape(depth, rows, 6, d)


def _tile_rows(x_refs, n_lat_tiles):
    if len(x_refs) == 1:
        return x_refs[0][...]
    return jnp.where(pl.program_id(1) >= n_lat_tiles, x_refs[1][...], x_refs[0][...])


def _win_kernel(*refs, n_src, n_lat_tiles):
    x_refs, refs = refs[:n_src], refs[n_src:]
    (mod_ref, nw_ref, w_ref, wgt_ref, wvt_ref, cos_ref, sin_ref, qan_ref, wq_ref, kvan_ref, wkv_ref, qn_ref, kn_ref,
     u_ref, mq_ref, mk_ref, mvt_ref, mo_ref, lx_ref, lg_ref, misc_ref, mgt_ref, qp_ref, kp_ref, v_ref) = refs
    h = _norm_mod(_tile_rows(x_refs, n_lat_tiles), nw_ref[...], mod_ref[0:1, :], mod_ref[1:2, :])
    nt_dims = (((1,), (1,)), ((), ()))

    def mm(lo, width):
        return jnp.dot(h, w_ref[:, lo:lo + width], preferred_element_type=F32)

    def store(ref, lo, scale=None):
        def run():
            y = mm(lo, ref.shape[-1])
            ref[...] = (y if scale is None else y * scale).astype(ref.dtype)
        return run

    def values_t():
        mvt = lax.dot_general(wvt_ref[...], h, nt_dims, preferred_element_type=F32).astype(BF16)
        for cc in range(mvt_ref.shape[0]):
            mvt_ref[cc] = mvt[:, cc * ML_CHUNK:(cc + 1) * ML_CHUNK]

    def gates_t():
        mgt_ref[...] = lax.dot_general(wgt_ref[...], h, nt_dims, preferred_element_type=F32)

    misc = mm(COL_MISC, MISC_W)
    misc_ref[...] = misc
    prep = _mla_prep_parts(mm(COL_CQ, MLA_Q_LORA), mm(COL_CKV, MLA_KV_LORA), misc, cos_ref, sin_ref, qan_ref,
                           wq_ref, kvan_ref, wkv_ref, qn_ref, kn_ref, qp_ref, kp_ref, v_ref)
    proj = [store(u_ref, COL_U), store(mq_ref, COL_MQ), store(mk_ref, COL_MK, ML_DH ** -0.5), values_t,
            store(mo_ref, COL_MO), store(lx_ref, COL_LX), store(lg_ref, COL_LG), gates_t]
    for k in range(max(len(prep), len(proj))):
        if k < len(proj):
            proj[k]()
        if k < len(prep):
            prep[k]()


def _residual_specs(src, tm, n_lat_tiles):
    D = src[0].shape[-1]
    if len(src) == 1:
        return [pl.BlockSpec((None, tm, D), lambda b, t: (b, t, 0))]
    return [pl.BlockSpec((None, tm, D), lambda b, t: (b, jnp.minimum(t, n_lat_tiles - 1), 0)),
            pl.BlockSpec((None, tm, D), lambda b, t: (b, jnp.maximum(t - n_lat_tiles, 0), 0))]


def _win(src, mod_l, nw, w_p, wgt, wvt, mla_w, tables, layer, n_lat_tiles):
    B, _, D = src[0].shape
    Lt = sum(a.shape[1] for a in src)
    tm = TOKEN_TILE
    nt = Lt // tm
    hp = MLA_HEADS * MLA_PAD
    bm = lambda b, t: (b, t, 0)
    per_layer = lambda *shape: pl.BlockSpec((None,) + shape, lambda b, t: (layer,) + (0,) * len(shape))
    table = pl.BlockSpec((tm, 128), lambda b, t: (t, 0))
    in_specs = _residual_specs(src, tm, n_lat_tiles) + [
        pl.BlockSpec((None, 6, D), lambda b, t: (jnp.where(t >= n_lat_tiles, B, b), 0, 0)),
        pl.BlockSpec((1, D), lambda b, t: (0, 0)),
        pl.BlockSpec((None, D, IN_COLS_PACKED), lambda b, t: (layer, 0, 0), pipeline_mode=pl.Buffered(1)),
        per_layer(16, D), per_layer(GROUP_W, D), table, table,
        per_layer(1, MLA_Q_LORA), per_layer(MLA_Q_LORA, hp),
        per_layer(1, MLA_KV_LORA), per_layer(MLA_KV_LORA, MLA_HEADS * (MLA_NOPE + MLA_V)),
        per_layer(1, MLA_PAD), per_layer(1, MLA_PAD),
    ]
    cpt = tm // ML_CHUNK
    out_specs = [
        pl.BlockSpec((None, tm, GROUP_W), bm),
        pl.BlockSpec((None, tm, GROUP_W), bm),
        pl.BlockSpec((None, tm, GROUP_W), bm),
        pl.BlockSpec((None, cpt, GROUP_W, ML_CHUNK), lambda b, t: (b, t, 0, 0)),
        pl.BlockSpec((None, tm, GROUP_W), bm),
        pl.BlockSpec((None, tm, GROUP_W), bm),
        pl.BlockSpec((None, tm, GROUP_W), bm),
        pl.BlockSpec((None, tm, MISC_W), bm),
        pl.BlockSpec((None, 16, tm), lambda b, t: (b, 0, t)),
        pl.BlockSpec((None, tm, hp), bm),
        pl.BlockSpec((None, tm, hp), bm),
        pl.BlockSpec((None, tm, GROUP_W), bm),
    ]
    sds = jax.ShapeDtypeStruct
    out_shape = [
        sds((B, Lt, GROUP_W), F32),
        sds((B, Lt, GROUP_W), BF16), sds((B, Lt, GROUP_W), BF16),
        sds((B, Lt // ML_CHUNK, GROUP_W, ML_CHUNK), BF16),
        sds((B, Lt, GROUP_W), F32),
        sds((B, Lt, GROUP_W), F32), sds((B, Lt, GROUP_W), F32),
        sds((B, Lt, MISC_W), F32),
        sds((B, 16, Lt), F32),
        sds((B, Lt, hp), BF16), sds((B, Lt, hp), BF16), sds((B, Lt, GROUP_W), BF16),
    ]
    return pl.pallas_call(
        functools.partial(_win_kernel, n_src=len(src), n_lat_tiles=n_lat_tiles),
        grid=(B, nt), in_specs=in_specs, out_specs=out_specs, out_shape=out_shape,
        compiler_params=_cparams(("parallel", "parallel"), 56), name="in_proj",
    )(*src, mod_l, nw.reshape(1, D), w_p, wgt, wvt, tables[0], tables[1], *mla_w)


def _rec_kernel(*refs, tc, nb, nlc, nc, reverse, final):
    if final:
        (ub_ref, yp_ref, wd_ref, a_ref, wc_ref, dsk_ref, gw_ref, gb_ref,
         x_ref, xp_ref, xn_ref, hp_ref, g_ref, cw_ref, cb_ref, w_ref, bias_ref, lam_ref,
         oa_ref, or_ref, u_ref, buf, st, xpad, abuf, bbuf, lst, ybuf, o3) = refs
    else:
        (ub_ref, wd_ref, a_ref, wc_ref,
         x_ref, xp_ref, xn_ref, cw_ref, cb_ref, w_ref, bias_ref, lam_ref,
         oa_ref, or_ref, u_ref, buf, st, xpad, abuf, bbuf, lst) = refs
    i = pl.program_id(0)
    c = (nc - 1 - i) if reverse else lax.rem(i + nlc, nc)
    R = tc * nb
    RB = min(R, 512)
    RL = min(R, 256)
    NS = S5_STRIP_STATES
    nh = nb // 8
    left = LRU_CONV // 2
    halo = xp_ref.shape[1]

    @pl.when(i == 0)
    def _():
        st[...] = jnp.zeros_like(st)
        lst[...] = jnp.zeros_like(lst)

    starts = jnp.logical_or(c == 0, c == nlc)
    ends = jnp.logical_or(c == nlc - 1, c == nc - 1)
    for j in range(left):
        xpad[j * nb:(j + 1) * nb, :] = jnp.where(starts, 0.0, xp_ref[:, halo - left + j, :])
    for t in range(tc):
        u_ref[t * nb:(t + 1) * nb, :] = ub_ref[:, t, :]
        xpad[(left + t) * nb:(left + t + 1) * nb, :] = x_ref[:, t, :]
    for j in range(LRU_CONV - 1 - left):
        xpad[(left + tc + j) * nb:(left + tc + j + 1) * nb, :] = jnp.where(ends, 0.0, xn_ref[:, j, :])
    sp = _softplus(-lam_ref[...])

    def lru_gate_parts(rb):
        r0 = rb * RL
        cache = {}

        def conv():
            if "xs" not in cache:
                xs = cb_ref[...]
                for j in range(LRU_CONV):
                    xs = xs + xpad[r0 + j * nb:r0 + j * nb + RL, :] * cw_ref[j:j + 1, :]
                cache["xs"] = xs
            return cache["xs"]

        def block(n):
            cs = slice(n * LRU_BLOCK_W, (n + 1) * LRU_BLOCK_W)
            xs = conv()[:, cs]
            z = jnp.dot(xs.astype(BF16), w_ref[n], preferred_element_type=F32)
            r = jax.nn.sigmoid(z[:, :LRU_BLOCK_W] + bias_ref[0:1, cs])
            ig = jax.nn.sigmoid(z[:, LRU_BLOCK_W:] + bias_ref[1:2, cs])
            log_a = (-LRU_C * r) * sp[:, cs]
            abuf[r0:r0 + RL, cs] = jnp.exp(log_a)
            th = jnp.tanh(log_a)
            bbuf[r0:r0 + RL, cs] = jnp.sqrt(-2.0 * th / (1.0 - th)) * (ig * xs)

        return [functools.partial(block, n) for n in range(LRU_BLOCKS)]

    def s5_drive_parts(s):
        cs = slice(s * 128, (s + 1) * 128)

        def part(rb):
            rs = slice(rb * RB, (rb + 1) * RB)
            buf[s % 2, rs, :] = jnp.dot(u_ref[rs, cs].astype(BF16), wd_ref[s], preferred_element_type=F32)

        return [functools.partial(part, rb) for rb in range(R // RB)]

    def s5_scan(s):
        sb = buf.at[s % 2]
        ar = jnp.broadcast_to(a_ref[s, 0:1, :], (8, NS))
        ai = jnp.broadcast_to(a_ref[s, 1:2, :], (8, NS))
        init = []
        for hh in range(nh):
            init += [st[s, hh * 8:(hh + 1) * 8, 0:NS], st[s, hh * 8:(hh + 1) * 8, NS:2 * NS]]

        def body(k, carry):
            t = (tc - 1 - k) if reverse else k
            out = []
            for hh in range(nh):
                sr, si = carry[2 * hh], carry[2 * hh + 1]
                row = pl.multiple_of(t * nb + hh * 8, 8)
                br = sb[pl.ds(row, 8), 0:NS]
                bi = sb[pl.ds(row, 8), NS:2 * NS]
                nsr = ar * sr - ai * si + br
                nsi = ar * si + ai * sr + bi
                sb[pl.ds(row, 8), 0:NS] = nsr
                sb[pl.ds(row, 8), NS:2 * NS] = nsi
                out += [nsr, nsi]
            return tuple(out)

        fin = lax.fori_loop(0, tc, body, tuple(init), unroll=2)
        for hh in range(nh):
            st[s, hh * 8:(hh + 1) * 8, 0:NS] = fin[2 * hh]
            st[s, hh * 8:(hh + 1) * 8, NS:2 * NS] = fin[2 * hh + 1]

    def s5_readout_parts(s):
        cs = slice(s * 128, (s + 1) * 128)

        def part(rb):
            rs = slice(rb * RB, (rb + 1) * RB)
            y = jnp.dot(buf[s % 2, rs, :].astype(BF16), wc_ref[s], preferred_element_type=F32)
            if final:
                ybuf[rs, cs] = y
            else:
                oa_ref[rs, cs] = y

        return [functools.partial(part, rb) for rb in range(R // RB)]

    def emit_interleaved(mxu_parts, vpu_parts):
        n = max(len(mxu_parts), len(vpu_parts))
        for k in range(n):
            if k < len(mxu_parts):
                mxu_parts[k]()
            if k < len(vpu_parts):
                vpu_parts[k]()

    n_gate_blocks = R // RL
    for s in range(S5_STRIPS + 1):
        mxu = (s5_readout_parts(s - 1) if s > 0 else []) + (s5_drive_parts(s) if s < S5_STRIPS else [])
        vpu = []
        for rb in range(n_gate_blocks):
            if rb % S5_STRIPS == s:
                vpu += lru_gate_parts(rb)
        emit_interleaved(mxu, vpu)
        if s < S5_STRIPS:
            s5_scan(s)

    init = tuple(lst[hh * 8:(hh + 1) * 8, :] for hh in range(nh))

    def lru_body(k, carry):
        t = (tc - 1 - k) if reverse else k
        out = []
        for hh in range(nh):
            row = pl.multiple_of(t * nb + hh * 8, 8)
            hnew = abuf[pl.ds(row, 8), :] * carry[hh] + bbuf[pl.ds(row, 8), :]
            bbuf[pl.ds(row, 8), :] = hnew
            out.append(hnew)
        return tuple(out)

    fin = lax.fori_loop(0, tc, lru_body, init, unroll=4)
    for hh in range(nh):
        lst[hh * 8:(hh + 1) * 8, :] = fin[hh]

    if not final:
        or_ref[...] = bbuf[...]
        return
    for rb in range(R // RB):
        rs = slice(rb * RB, (rb + 1) * RB)
        y = ybuf[rs, :] + yp_ref[rs, :] + dsk_ref[...] * u_ref[rs, :]
        g = jax.nn.gelu(y)
        z = jnp.dot(g.astype(BF16), gw_ref[...], preferred_element_type=F32) + gb_ref[...]
        res = g * jax.nn.sigmoid(z)
        for tt in range(RB // nb):
            o3[:, rb * (RB // nb) + tt, :] = res[tt * nb:(tt + 1) * nb, :]
    oa_ref[...] = o3[...].astype(BF16)
    for t in range(tc):
        rows = slice(t * nb, (t + 1) * nb)
        o3[:, t, :] = hp_ref[rows, :] + bbuf[rows, :]
    or_ref[...] = (o3[...] * jax.nn.gelu(g_ref[...])).astype(BF16)


def _chunk_order(nc, nlc, reverse):
    if reverse:
        return lambda i: nc - 1 - i
    return lambda i: lax.rem(i + nlc, nc)


def _recurrent(u, lx, lg, s5p, d_skip, glu_w, glu_b, conv_w, conv_b, wab, bias, lam, n_lat):
    nb, lt, _ = u.shape
    rows = nb * lt
    tc = S5_CHUNK
    R = tc * nb
    nc, nlc = lt // tc, n_lat // tc
    halo = 8
    wd, a, wc = s5p
    fwd = None
    for d in range(2):
        final = d == 1
        cidx = _chunk_order(nc, nlc, reverse=final)
        row_spec = pl.BlockSpec((R, GROUP_W), lambda i: (cidx(i), 0))
        bt_spec = pl.BlockSpec((nb, tc, GROUP_W), lambda i: (0, cidx(i), 0))
        prev_spec = pl.BlockSpec((nb, halo, GROUP_W),
                                 lambda i: (0, jnp.maximum(cidx(i) * (tc // halo) - 1, 0), 0))
        next_spec = pl.BlockSpec((nb, halo, GROUP_W),
                                 lambda i: (0, jnp.minimum((cidx(i) + 1) * (tc // halo), lt // halo - 1), 0))
        full = lambda shape: pl.BlockSpec(shape, lambda i: (0,) * len(shape))
        s5_w = [full((S5_STRIPS, 128, 2 * S5_STRIP_STATES)), full((S5_STRIPS, 2, S5_STRIP_STATES)),
                full((S5_STRIPS, 2 * S5_STRIP_STATES, 128))]
        lru_w = [full((LRU_CONV, GROUP_W)), full((1, GROUP_W)), full((LRU_BLOCKS, LRU_BLOCK_W, 2 * LRU_BLOCK_W)),
                 full((2, GROUP_W)), full((1, GROUP_W))]
        lru_args = [conv_w, conv_b.reshape(1, GROUP_W), wab[d], bias[d], lam[d].reshape(1, GROUP_W)]
        if final:
            in_specs = ([bt_spec, row_spec] + s5_w + [full((1, GROUP_W)), full((GROUP_W, GROUP_W)), full((1, GROUP_W))]
                        + [bt_spec, prev_spec, next_spec, row_spec, bt_spec] + lru_w)
            args = ([u, fwd[0], wd[d], a[d], wc[d], d_skip.reshape(1, GROUP_W), glu_w, glu_b.reshape(1, GROUP_W)]
                    + [lx, lx, lx, fwd[1], lg] + lru_args)
            out_specs = [bt_spec, bt_spec]
            out_shape = [jax.ShapeDtypeStruct((nb, lt, GROUP_W), BF16)] * 2
        else:
            in_specs = [bt_spec] + s5_w + [bt_spec, prev_spec, next_spec] + lru_w
            args = [u, wd[d], a[d], wc[d], lx, lx, lx] + lru_args
            out_specs = [row_spec, row_spec]
            out_shape = [jax.ShapeDtypeStruct((rows, GROUP_W), F32)] * 2
        scratch = [pltpu.VMEM((R, GROUP_W), F32), pltpu.VMEM((2, R, 2 * S5_STRIP_STATES), F32),
                   pltpu.VMEM((S5_STRIPS, nb, 2 * S5_STRIP_STATES), F32),
                   pltpu.VMEM((R + 3 * nb, GROUP_W), F32), pltpu.VMEM((R, GROUP_W), F32),
                   pltpu.VMEM((R, GROUP_W), F32), pltpu.VMEM((nb, GROUP_W), F32)]
        if final:
            scratch += [pltpu.VMEM((R, GROUP_W), F32), pltpu.VMEM((nb, tc, GROUP_W), F32)]
        fwd = pl.pallas_call(
            functools.partial(_rec_kernel, tc=tc, nb=nb, nlc=nlc, nc=nc, reverse=final, final=final),
            grid=(nc,), in_specs=in_specs, out_specs=out_specs, out_shape=out_shape,
            scratch_shapes=scratch,
            compiler_params=_cparams(("arbitrary",), 56), name="rec_rev" if final else "rec_fwd",
        )(*args)
    return fwd


def _s5_params(lam_re, lam_im, log_dt, b_re, b_im, c_re, c_im):
    lr = jnp.minimum(lam_re.astype(F32), S5_MIN_NEG)
    li = lam_im.astype(F32)
    dt = jnp.exp(log_dt.astype(F32))[..., None]
    mag = jnp.exp(lr * dt)
    ar, ai = mag * jnp.cos(li * dt), mag * jnp.sin(li * dt)
    den = lr * lr + li * li
    fr = ((ar - 1.0) * lr + ai * li) / den
    fi = (ai * lr - (ar - 1.0) * li) / den
    bbr = fr[..., None] * b_re - fi[..., None] * b_im
    bbi = fr[..., None] * b_im + fi[..., None] * b_re
    gs = 128 // S5_CH
    eye = jnp.eye(gs, dtype=F32)

    def drive(bb):
        t = bb.reshape(2, S5_STRIPS, gs, S5_STATE, S5_CH)
        return jnp.einsum('dsgpc,gh->dsgchp', t, eye).reshape(2, S5_STRIPS, 128, S5_STRIP_STATES)

    def read(cc):
        t = cc.reshape(2, S5_STRIPS, gs, S5_CH, S5_STATE)
        return jnp.einsum('dsgcp,gh->dsgphc', t, eye).reshape(2, S5_STRIPS, S5_STRIP_STATES, 128)

    wd = jnp.concatenate([drive(bbr), drive(bbi)], axis=-1).astype(BF16)
    wc = jnp.concatenate([read(c_re.astype(F32)), read(-c_im.astype(F32))], axis=-2).astype(BF16)
    a = jnp.stack([ar.reshape(2, S5_STRIPS, S5_STRIP_STATES), ai.reshape(2, S5_STRIPS, S5_STRIP_STATES)], axis=2)
    return wd, a, wc


def _mlstm_kernel(q_ref, k_ref, vt_ref, o_ref, gr_ref, gc_ref, br_ref, bc_ref, nw_ref, out_ref,
                  xb_scr, row_scr, nt_scr, u_scr, vec_scr, *, nch, nlat, hp):
    T = ML_CHUNK
    nt_dims = (((1,), (1,)), ((), ()))
    rid = lax.broadcasted_iota(jnp.int32, (T, T), 0)
    cid = lax.broadcasted_iota(jnp.int32, (T, T), 1)
    incl_rows = (rid <= cid).astype(F32)
    incl_cols = (rid >= cid).astype(F32)

    for j in range(hp):
        g = (gr_ref[j] + br_ref[j]).reshape(nch * 8, T)
        kind = lax.broadcasted_iota(jnp.int32, (nch * 8, T), 0) % 8
        lf = _log_sigmoid(g)
        pre = jnp.dot(lf, incl_rows, precision=HIGHEST, preferred_element_type=F32)
        b = jnp.where(kind == 1, pre[:, T - 1:T] - pre + lf, pre)
        x = g - pltpu.roll(b, 2, axis=0)
        row_scr[j] = jnp.where(kind < 2, b, x).reshape(nch, 8, T)
        gl = gc_ref[j] + bc_ref[j]
        kind = lax.broadcasted_iota(jnp.int32, (T, 128), 1) % 4
        lf = _log_sigmoid(gl)
        pre = jnp.dot(incl_cols, lf, precision=HIGHEST, preferred_element_type=F32)
        b = jnp.where(kind == 1, pre[T - 1:T, :] - pre + lf, pre)
        x = gl - pltpu.roll(b, 2, axis=1)
        for c in range(nch):
            for d in range(2):
                lane = c * 4 + 2 + d
                xb_scr[j, d, c] = jnp.broadcast_to(x[:, lane:lane + 1], (T, T))

    def independent(c, carry):
        rows = pl.ds(pl.multiple_of(c * T, T), T)
        pairs = [(j, d) for j in range(hp) for d in range(2)]
        hs = lambda j: slice(j * ML_DH, (j + 1) * ML_DH)
        swept = lambda d: (rid <= cid) if d == 0 else (rid >= cid)
        kq = {j: lax.dot_general(k_ref[rows, hs(j)], q_ref[rows, hs(j)], nt_dims, preferred_element_type=F32)
              for j in range(hp)}
        xm = {p: jnp.where(swept(p[1]), xb_scr[p[0], p[1], c], -jnp.inf) for p in pairs}
        a_row = {p: jnp.max(xm[p], axis=0, keepdims=True) for p in pairs}
        w0 = {}
        for j, d in pairs:
            a_last = a_row[j, d][:, T - 1:T] if d == 0 else a_row[j, d][:, 0:1]
            w0[j, d] = jnp.exp(row_scr[j, c, 2 + d:3 + d, :] - a_last)
        s0 = {p: kq[p[0]] * jnp.exp(xm[p] - a_row[p]) for p in pairs}
        for j, d in pairs:
            vt, kc = vt_ref[c, hs(j), :], k_ref[rows, hs(j)]
            nt_scr[j, d, c] = jnp.dot(vt, s0[j, d].astype(BF16), preferred_element_type=F32)
            wv = (vt.astype(F32) * w0[j, d]).astype(BF16)
            u_scr[j, d, c] = jnp.dot(wv, kc, preferred_element_type=F32)
            vec_scr[j, d, c, 0:1, :] = a_row[j, d]
            vec_scr[j, d, c, 1:2, :] = jnp.sum(s0[j, d], axis=0, keepdims=True)
            vec_scr[j, d, c, 2:3, :] = jnp.dot(jnp.broadcast_to(w0[j, d], (8, T)).astype(BF16), kc,
                                               preferred_element_type=F32)[0:1, :]
        return carry

    lax.fori_loop(0, nch, independent, 0)

    def pass2(c, d, j, state):
        cm, n, m = state
        hs = slice(j * ML_DH, (j + 1) * ML_DH)
        qc = q_ref[pl.ds(pl.multiple_of(c * T, T), T), hs]
        a_row, d0, n0 = vec_scr[j, d, c, 0:1, :], vec_scr[j, d, c, 1:2, :], vec_scr[j, d, c, 2:3, :]
        b_row = row_scr[j, c, d:d + 1, :]
        mu = jnp.maximum(m, a_row)
        inter, r = jnp.exp(m - mu), jnp.exp(a_row - mu)
        qct = lax.dot_general(cm.astype(BF16), qc, nt_dims, preferred_element_type=F32)
        qn = lax.dot_general(jnp.broadcast_to(n, (8, ML_DH)).astype(BF16), qc, nt_dims,
                             preferred_element_type=F32)[0:1, :]
        den = inter * qn + r * d0
        inv = 1.0 / jnp.maximum(jnp.abs(den), jnp.exp(-b_row - mu))
        nt_scr[j, d, c] = (inter * qct + r * nt_scr[j, d, c]) * inv
        a_last = a_row[:, T - 1:T] if d == 0 else a_row[:, 0:1]
        b_last = b_row[:, T - 1:T] if d == 0 else b_row[:, 0:1]
        mul = jnp.maximum(m, a_last)
        decay, rl = jnp.exp(m - mul), jnp.exp(a_last - mul)
        return decay * cm + rl * u_scr[j, d, c], decay * n + rl * n0, b_last + mul

    def states(i, carry):
        out = []
        for j in range(hp):
            out.append(pass2(lax.rem(i + nlat, nch), 0, j, carry[2 * j]))
            out.append(pass2(nch - 1 - i, 1, j, carry[2 * j + 1]))
        return tuple(out)

    zero = (jnp.zeros((ML_DH, ML_DH), F32), jnp.zeros((1, ML_DH), F32), jnp.zeros((1, 1), F32))
    lax.fori_loop(0, nch, states, (zero,) * (2 * hp))

    for c in range(nch):
        rows = slice(c * T, (c + 1) * T)
        for j in range(hp):
            hs = slice(j * ML_DH, (j + 1) * ML_DH)
            ht = nt_scr[j, 0, c] + nt_scr[j, 1, c]
            hn = ht * lax.rsqrt(jnp.mean(ht * ht, axis=0, keepdims=True) + EPS) * nw_ref[j]
            out_ref[rows, hs] = (hn.T * jax.nn.sigmoid(o_ref[rows, hs])).astype(BF16)


def _mlstm(mq, mk, mvt, mo, misc, mgt, ig_bias, fg_bias, out_norm, n_lat):
    B, Lt, _ = mq.shape
    T = ML_CHUNK
    nch, nlat = Lt // T, n_lat // T
    hp = ML_HEADS_PER_STEP
    assert 4 * nch <= 128
    gr = jnp.transpose(mgt.reshape(B, ML_HEADS, 4, nch, T), (0, 1, 3, 2, 4))
    gr = jnp.pad(gr, ((0, 0), (0, 0), (0, 0), (0, 4), (0, 0)))
    gc = misc[:, :, MLA_ROPE:MLA_ROPE + 16].reshape(B, nch, T, ML_HEADS, 4)
    gc = jnp.transpose(gc, (0, 3, 2, 1, 4)).reshape(B, ML_HEADS, T, nch * 4)
    gc = jnp.pad(gc, ((0, 0), (0, 0), (0, 0), (0, 128 - nch * 4)))
    kinds = jnp.concatenate([fg_bias.astype(F32), ig_bias.astype(F32)], axis=0).T
    br = jnp.broadcast_to(jnp.pad(kinds, ((0, 0), (0, 4)))[:, :, None], (ML_HEADS, 8, T))
    bc = jnp.pad(jnp.tile(kinds, (1, nch)), ((0, 0), (0, 128 - nch * 4)))[:, None, :]
    nwb = jnp.broadcast_to(out_norm.astype(F32)[:, :, None], (ML_HEADS, ML_DH, T))
    head_spec = pl.BlockSpec((None, Lt, hp * ML_DH), lambda b, h: (b, 0, h))
    slab = lambda: pltpu.VMEM((hp, 2, nch, T, T), F32)
    return pl.pallas_call(
        functools.partial(_mlstm_kernel, nch=nch, nlat=nlat, hp=hp),
        grid=(B, ML_HEADS // hp),
        in_specs=[head_spec, head_spec,
                  pl.BlockSpec((None, nch, hp * ML_DH, T), lambda b, h: (b, 0, h, 0)),
                  head_spec,
                  pl.BlockSpec((None, hp, nch, 8, T), lambda b, h: (b, h, 0, 0, 0)),
                  pl.BlockSpec((None, hp, T, 128), lambda b, h: (b, h, 0, 0)),
                  pl.BlockSpec((hp, 8, T), lambda b, h: (h, 0, 0)),
                  pl.BlockSpec((hp, 1, 128), lambda b, h: (h, 0, 0)),
                  pl.BlockSpec((hp, ML_DH, T), lambda b, h: (h, 0, 0))],
        out_specs=head_spec,
        out_shape=jax.ShapeDtypeStruct((B, Lt, GROUP_W), BF16),
        scratch_shapes=[slab(), pltpu.VMEM((hp, nch, 8, T), F32), slab(), slab(),
                        pltpu.VMEM((hp, 2, nch, 8, T), F32)],
        compiler_params=_cparams(("parallel", "parallel"), 60), name="mlstm",
    )(mq, mk, mvt, mo, gr, gc, br, bc, nwb)


def _mla_prep_parts(cq, ckv, misc, cos_ref, sin_ref, qan_ref, wq_ref, kvan_ref, wkv_ref, qn_ref, kn_ref,
                    qp_ref, kp_ref, v_ref):
    def rms(x, w):
        return (x * lax.rsqrt(jnp.mean(x * x, axis=-1, keepdims=True) + EPS)) * w

    tm = cq.shape[0]
    lane = lax.broadcasted_iota(jnp.int32, (tm, 128), 1)
    first = (lane % (MLA_ROPE // 2)) < (MLA_ROPE // 4)
    hw = MLA_HEADS * MLA_NOPE
    vals = {}

    def rope(x):
        sw = jnp.where(first, pltpu.roll(x, 128 - MLA_ROPE // 4, axis=1), pltpu.roll(x, MLA_ROPE // 4, axis=1))
        return x * cos_ref[...] + sw * sin_ref[...]

    def project():
        vals["q"] = jnp.dot(rms(cq, qan_ref[...]).astype(BF16), wq_ref[...], preferred_element_type=F32)
        kv = jnp.dot(rms(ckv, kvan_ref[...]).astype(BF16), wkv_ref[...], preferred_element_type=F32)
        vals["k"] = kv[:, :hw]
        v_ref[...] = kv[:, hw:].astype(BF16)
        kr = jnp.where(lane < MLA_ROPE, misc, 0.0)
        vals["kr_sq"] = jnp.sum(kr * kr, axis=-1, keepdims=True)
        vals["kr_rot"] = rope(kr * kn_ref[:, 128:256])

    def head(h):
        base = h * MLA_PAD
        qa, qb = vals["q"][:, base:base + 128], vals["q"][:, base + 128:base + 256]
        inv = lax.rsqrt(jnp.sum(qa * qa + qb * qb, axis=-1, keepdims=True) / MLA_QK + EPS)
        qp_ref[:, base:base + 128] = ((qa * inv) * (qn_ref[:, 0:128] * ATTN_SCALE)).astype(BF16)
        qp_ref[:, base + 128:base + 256] = rope((qb * inv) * (qn_ref[:, 128:256] * ATTN_SCALE)).astype(BF16)
        ka = vals["k"][:, h * MLA_NOPE:(h + 1) * MLA_NOPE]
        inv = lax.rsqrt((jnp.sum(ka * ka, axis=-1, keepdims=True) + vals["kr_sq"]) / MLA_QK + EPS)
        kp_ref[:, base:base + 128] = ((ka * inv) * kn_ref[:, 0:128]).astype(BF16)
        kp_ref[:, base + 128:base + 256] = (vals["kr_rot"] * inv).astype(BF16)

    return [project] + [functools.partial(head, h) for h in range(MLA_HEADS)]


def _attn_kernel(q_ref, k_ref, v_ref, o_ref, *, n_lat, n_lat_tiles, sub):
    def attend(n_sub, k, v):
        for first in range(0, n_sub, 2):
            blocks = [slice(i * sub, (i + 1) * sub) for i in range(first, min(first + 2, n_sub))]
            s = [lax.dot_general(q_ref[r, :], k, (((1,), (1,)), ((), ())), preferred_element_type=F32)
                 for r in blocks]
            p = [jnp.exp(si - jnp.max(si, axis=-1, keepdims=True)) for si in s]
            for r, pi in zip(blocks, p):
                den = jnp.sum(pi, axis=-1, keepdims=True)
                o_ref[r, :] = (jnp.dot(pi.astype(BF16), v, preferred_element_type=F32) / den).astype(BF16)

    qi = pl.program_id(2)

    @pl.when(qi < n_lat_tiles)
    def _():
        attend(q_ref.shape[0] // sub, k_ref[...], v_ref[...])

    @pl.when(qi >= n_lat_tiles)
    def _():
        attend((k_ref.shape[0] - n_lat) // sub, k_ref[n_lat:, :], v_ref[n_lat:, :])


def _rope_tables(n_lat, n_ctx):
    n_rows = n_lat // GRID_W
    rows = jnp.repeat(jnp.arange(n_rows, dtype=F32), GRID_W)
    cols = jnp.tile(jnp.arange(GRID_W, dtype=F32), n_rows)
    n_freq = MLA_ROPE // 4
    inv_freq = ROPE_BASE ** (-jnp.arange(n_freq, dtype=F32) / n_freq)
    ar, ac = rows[:, None] * inv_freq, cols[:, None] * inv_freq
    cos = jnp.concatenate([jnp.cos(ar), jnp.cos(ar), jnp.cos(ac), jnp.cos(ac)], axis=1)
    sins = jnp.concatenate([-jnp.sin(ar), jnp.sin(ar), -jnp.sin(ac), jnp.sin(ac)], axis=1)
    cos = jnp.pad(cos, ((0, n_ctx), (0, 128 - MLA_ROPE)), constant_values=1.0)
    sins = jnp.pad(sins, ((0, n_ctx), (0, 128 - MLA_ROPE)))
    return cos, sins


def _mla_attn(qp, kp, v, n_lat):
    B, Lt, _ = v.shape
    tq = min(Q_TILE, n_lat)
    return pl.pallas_call(
        functools.partial(_attn_kernel, n_lat=n_lat, n_lat_tiles=n_lat // tq, sub=Q_SUB),
        grid=(B, MLA_HEADS, pl.cdiv(Lt, tq)),
        in_specs=[pl.BlockSpec((None, tq, MLA_PAD), lambda b, h, t: (b, t, h)),
                  pl.BlockSpec((None, Lt, MLA_PAD), lambda b, h, t: (b, 0, h)),
                  pl.BlockSpec((None, Lt, MLA_V), lambda b, h, t: (b, 0, h))],
        out_specs=pl.BlockSpec((None, tq, MLA_V), lambda b, h, t: (b, t, h)),
        out_shape=jax.ShapeDtypeStruct((B, Lt, GROUP_W), BF16),
        compiler_params=_cparams(("parallel", "parallel", "arbitrary"), 40), name="mla_attn",
    )(qp, kp, v)


def _wout_kernel(*refs, n_src, n_lat_tiles):
    x_refs, (mod_ref, a_ref, b_ref, m_ref, r_ref, w_ref, o_ref) = refs[:n_src], refs[n_src:]
    acc = jnp.dot(a_ref[...], w_ref[0:GROUP_W, :], preferred_element_type=F32)
    acc += jnp.dot(b_ref[...], w_ref[GROUP_W:2 * GROUP_W, :], preferred_element_type=F32)
    acc += jnp.dot(m_ref[...], w_ref[2 * GROUP_W:3 * GROUP_W, :], preferred_element_type=F32)
    acc += jnp.dot(r_ref[...], w_ref[3 * GROUP_W:4 * GROUP_W, :], preferred_element_type=F32)
    o_ref[...] = _tile_rows(x_refs, n_lat_tiles) + mod_ref[2:3, :] * acc


def _wout(src, mod_l, a, b, m, r, w, layer, n_lat_tiles):
    B, _, D = src[0].shape
    Lt = sum(s.shape[1] for s in src)
    tm = TOKEN_TILE
    bm = lambda b, t: (b, t, 0)
    mix = pl.BlockSpec((None, tm, GROUP_W), bm)
    return pl.pallas_call(
        functools.partial(_wout_kernel, n_src=len(src), n_lat_tiles=n_lat_tiles), grid=(B, Lt // tm),
        in_specs=_residual_specs(src, tm, n_lat_tiles) + [
            pl.BlockSpec((None, 6, D), lambda b, t: (jnp.where(t >= n_lat_tiles, B, b), 0, 0)),
            mix, mix, mix, mix,
            pl.BlockSpec((None, 4 * GROUP_W, D), lambda b, t: (layer, 0, 0))],
        out_specs=pl.BlockSpec((None, tm, D), bm),
        out_shape=jax.ShapeDtypeStruct((B, Lt, D), F32),
        compiler_params=_cparams(("parallel", "parallel"), 48), name="out_proj",
    )(*src, mod_l, a, b, m, r, w)


def _mlp_kernel(x_ref, xc_ref, mod_ref, modc_ref, nw_ref, w1a_ref, w1b_ref, w2a_ref, w2b_ref, *rest,
                nf, rows, saxis):
    o_ref, h_ref, a_ref = rest[-3:]
    s = pl.program_id(saxis)
    hw = w1a_ref.shape[-1]

    def up(h, f, r):
        for k, w_ref in enumerate((w1a_ref, w1b_ref)):
            a = jnp.maximum(jnp.dot(h, w_ref[...], preferred_element_type=F32), 0.0)
            a_ref[f, r, k * hw:(k + 1) * hw] = (a * a).astype(BF16)

    @pl.when(s == 0)
    def _():
        x = x_ref[...].reshape(rows, x_ref.shape[-1])
        half = rows // 2
        for r in (slice(0, half), slice(half, rows)):
            h = _norm_mod(x[r], nw_ref[...], mod_ref[3:4, :], mod_ref[4:5, :])
            h_ref[r, :] = h
            up(h, 0, r)

    @pl.when(jnp.logical_and(s > 0, s < nf))
    def _():
        up(h_ref[...], s, slice(None))

    @pl.when(s >= nf)
    def _():
        parts = [(f, w2a_ref, f) for f in range(nf // 2)] + [(nf // 2 + f, w2b_ref, f) for f in range(nf // 2)]
        y = None
        for f, w_ref, k in parts:
            d = jnp.dot(a_ref[f], w_ref[k], preferred_element_type=F32)
            y = d if y is None else y + d
        o_ref[...] = xc_ref[...] + (modc_ref[5:6, :] * y).reshape(o_ref.shape)


def _mlp(X, mod_l, nw, w1, w2, layer, n_lat, with_ctx):
    B, Lt, D = X.shape
    n_ctx = Lt - n_lat
    ff = w1.shape[2]
    tm, tf, tn = MLP_ROWS, MLP_FF_TILE, MLP_OUT_TILE
    nf, nn = ff // tf, D // tn
    ns = nf + nn
    w2r = w2.reshape(w2.shape[0], nf, tf, D)
    nwr = nw.reshape(1, D)
    col = lambda s: jnp.maximum(s - nf, 0)
    scratch = [pltpu.VMEM((tm, D), BF16), pltpu.VMEM((nf, tm, tf), BF16)]
    lat = pl.pallas_call(
        functools.partial(_mlp_kernel, nf=nf, rows=tm, saxis=2), grid=(B, n_lat // tm, ns),
        in_specs=[pl.BlockSpec((None, tm, D), lambda b, t, s: (b, t, 0)),
                  pl.BlockSpec((None, tm, tn), lambda b, t, s: (b, t, col(s))),
                  pl.BlockSpec((None, 6, D), lambda b, t, s: (b, 0, 0)),
                  pl.BlockSpec((None, 6, tn), lambda b, t, s: (b, 0, col(s))),
                  pl.BlockSpec((1, D), lambda b, t, s: (0, 0)),
                  pl.BlockSpec((None, D, tf // 2), lambda b, t, s: (layer, 0, 2 * jnp.minimum(s, nf - 1))),
                  pl.BlockSpec((None, D, tf // 2), lambda b, t, s: (layer, 0, 2 * jnp.minimum(s, nf - 1) + 1)),
                  pl.BlockSpec((None, nf // 2, tf, tn), lambda b, t, s: (layer, 0, 0, col(s))),
                  pl.BlockSpec((None, nf // 2, tf, tn), lambda b, t, s: (layer, 1, 0, col(s)))],
        out_specs=pl.BlockSpec((None, tm, tn), lambda b, t, s: (b, t, col(s))),
        out_shape=jax.ShapeDtypeStruct((B, Lt if with_ctx else n_lat, D), F32),
        scratch_shapes=scratch,
        compiler_params=_cparams(("parallel", "parallel", "arbitrary"), 58), name="mlp_lat",
    )(X, X, mod_l, mod_l, nwr, w1, w1, w2r, w2r)
    if not with_ctx:
        return lat
    g = tm // n_ctx
    cblk = n_lat // n_ctx
    return pl.pallas_call(
        functools.partial(_mlp_kernel, nf=nf, rows=tm, saxis=1), grid=(B // g, ns),
        in_specs=[pl.BlockSpec((g, n_ctx, D), lambda i, s: (i, cblk, 0)),
                  pl.BlockSpec((g, n_ctx, tn), lambda i, s: (i, cblk, col(s))),
                  pl.BlockSpec((None, 6, D), lambda i, s: (B, 0, 0)),
                  pl.BlockSpec((None, 6, tn), lambda i, s: (B, 0, col(s))),
                  pl.BlockSpec((1, D), lambda i, s: (0, 0)),
                  pl.BlockSpec((None, D, tf // 2), lambda i, s: (layer, 0, 2 * jnp.minimum(s, nf - 1))),
                  pl.BlockSpec((None, D, tf // 2), lambda i, s: (layer, 0, 2 * jnp.minimum(s, nf - 1) + 1)),
                  pl.BlockSpec((None, nf // 2, tf, tn), lambda i, s: (layer, 0, 0, col(s))),
                  pl.BlockSpec((None, nf // 2, tf, tn), lambda i, s: (layer, 1, 0, col(s))),
                  pl.BlockSpec(memory_space=pl.ANY)],
        out_specs=pl.BlockSpec((g, n_ctx, tn), lambda i, s: (i, cblk, col(s))),
        out_shape=jax.ShapeDtypeStruct((B, Lt, D), F32),
        input_output_aliases={9: 0},
        scratch_shapes=scratch,
        compiler_params=_cparams(("parallel", "arbitrary"), 58), name="mlp_ctx",
    )(X, X, mod_l, mod_l, nwr, w1, w1, w2r, w2r, lat)


def _pack_w_in(w_in):
    offs = [0, 512, 1024, 1536, 2048, 2560, 2576, 2960, 3088, 3152, 3664, 4176]
    u, mq, mk, mv, mo, mg, cq, ckv, kr, lx, lg = [w_in[..., offs[i]:offs[i + 1]] for i in range(11)]
    depth, d = w_in.shape[:2]
    mg = jnp.transpose(mg.reshape(depth, d, 2, 2, ML_HEADS), (0, 1, 4, 3, 2))[:, :, :, ::-1, :].reshape(depth, d, 16)
    pad = jnp.zeros((depth, d, MISC_W - MLA_ROPE - 16), w_in.dtype)
    w_p = jnp.concatenate([u, mq, mk, mv, mo, cq, ckv, lx, lg, kr, mg, pad], axis=-1).astype(BF16)
    return w_p, jnp.transpose(mg, (0, 2, 1)).astype(BF16), jnp.transpose(mv, (0, 2, 1)).astype(BF16)


def _pack_mla(q_a_norm, w_q_up, kv_a_norm, w_kv_up, q_norm, k_norm):
    depth = w_q_up.shape[0]
    wq = w_q_up.reshape(depth, MLA_Q_LORA, MLA_HEADS, MLA_QK)
    wq = jnp.pad(wq, ((0, 0), (0, 0), (0, 0), (0, MLA_PAD - MLA_QK))).reshape(depth, MLA_Q_LORA, -1).astype(BF16)
    wkv = w_kv_up.reshape(depth, MLA_KV_LORA, MLA_HEADS, MLA_NOPE + MLA_V)
    wkv = jnp.concatenate([wkv[..., :MLA_NOPE].reshape(depth, MLA_KV_LORA, -1),
                           wkv[..., MLA_NOPE:].reshape(depth, MLA_KV_LORA, -1)], axis=-1).astype(BF16)
    row = lambda a: a.astype(F32)[:, None, :]
    head = lambda a: jnp.pad(a.astype(F32), ((0, 0), (0, MLA_PAD - MLA_QK)))[:, None, :]
    return row(q_a_norm), wq, row(kv_a_norm), wkv, head(q_norm), head(k_norm)


def _pack_lru(wa, ba, wx, bx):
    wab = jnp.concatenate([wa, wx], axis=-1).astype(BF16)
    bias = jnp.stack([ba, bx], axis=2).astype(F32)
    return wab, bias


def kernel(x, c, ctx, c_ctx, ada_w, ada_b, norm1_w, norm2_w, w_in, w_out, s5_lam_re, s5_lam_im, s5_log_dt, s5_b_re, s5_b_im, s5_c_re, s5_c_im, s5_d, s5_glu_w, s5_glu_b, ml_ig_bias, ml_fg_bias, ml_out_norm, mla_q_a_norm, mla_w_q_up, mla_kv_a_norm, mla_w_kv_up, mla_q_norm, mla_k_norm, lru_conv_w, lru_conv_b, lru_wa, lru_ba, lru_wx, lru_bx, lru_lam, mlp_w1, mlp_w2):
    B, L, D = x.shape
    Lc = ctx.shape[1]
    depth = ada_w.shape[0]
    n_lat_tiles = L // TOKEN_TILE
    src = (x.astype(F32), ctx.astype(F32))
    mod = _ada_all(c, c_ctx, ada_w, ada_b)
    w_p, wgt, wvt = _pack_w_in(w_in)
    mla_w = _pack_mla(mla_q_a_norm, mla_w_q_up, mla_kv_a_norm, mla_w_kv_up, mla_q_norm, mla_k_norm)
    wab, lru_bias = _pack_lru(lru_wa, lru_ba, lru_wx, lru_bx)
    tables = _rope_tables(L, Lc)
    w_out_b, glu_w_b = w_out.astype(BF16), s5_glu_w.astype(BF16)
    w1_b, w2_b = mlp_w1.astype(BF16), mlp_w2.astype(BF16)
    for l in range(depth):
        u, mq, mk, mv, mo, lx, lg, misc, mgt, qp, kp, v = _win(src, mod[l], norm1_w[l], w_p, wgt, wvt, mla_w, tables,
                                                               l, n_lat_tiles)
        s5p = _s5_params(s5_lam_re[l], s5_lam_im[l], s5_log_dt[l], s5_b_re[l], s5_b_im[l], s5_c_re[l], s5_c_im[l])
        a_mix, r_mix = _recurrent(u, lx, lg, s5p, s5_d[l], glu_w_b[l], s5_glu_b[l], lru_conv_w[l], lru_conv_b[l],
                                  wab[l], lru_bias[l], lru_lam[l], L)
        b_mix = _mlstm(mq, mk, mv, mo, misc, mgt, ml_ig_bias[l], ml_fg_bias[l], ml_out_norm[l], L)
        m_mix = _mla_attn(qp, kp, v, L)
        X = _wout(src, mod[l], a_mix, b_mix, m_mix, r_mix, w_out_b, l, n_lat_tiles)
        X = _mlp(X, mod[l], norm2_w[l], w1_b, w2_b, l, L, with_ctx=l < depth - 1)
        src = (X,)
    return X
```

```python
import functools

import jax
import jax.numpy as jnp
from jax import lax
from jax.experimental import pallas as pl
from jax.experimental.pallas import tpu as pltpu

F32 = jnp.float32
BF16 = jnp.bfloat16
EPS = 1e-6
HIGHEST = lax.Precision.HIGHEST

GROUP_W = 512
TOKEN_TILE = 256
ROW_BATCHES = 2
S5_STATE = 64
S5_CH = 16
S5_STRIPS = 4
S5_STRIP_STATES = 512
S5_MIN_NEG = -1e-4
S5_CHUNK = 64
ML_HEADS = 4
ML_DH = 128
ML_CHUNK = 128
ML_HEADS_PER_STEP = 4
MLA_HEADS = 4
MLA_NOPE = 128
MLA_ROPE = 64
MLA_QK = MLA_NOPE + MLA_ROPE
MLA_V = 128
MLA_Q_LORA = 384
MLA_KV_LORA = 128
MLA_PAD = 256
ATTN_SCALE = MLA_QK ** -0.5
ROPE_BASE = 10000.0
GRID_W = 64
Q_TILE = 1024
Q_SUB = 256
LRU_BLOCKS = 4
LRU_BLOCK_W = 128
LRU_CONV = 4
LRU_C = 8.0
MLP_FF_TILE = 2048
MLP_OUT_TILE = 512
MLP_ROWS = 512

COL_U, COL_MQ, COL_MK, COL_MV, COL_MO = 0, 512, 1024, 1536, 2048
COL_CQ, COL_CKV, COL_LX, COL_LG, COL_MISC = 2560, 2944, 3072, 3584, 4096
IN_COLS_PACKED = 4224
MISC_W = 128


def _cparams(sem, vmem_mb):
    return pltpu.CompilerParams(dimension_semantics=sem, vmem_limit_bytes=vmem_mb << 20)


def _norm_mod(x, nw, shift, scale):
    ms = jnp.mean(x * x, axis=-1, keepdims=True)
    y = (x * lax.rsqrt(ms + EPS)) * nw
    return (y * (1.0 + scale) + shift).astype(BF16)


def _log_sigmoid(x):
    return jnp.minimum(x, 0.0) - jnp.log1p(jnp.exp(-jnp.abs(x)))


def _softplus(x):
    return jnp.maximum(x, 0.0) + jnp.log1p(jnp.exp(-jnp.abs(x)))


def _ada_kernel(c_ref, w_ref, b_ref, o_ref):
    c = c_ref[...]
    act = (c * jax.nn.sigmoid(c)).astype(BF16)
    o_ref[...] = jnp.dot(act, w_ref[...].astype(BF16), preferred_element_type=F32) + b_ref[...]


def _ada_all(c, c_ctx, ada_w, ada_b):
    depth, d, n6 = ada_w.shape
    nb = c.shape[0]
    rows = nb + 8
    cc = jnp.concatenate([c.astype(F32), jnp.broadcast_to(c_ctx.astype(F32)[None], (8, d))], axis=0)
    tn = 1024
    out = pl.pallas_call(
        _ada_kernel,
        grid=(depth, n6 // tn),
        in_specs=[pl.BlockSpec((rows, d), lambda l, n: (0, 0)),
                  pl.BlockSpec((None, d, tn), lambda l, n: (l, 0, n)),
                  pl.BlockSpec((None, 1, tn), lambda l, n: (l, 0, n))],
        out_specs=pl.BlockSpec((None, rows, tn), lambda l, n: (l, 0, n)),
        out_shape=jax.ShapeDtypeStruct((depth, rows, n6), F32),
        compiler_params=_cparams(("parallel", "parallel"), 40),
        name="ada_mod",
    )(cc, ada_w, ada_b.reshape(depth, 1, n6))
    return out.reshape(depth, rows, 6, d)


def _tile_rows(x_refs, n_lat_tiles):
    if len(x_refs) == 1:
        return x_refs[0][...]
    return jnp.where(pl.program_id(1) >= n_lat_tiles, x_refs[1][...], x_refs[0][...])


def _win_kernel(*refs, n_src, n_lat_tiles):
    x_refs, refs = refs[:n_src], refs[n_src:]
    (mod_ref, nw_ref, w_ref, wgt_ref, wvt_ref, cos_ref, sin_ref, qan_ref, wq_ref, kvan_ref, wkv_ref, qn_ref, kn_ref,
     u_ref, mq_ref, mk_ref, mvt_ref, mo_ref, lx_ref, lg_ref, misc_ref, mgt_ref, qp_ref, kp_ref, v_ref) = refs
    g, tm, d = x_refs[0].shape
    h = _norm_mod(_tile_rows(x_refs, n_lat_tiles), nw_ref[...], mod_ref[:, 0:1, :], mod_ref[:, 1:2, :])
    h = h.reshape(g * tm, d)
    nt_dims = (((1,), (1,)), ((), ()))

    def mm(lo, width):
        return jnp.dot(h, w_ref[:, lo:lo + width], preferred_element_type=F32)

    def store(ref, lo, scale=None):
        def run():
            y = mm(lo, ref.shape[-1])
            ref[...] = (y if scale is None else y * scale).astype(ref.dtype).reshape(ref.shape)
        return run

    def values_t():
        mvt = lax.dot_general(wvt_ref[...], h, nt_dims, preferred_element_type=F32).astype(BF16)
        for bi in range(g):
            for cc in range(mvt_ref.shape[1]):
                lo = bi * tm + cc * ML_CHUNK
                mvt_ref[bi, cc] = mvt[:, lo:lo + ML_CHUNK]

    def gates_t():
        mgt = lax.dot_general(wgt_ref[...], h, nt_dims, preferred_element_type=F32)
        for bi in range(g):
            mgt_ref[bi] = mgt[:, bi * tm:(bi + 1) * tm]

    misc = mm(COL_MISC, MISC_W)
    misc_ref[...] = misc.reshape(misc_ref.shape)
    prep = _mla_prep_parts(mm(COL_CQ, MLA_Q_LORA), mm(COL_CKV, MLA_KV_LORA), misc, cos_ref, sin_ref, qan_ref,
                           wq_ref, kvan_ref, wkv_ref, qn_ref, kn_ref, qp_ref, kp_ref, v_ref)
    proj = [store(u_ref, COL_U), store(mq_ref, COL_MQ), store(mk_ref, COL_MK, ML_DH ** -0.5), values_t,
            store(mo_ref, COL_MO), store(lx_ref, COL_LX), store(lg_ref, COL_LG), gates_t]
    for k in range(max(len(prep), len(proj))):
        if k < len(proj):
            proj[k]()
        if k < len(prep):
            prep[k]()


def _residual_specs(src, tm, n_lat_tiles, g=None):
    D = src[0].shape[-1]
    if len(src) == 1:
        return [pl.BlockSpec((g, tm, D), lambda b, t: (b, t, 0))]
    return [pl.BlockSpec((g, tm, D), lambda b, t: (b, jnp.minimum(t, n_lat_tiles - 1), 0)),
            pl.BlockSpec((g, tm, D), lambda b, t: (b, jnp.maximum(t - n_lat_tiles, 0), 0),
                         pipeline_mode=pl.Buffered(1))]


def _win(src, mod_l, nw, w_p, wgt, wvt, mla_w, tables, layer, n_lat_tiles):
    B, _, D = src[0].shape
    Lt = sum(a.shape[1] for a in src)
    tm, g = TOKEN_TILE, ROW_BATCHES
    nt = Lt // tm
    hp = MLA_HEADS * MLA_PAD
    bm = lambda b, t: (b, t, 0)
    per_layer = lambda *shape: pl.BlockSpec((None,) + shape, lambda b, t: (layer,) + (0,) * len(shape))
    table = pl.BlockSpec((tm, 128), lambda b, t: (t, 0))
    in_specs = _residual_specs(src, tm, n_lat_tiles, g) + [
        pl.BlockSpec((g, 6, D), lambda b, t: (jnp.where(t >= n_lat_tiles, B // g, b), 0, 0)),
        pl.BlockSpec((1, D), lambda b, t: (0, 0)),
        pl.BlockSpec((None, D, IN_COLS_PACKED), lambda b, t: (layer, 0, 0), pipeline_mode=pl.Buffered(1)),
        per_layer(16, D), per_layer(GROUP_W, D), table, table,
        per_layer(1, MLA_Q_LORA), per_layer(MLA_Q_LORA, hp),
        per_layer(1, MLA_KV_LORA), per_layer(MLA_KV_LORA, MLA_HEADS * (MLA_NOPE + MLA_V)),
        per_layer(1, MLA_PAD), per_layer(1, MLA_PAD),
    ]
    cpt = tm // ML_CHUNK
    out_specs = [
        pl.BlockSpec((g, tm, GROUP_W), bm),
        pl.BlockSpec((g, tm, GROUP_W), bm),
        pl.BlockSpec((g, tm, GROUP_W), bm),
        pl.BlockSpec((g, cpt, GROUP_W, ML_CHUNK), lambda b, t: (b, t, 0, 0)),
        pl.BlockSpec((g, tm, GROUP_W), bm),
        pl.BlockSpec((g, tm, GROUP_W), bm),
        pl.BlockSpec((g, tm, GROUP_W), bm),
        pl.BlockSpec((g, tm, MISC_W), bm),
        pl.BlockSpec((g, 16, tm), lambda b, t: (b, 0, t)),
        pl.BlockSpec((g, tm, hp), bm),
        pl.BlockSpec((g, tm, hp), bm),
        pl.BlockSpec((g, tm, GROUP_W), bm),
    ]
    sds = jax.ShapeDtypeStruct
    out_shape = [
        sds((B, Lt, GROUP_W), F32),
        sds((B, Lt, GROUP_W), BF16), sds((B, Lt, GROUP_W), BF16),
        sds((B, Lt // ML_CHUNK, GROUP_W, ML_CHUNK), BF16),
        sds((B, Lt, GROUP_W), F32),
        sds((B, Lt, GROUP_W), F32), sds((B, Lt, GROUP_W), F32),
        sds((B, Lt, MISC_W), F32),
        sds((B, 16, Lt), F32),
        sds((B, Lt, hp), BF16), sds((B, Lt, hp), BF16), sds((B, Lt, GROUP_W), BF16),
    ]
    return pl.pallas_call(
        functools.partial(_win_kernel, n_src=len(src), n_lat_tiles=n_lat_tiles),
        grid=(B // g, nt), in_specs=in_specs, out_specs=out_specs, out_shape=out_shape,
        compiler_params=_cparams(("parallel", "parallel"), 58), name="in_proj",
    )(*src, mod_l, nw.reshape(1, D), w_p, wgt, wvt, tables[0], tables[1], *mla_w)


def _rec_kernel(*refs, tc, nb, nlc, nc, reverse, final):
    if final:
        (ub_ref, yp_ref, wd_ref, a_ref, wc_ref, dsk_ref, gw_ref, gb_ref,
         x_ref, xp_ref, xn_ref, hp_ref, g_ref, cw_ref, cb_ref, w_ref, bias_ref, lam_ref,
         oa_ref, or_ref, u_ref, buf, st, xpad, abuf, bbuf, lst, ybuf, o3) = refs
    else:
        (ub_ref, wd_ref, a_ref, wc_ref,
         x_ref, xp_ref, xn_ref, cw_ref, cb_ref, w_ref, bias_ref, lam_ref,
         oa_ref, or_ref, u_ref, buf, st, xpad, abuf, bbuf, lst) = refs
    i = pl.program_id(0)
    c = (nc - 1 - i) if reverse else lax.rem(i + nlc, nc)
    R = tc * nb
    RB = min(R, 512)
    RL = min(R, 256)
    NS = S5_STRIP_STATES
    nh = nb // 8
    left = LRU_CONV // 2
    halo = xp_ref.shape[1]

    @pl.when(i == 0)
    def _():
        st[...] = jnp.zeros_like(st)
        lst[...] = jnp.zeros_like(lst)

    starts = jnp.logical_or(c == 0, c == nlc)
    ends = jnp.logical_or(c == nlc - 1, c == nc - 1)
    for j in range(left):
        xpad[j * nb:(j + 1) * nb, :] = jnp.where(starts, 0.0, xp_ref[:, halo - left + j, :])
    for t in range(tc):
        u_ref[t * nb:(t + 1) * nb, :] = ub_ref[:, t, :]
        xpad[(left + t) * nb:(left + t + 1) * nb, :] = x_ref[:, t, :]
    for j in range(LRU_CONV - 1 - left):
        xpad[(left + tc + j) * nb:(left + tc + j + 1) * nb, :] = jnp.where(ends, 0.0, xn_ref[:, j, :])
    sp = _softplus(-lam_ref[...])

    def lru_gate_parts(rb):
        r0 = rb * RL
        cache = {}

        def conv():
            if "xs" not in cache:
                xs = cb_ref[...]
                for j in range(LRU_CONV):
                    xs = xs + xpad[r0 + j * nb:r0 + j * nb + RL, :] * cw_ref[j:j + 1, :]
                cache["xs"] = xs
            return cache["xs"]

        def block(n):
            cs = slice(n * LRU_BLOCK_W, (n + 1) * LRU_BLOCK_W)
            xs = conv()[:, cs]
            z = jnp.dot(xs.astype(BF16), w_ref[n], preferred_element_type=F32)
            r = jax.nn.sigmoid(z[:, :LRU_BLOCK_W] + bias_ref[0:1, cs])
            ig = jax.nn.sigmoid(z[:, LRU_BLOCK_W:] + bias_ref[1:2, cs])
            log_a = (-LRU_C * r) * sp[:, cs]
            abuf[r0:r0 + RL, cs] = jnp.exp(log_a)
            th = jnp.tanh(log_a)
            bbuf[r0:r0 + RL, cs] = jnp.sqrt(-2.0 * th / (1.0 - th)) * (ig * xs)

        return [functools.partial(block, n) for n in range(LRU_BLOCKS)]

    def s5_drive_parts(s):
        cs = slice(s * 128, (s + 1) * 128)

        def part(rb):
            rs = slice(rb * RB, (rb + 1) * RB)
            buf[s % 2, rs, :] = jnp.dot(u_ref[rs, cs].astype(BF16), wd_ref[s], preferred_element_type=F32)

        return [functools.partial(part, rb) for rb in range(R // RB)]

    def s5_scan(s):
        sb = buf.at[s % 2]
        ar = jnp.broadcast_to(a_ref[s, 0:1, :], (8, NS))
        ai = jnp.broadcast_to(a_ref[s, 1:2, :], (8, NS))
        init = []
        for hh in range(nh):
            init += [st[s, hh * 8:(hh + 1) * 8, 0:NS], st[s, hh * 8:(hh + 1) * 8, NS:2 * NS]]

        def body(k, carry):
            t = (tc - 1 - k) if reverse else k
            out = []
            for hh in range(nh):
                sr, si = carry[2 * hh], carry[2 * hh + 1]
                row = pl.multiple_of(t * nb + hh * 8, 8)
                br = sb[pl.ds(row, 8), 0:NS]
                bi = sb[pl.ds(row, 8), NS:2 * NS]
                nsr = ar * sr - ai * si + br
                nsi = ar * si + ai * sr + bi
                sb[pl.ds(row, 8), 0:NS] = nsr
                sb[pl.ds(row, 8), NS:2 * NS] = nsi
                out += [nsr, nsi]
            return tuple(out)

        fin = lax.fori_loop(0, tc, body, tuple(init), unroll=2)
        for hh in range(nh):
            st[s, hh * 8:(hh + 1) * 8, 0:NS] = fin[2 * hh]
            st[s, hh * 8:(hh + 1) * 8, NS:2 * NS] = fin[2 * hh + 1]

    def s5_readout_parts(s):
        cs = slice(s * 128, (s + 1) * 128)

        def part(rb):
            rs = slice(rb * RB, (rb + 1) * RB)
            y = jnp.dot(buf[s % 2, rs, :].astype(BF16), wc_ref[s], preferred_element_type=F32)
            if final:
                ybuf[rs, cs] = y
            else:
                oa_ref[rs, cs] = y

        return [functools.partial(part, rb) for rb in range(R // RB)]

    def emit_interleaved(mxu_parts, vpu_parts):
        n = max(len(mxu_parts), len(vpu_parts))
        for k in range(n):
            if k < len(mxu_parts):
                mxu_parts[k]()
            if k < len(vpu_parts):
                vpu_parts[k]()

    n_gate_blocks = R // RL
    for s in range(S5_STRIPS + 1):
        mxu = (s5_readout_parts(s - 1) if s > 0 else []) + (s5_drive_parts(s) if s < S5_STRIPS else [])
        vpu = []
        for rb in range(n_gate_blocks):
            if rb % S5_STRIPS == s:
                vpu += lru_gate_parts(rb)
        emit_interleaved(mxu, vpu)
        if s < S5_STRIPS:
            s5_scan(s)

    init = tuple(lst[hh * 8:(hh + 1) * 8, :] for hh in range(nh))

    def lru_body(k, carry):
        t = (tc - 1 - k) if reverse else k
        out = []
        for hh in range(nh):
            row = pl.multiple_of(t * nb + hh * 8, 8)
            hnew = abuf[pl.ds(row, 8), :] * carry[hh] + bbuf[pl.ds(row, 8), :]
            bbuf[pl.ds(row, 8), :] = hnew
            out.append(hnew)
        return tuple(out)

    fin = lax.fori_loop(0, tc, lru_body, init, unroll=4)
    for hh in range(nh):
        lst[hh * 8:(hh + 1) * 8, :] = fin[hh]

    if not final:
        or_ref[...] = bbuf[...]
        return
    for rb in range(R // RB):
        rs = slice(rb * RB, (rb + 1) * RB)
        y = ybuf[rs, :] + yp_ref[rs, :] + dsk_ref[...] * u_ref[rs, :]
        g = jax.nn.gelu(y)
        z = jnp.dot(g.astype(BF16), gw_ref[...], preferred_element_type=F32) + gb_ref[...]
        res = g * jax.nn.sigmoid(z)
        for tt in range(RB // nb):
            o3[:, rb * (RB // nb) + tt, :] = res[tt * nb:(tt + 1) * nb, :]
    oa_ref[...] = o3[...].astype(BF16)
    for t in range(tc):
        rows = slice(t * nb, (t + 1) * nb)
        o3[:, t, :] = hp_ref[rows, :] + bbuf[rows, :]
    or_ref[...] = (o3[...] * jax.nn.gelu(g_ref[...])).astype(BF16)


def _chunk_order(nc, nlc, reverse):
    if reverse:
        return lambda i: nc - 1 - i
    return lambda i: lax.rem(i + nlc, nc)


def _recurrent(u, lx, lg, s5p, d_skip, glu_w, glu_b, conv_w, conv_b, wab, bias, lam, n_lat):
    nb, lt, _ = u.shape
    rows = nb * lt
    tc = S5_CHUNK
    R = tc * nb
    nc, nlc = lt // tc, n_lat // tc
    halo = 8
    wd, a, wc = s5p
    fwd = None
    for d in range(2):
        final = d == 1
        cidx = _chunk_order(nc, nlc, reverse=final)
        row_spec = pl.BlockSpec((R, GROUP_W), lambda i: (cidx(i), 0))
        bt_spec = pl.BlockSpec((nb, tc, GROUP_W), lambda i: (0, cidx(i), 0))
        prev_spec = pl.BlockSpec((nb, halo, GROUP_W),
                                 lambda i: (0, jnp.maximum(cidx(i) * (tc // halo) - 1, 0), 0))
        next_spec = pl.BlockSpec((nb, halo, GROUP_W),
                                 lambda i: (0, jnp.minimum((cidx(i) + 1) * (tc // halo), lt // halo - 1), 0))
        full = lambda shape: pl.BlockSpec(shape, lambda i: (0,) * len(shape))
        s5_w = [full((S5_STRIPS, 128, 2 * S5_STRIP_STATES)), full((S5_STRIPS, 2, S5_STRIP_STATES)),
                full((S5_STRIPS, 2 * S5_STRIP_STATES, 128))]
        lru_w = [full((LRU_CONV, GROUP_W)), full((1, GROUP_W)), full((LRU_BLOCKS, LRU_BLOCK_W, 2 * LRU_BLOCK_W)),
                 full((2, GROUP_W)), full((1, GROUP_W))]
        lru_args = [conv_w, conv_b.reshape(1, GROUP_W), wab[d], bias[d], lam[d].reshape(1, GROUP_W)]
        if final:
            in_specs = ([bt_spec, row_spec] + s5_w + [full((1, GROUP_W)), full((GROUP_W, GROUP_W)), full((1, GROUP_W))]
                        + [bt_spec, prev_spec, next_spec, row_spec, bt_spec] + lru_w)
            args = ([u, fwd[0], wd[d], a[d], wc[d], d_skip.reshape(1, GROUP_W), glu_w, glu_b.reshape(1, GROUP_W)]
                    + [lx, lx, lx, fwd[1], lg] + lru_args)
            out_specs = [bt_spec, bt_spec]
            out_shape = [jax.ShapeDtypeStruct((nb, lt, GROUP_W), BF16)] * 2
        else:
            in_specs = [bt_spec] + s5_w + [bt_spec, prev_spec, next_spec] + lru_w
            args = [u, wd[d], a[d], wc[d], lx, lx, lx] + lru_args
            out_specs = [row_spec, row_spec]
            out_shape = [jax.ShapeDtypeStruct((rows, GROUP_W), F32)] * 2
        scratch = [pltpu.VMEM((R, GROUP_W), F32), pltpu.VMEM((2, R, 2 * S5_STRIP_STATES), F32),
                   pltpu.VMEM((S5_STRIPS, nb, 2 * S5_STRIP_STATES), F32),
                   pltpu.VMEM((R + 3 * nb, GROUP_W), F32), pltpu.VMEM((R, GROUP_W), F32),
                   pltpu.VMEM((R, GROUP_W), F32), pltpu.VMEM((nb, GROUP_W), F32)]
        if final:
            scratch += [pltpu.VMEM((R, GROUP_W), F32), pltpu.VMEM((nb, tc, GROUP_W), F32)]
        fwd = pl.pallas_call(
            functools.partial(_rec_kernel, tc=tc, nb=nb, nlc=nlc, nc=nc, reverse=final, final=final),
            grid=(nc,), in_specs=in_specs, out_specs=out_specs, out_shape=out_shape,
            scratch_shapes=scratch,
            compiler_params=_cparams(("arbitrary",), 56), name="rec_rev" if final else "rec_fwd",
        )(*args)
    return fwd


def _s5_params(lam_re, lam_im, log_dt, b_re, b_im, c_re, c_im):
    lr = jnp.minimum(lam_re.astype(F32), S5_MIN_NEG)
    li = lam_im.astype(F32)
    dt = jnp.exp(log_dt.astype(F32))[..., None]
    mag = jnp.exp(lr * dt)
    ar, ai = mag * jnp.cos(li * dt), mag * jnp.sin(li * dt)
    den = lr * lr + li * li
    fr = ((ar - 1.0) * lr + ai * li) / den
    fi = (ai * lr - (ar - 1.0) * li) / den
    bbr = fr[..., None] * b_re - fi[..., None] * b_im
    bbi = fr[..., None] * b_im + fi[..., None] * b_re
    gs = 128 // S5_CH
    eye = jnp.eye(gs, dtype=F32)

    def drive(bb):
        t = bb.reshape(2, S5_STRIPS, gs, S5_STATE, S5_CH)
        return jnp.einsum('dsgpc,gh->dsgchp', t, eye).reshape(2, S5_STRIPS, 128, S5_STRIP_STATES)

    def read(cc):
        t = cc.reshape(2, S5_STRIPS, gs, S5_CH, S5_STATE)
        return jnp.einsum('dsgcp,gh->dsgphc', t, eye).reshape(2, S5_STRIPS, S5_STRIP_STATES, 128)

    wd = jnp.concatenate([drive(bbr), drive(bbi)], axis=-1).astype(BF16)
    wc = jnp.concatenate([read(c_re.astype(F32)), read(-c_im.astype(F32))], axis=-2).astype(BF16)
    a = jnp.stack([ar.reshape(2, S5_STRIPS, S5_STRIP_STATES), ai.reshape(2, S5_STRIPS, S5_STRIP_STATES)], axis=2)
    return wd, a, wc


def _mlstm_kernel(q_ref, k_ref, vt_ref, o_ref, gr_ref, gc_ref, br_ref, bc_ref, nw_ref, out_ref,
                  xb_scr, row_scr, nt_scr, u_scr, vec_scr, *, nch, nlat, hp):
    T = ML_CHUNK
    nt_dims = (((1,), (1,)), ((), ()))
    rid = lax.broadcasted_iota(jnp.int32, (T, T), 0)
    cid = lax.broadcasted_iota(jnp.int32, (T, T), 1)
    incl_rows = (rid <= cid).astype(F32)
    incl_cols = (rid >= cid).astype(F32)

    for j in range(hp):
        g = (gr_ref[j] + br_ref[j]).reshape(nch * 8, T)
        kind = lax.broadcasted_iota(jnp.int32, (nch * 8, T), 0) % 8
        lf = _log_sigmoid(g)
        pre = jnp.dot(lf, incl_rows, precision=HIGHEST, preferred_element_type=F32)
        b = jnp.where(kind == 1, pre[:, T - 1:T] - pre + lf, pre)
        x = g - pltpu.roll(b, 2, axis=0)
        row_scr[j] = jnp.where(kind < 2, b, x).reshape(nch, 8, T)
        gl = gc_ref[j] + bc_ref[j]
        kind = lax.broadcasted_iota(jnp.int32, (T, 128), 1) % 4
        lf = _log_sigmoid(gl)
        pre = jnp.dot(incl_cols, lf, precision=HIGHEST, preferred_element_type=F32)
        b = jnp.where(kind == 1, pre[T - 1:T, :] - pre + lf, pre)
        x = gl - pltpu.roll(b, 2, axis=1)
        for c in range(nch):
            for d in range(2):
                lane = c * 4 + 2 + d
                xb_scr[j, d, c] = jnp.broadcast_to(x[:, lane:lane + 1], (T, T))

    def independent(c, carry):
        rows = pl.ds(pl.multiple_of(c * T, T), T)
        pairs = [(j, d) for j in range(hp) for d in range(2)]
        hs = lambda j: slice(j * ML_DH, (j + 1) * ML_DH)
        swept = lambda d: (rid <= cid) if d == 0 else (rid >= cid)
        kq = {j: lax.dot_general(k_ref[rows, hs(j)], q_ref[rows, hs(j)], nt_dims, preferred_element_type=F32)
              for j in range(hp)}
        xm = {p: jnp.where(swept(p[1]), xb_scr[p[0], p[1], c], -jnp.inf) for p in pairs}
        a_row = {p: jnp.max(xm[p], axis=0, keepdims=True) for p in pairs}
        w0 = {}
        for j, d in pairs:
            a_last = a_row[j, d][:, T - 1:T] if d == 0 else a_row[j, d][:, 0:1]
            w0[j, d] = jnp.exp(row_scr[j, c, 2 + d:3 + d, :] - a_last)
        s0 = {p: kq[p[0]] * jnp.exp(xm[p] - a_row[p]) for p in pairs}
        for j, d in pairs:
            vt, kc = vt_ref[c, hs(j), :], k_ref[rows, hs(j)]
            nt_scr[j, d, c] = jnp.dot(vt, s0[j, d].astype(BF16), preferred_element_type=F32)
            wv = (vt.astype(F32) * w0[j, d]).astype(BF16)
            u_scr[j, d, c] = jnp.dot(wv, kc, preferred_element_type=F32)
            vec_scr[j, d, c, 0:1, :] = a_row[j, d]
            vec_scr[j, d, c, 1:2, :] = jnp.sum(s0[j, d], axis=0, keepdims=True)
            vec_scr[j, d, c, 2:3, :] = jnp.dot(jnp.broadcast_to(w0[j, d], (8, T)).astype(BF16), kc,
                                               preferred_element_type=F32)[0:1, :]
        return carry

    lax.fori_loop(0, nch, independent, 0)

    def pass2(c, d, j, state):
        cm, n, m = state
        hs = slice(j * ML_DH, (j + 1) * ML_DH)
        qc = q_ref[pl.ds(pl.multiple_of(c * T, T), T), hs]
        a_row, d0, n0 = vec_scr[j, d, c, 0:1, :], vec_scr[j, d, c, 1:2, :], vec_scr[j, d, c, 2:3, :]
        b_row = row_scr[j, c, d:d + 1, :]
        mu = jnp.maximum(m, a_row)
        inter, r = jnp.exp(m - mu), jnp.exp(a_row - mu)
        qct = lax.dot_general(cm.astype(BF16), qc, nt_dims, preferred_element_type=F32)
        qn = lax.dot_general(jnp.broadcast_to(n, (8, ML_DH)).astype(BF16), qc, nt_dims,
                             preferred_element_type=F32)[0:1, :]
        den = inter * qn + r * d0
        inv = 1.0 / jnp.maximum(jnp.abs(den), jnp.exp(-b_row - mu))
        nt_scr[j, d, c] = (inter * qct + r * nt_scr[j, d, c]) * inv
        a_last = a_row[:, T - 1:T] if d == 0 else a_row[:, 0:1]
        b_last = b_row[:, T - 1:T] if d == 0 else b_row[:, 0:1]
        mul = jnp.maximum(m, a_last)
        decay, rl = jnp.exp(m - mul), jnp.exp(a_last - mul)
        return decay * cm + rl * u_scr[j, d, c], decay * n + rl * n0, b_last + mul

    def states(i, carry):
        out = []
        for j in range(hp):
            out.append(pass2(lax.rem(i + nlat, nch), 0, j, carry[2 * j]))
            out.append(pass2(nch - 1 - i, 1, j, carry[2 * j + 1]))
        return tuple(out)

    zero = (jnp.zeros((ML_DH, ML_DH), F32), jnp.zeros((1, ML_DH), F32), jnp.zeros((1, 1), F32))
    lax.fori_loop(0, nch, states, (zero,) * (2 * hp))

    for c in range(nch):
        rows = slice(c * T, (c + 1) * T)
        for j in range(hp):
            hs = slice(j * ML_DH, (j + 1) * ML_DH)
            ht = nt_scr[j, 0, c] + nt_scr[j, 1, c]
            hn = ht * lax.rsqrt(jnp.mean(ht * ht, axis=0, keepdims=True) + EPS) * nw_ref[j]
            out_ref[rows, hs] = (hn.T * jax.nn.sigmoid(o_ref[rows, hs])).astype(BF16)


def _mlstm(mq, mk, mvt, mo, misc, mgt, ig_bias, fg_bias, out_norm, n_lat):
    B, Lt, _ = mq.shape
    T = ML_CHUNK
    nch, nlat = Lt // T, n_lat // T
    hp = ML_HEADS_PER_STEP
    assert 4 * nch <= 128
    gr = jnp.transpose(mgt.reshape(B, ML_HEADS, 4, nch, T), (0, 1, 3, 2, 4))
    gr = jnp.pad(gr, ((0, 0), (0, 0), (0, 0), (0, 4), (0, 0)))
    gc = misc[:, :, MLA_ROPE:MLA_ROPE + 16].reshape(B, nch, T, ML_HEADS, 4)
    gc = jnp.transpose(gc, (0, 3, 2, 1, 4)).reshape(B, ML_HEADS, T, nch * 4)
    gc = jnp.pad(gc, ((0, 0), (0, 0), (0, 0), (0, 128 - nch * 4)))
    kinds = jnp.concatenate([fg_bias.astype(F32), ig_bias.astype(F32)], axis=0).T
    br = jnp.broadcast_to(jnp.pad(kinds, ((0, 0), (0, 4)))[:, :, None], (ML_HEADS, 8, T))
    bc = jnp.pad(jnp.tile(kinds, (1, nch)), ((0, 0), (0, 128 - nch * 4)))[:, None, :]
    nwb = jnp.broadcast_to(out_norm.astype(F32)[:, :, None], (ML_HEADS, ML_DH, T))
    head_spec = pl.BlockSpec((None, Lt, hp * ML_DH), lambda b, h: (b, 0, h))
    slab = lambda: pltpu.VMEM((hp, 2, nch, T, T), F32)
    return pl.pallas_call(
        functools.partial(_mlstm_kernel, nch=nch, nlat=nlat, hp=hp),
        grid=(B, ML_HEADS // hp),
        in_specs=[head_spec, head_spec,
                  pl.BlockSpec((None, nch, hp * ML_DH, T), lambda b, h: (b, 0, h, 0)),
                  head_spec,
                  pl.BlockSpec((None, hp, nch, 8, T), lambda b, h: (b, h, 0, 0, 0)),
                  pl.BlockSpec((None, hp, T, 128), lambda b, h: (b, h, 0, 0)),
                  pl.BlockSpec((hp, 8, T), lambda b, h: (h, 0, 0)),
                  pl.BlockSpec((hp, 1, 128), lambda b, h: (h, 0, 0)),
                  pl.BlockSpec((hp, ML_DH, T), lambda b, h: (h, 0, 0))],
        out_specs=head_spec,
        out_shape=jax.ShapeDtypeStruct((B, Lt, GROUP_W), BF16),
        scratch_shapes=[slab(), pltpu.VMEM((hp, nch, 8, T), F32), slab(), slab(),
                        pltpu.VMEM((hp, 2, nch, 8, T), F32)],
        compiler_params=_cparams(("parallel", "parallel"), 60), name="mlstm",
    )(mq, mk, mvt, mo, gr, gc, br, bc, nwb)


def _mla_prep_parts(cq, ckv, misc, cos_ref, sin_ref, qan_ref, wq_ref, kvan_ref, wkv_ref, qn_ref, kn_ref,
                    qp_ref, kp_ref, v_ref):
    def rms(x, w):
        return (x * lax.rsqrt(jnp.mean(x * x, axis=-1, keepdims=True) + EPS)) * w

    rows = cq.shape[0]
    g = rows // cos_ref.shape[0]
    lane = lax.broadcasted_iota(jnp.int32, (rows, 128), 1)
    first = (lane % (MLA_ROPE // 2)) < (MLA_ROPE // 4)
    hw = MLA_HEADS * MLA_NOPE
    vals = {}

    def put(ref, lo, val):
        ref[:, :, lo:lo + val.shape[-1]] = val.astype(BF16).reshape(ref.shape[0], ref.shape[1], val.shape[-1])

    def rope(x):
        sw = jnp.where(first, pltpu.roll(x, 128 - MLA_ROPE // 4, axis=1), pltpu.roll(x, MLA_ROPE // 4, axis=1))
        return x * vals["cos"] + sw * vals["sin"]

    def project():
        vals["cos"] = jnp.concatenate([cos_ref[...]] * g, axis=0)
        vals["sin"] = jnp.concatenate([sin_ref[...]] * g, axis=0)
        vals["q"] = jnp.dot(rms(cq, qan_ref[...]).astype(BF16), wq_ref[...], preferred_element_type=F32)
        kv = jnp.dot(rms(ckv, kvan_ref[...]).astype(BF16), wkv_ref[...], preferred_element_type=F32)
        vals["k"] = kv[:, :hw]
        put(v_ref, 0, kv[:, hw:])
        kr = jnp.where(lane < MLA_ROPE, misc, 0.0)
        vals["kr_sq"] = jnp.sum(kr * kr, axis=-1, keepdims=True)
        vals["kr_rot"] = rope(kr * kn_ref[:, 128:256])

    def head(h):
        base = h * MLA_PAD
        qa, qb = vals["q"][:, base:base + 128], vals["q"][:, base + 128:base + 256]
        inv = lax.rsqrt(jnp.sum(qa * qa + qb * qb, axis=-1, keepdims=True) / MLA_QK + EPS)
        put(qp_ref, base, (qa * inv) * (qn_ref[:, 0:128] * ATTN_SCALE))
        put(qp_ref, base + 128, rope((qb * inv) * (qn_ref[:, 128:256] * ATTN_SCALE)))
        ka = vals["k"][:, h * MLA_NOPE:(h + 1) * MLA_NOPE]
        inv = lax.rsqrt((jnp.sum(ka * ka, axis=-1, keepdims=True) + vals["kr_sq"]) / MLA_QK + EPS)
        put(kp_ref, base, (ka * inv) * kn_ref[:, 0:128])
        put(kp_ref, base + 128, vals["kr_rot"] * inv)

    return [project] + [functools.partial(head, h) for h in range(MLA_HEADS)]


def _attn_kernel(q_ref, k_ref, v_ref, o_ref, *, n_lat, n_lat_tiles, sub):
    def attend(n_sub, k, v):
        for first in range(0, n_sub, 2):
            blocks = [slice(i * sub, (i + 1) * sub) for i in range(first, min(first + 2, n_sub))]
            s = [lax.dot_general(q_ref[r, :], k, (((1,), (1,)), ((), ())), preferred_element_type=F32)
                 for r in blocks]
            p = [jnp.exp(si - jnp.max(si, axis=-1, keepdims=True)) for si in s]
            for r, pi in zip(blocks, p):
                den = jnp.sum(pi, axis=-1, keepdims=True)
                o_ref[r, :] = (jnp.dot(pi.astype(BF16), v, preferred_element_type=F32) / den).astype(BF16)

    qi = pl.program_id(2)

    @pl.when(qi < n_lat_tiles)
    def _():
        attend(q_ref.shape[0] // sub, k_ref[...], v_ref[...])

    @pl.when(qi >= n_lat_tiles)
    def _():
        attend((k_ref.shape[0] - n_lat) // sub, k_ref[n_lat:, :], v_ref[n_lat:, :])


def _rope_tables(n_lat, n_ctx):
    n_rows = n_lat // GRID_W
    rows = jnp.repeat(jnp.arange(n_rows, dtype=F32), GRID_W)
    cols = jnp.tile(jnp.arange(GRID_W, dtype=F32), n_rows)
    n_freq = MLA_ROPE // 4
    inv_freq = ROPE_BASE ** (-jnp.arange(n_freq, dtype=F32) / n_freq)
    ar, ac = rows[:, None] * inv_freq, cols[:, None] * inv_freq
    cos = jnp.concatenate([jnp.cos(ar), jnp.cos(ar), jnp.cos(ac), jnp.cos(ac)], axis=1)
    sins = jnp.concatenate([-jnp.sin(ar), jnp.sin(ar), -jnp.sin(ac), jnp.sin(ac)], axis=1)
    cos = jnp.pad(cos, ((0, n_ctx), (0, 128 - MLA_ROPE)), constant_values=1.0)
    sins = jnp.pad(sins, ((0, n_ctx), (0, 128 - MLA_ROPE)))
    return cos, sins


def _mla_attn(qp, kp, v, n_lat):
    B, Lt, _ = v.shape
    tq = min(Q_TILE, n_lat)
    return pl.pallas_call(
        functools.partial(_attn_kernel, n_lat=n_lat, n_lat_tiles=n_lat // tq, sub=Q_SUB),
        grid=(B, MLA_HEADS, pl.cdiv(Lt, tq)),
        in_specs=[pl.BlockSpec((None, tq, MLA_PAD), lambda b, h, t: (b, t, h)),
                  pl.BlockSpec((None, Lt, MLA_PAD), lambda b, h, t: (b, 0, h)),
                  pl.BlockSpec((None, Lt, MLA_V), lambda b, h, t: (b, 0, h))],
        out_specs=pl.BlockSpec((None, tq, MLA_V), lambda b, h, t: (b, t, h)),
        out_shape=jax.ShapeDtypeStruct((B, Lt, GROUP_W), BF16),
        compiler_params=_cparams(("parallel", "parallel", "arbitrary"), 40), name="mla_attn",
    )(qp, kp, v)


def _wout_kernel(*refs, n_src, n_lat_tiles):
    x_refs, (mod_ref, a_ref, b_ref, m_ref, r_ref, w_ref, o_ref) = refs[:n_src], refs[n_src:]
    g, tm, d = o_ref.shape
    acc = None
    for k, ref in enumerate((a_ref, b_ref, m_ref, r_ref)):
        part = jnp.dot(ref[...].reshape(g * tm, GROUP_W), w_ref[k * GROUP_W:(k + 1) * GROUP_W, :],
                       preferred_element_type=F32)
        acc = part if acc is None else acc + part
    o_ref[...] = _tile_rows(x_refs, n_lat_tiles) + mod_ref[:, 2:3, :] * acc.reshape(g, tm, d)


def _wout(src, mod_l, a, b, m, r, w, layer, n_lat_tiles):
    B, _, D = src[0].shape
    Lt = sum(s.shape[1] for s in src)
    tm, g = TOKEN_TILE, ROW_BATCHES
    bm = lambda b, t: (b, t, 0)
    mix = pl.BlockSpec((g, tm, GROUP_W), bm)
    return pl.pallas_call(
        functools.partial(_wout_kernel, n_src=len(src), n_lat_tiles=n_lat_tiles), grid=(B // g, Lt // tm),
        in_specs=_residual_specs(src, tm, n_lat_tiles, g) + [
            pl.BlockSpec((g, 6, D), lambda b, t: (jnp.where(t >= n_lat_tiles, B // g, b), 0, 0)),
            mix, mix, mix, mix,
            pl.BlockSpec((None, 4 * GROUP_W, D), lambda b, t: (layer, 0, 0))],
        out_specs=pl.BlockSpec((g, tm, D), bm),
        out_shape=jax.ShapeDtypeStruct((B, Lt, D), F32),
        compiler_params=_cparams(("parallel", "parallel"), 52), name="out_proj",
    )(*src, mod_l, a, b, m, r, w)


def _mlp_kernel(x_ref, xc_ref, mod_ref, modc_ref, nw_ref, w1_ref, w2_ref, *rest, nf, rows, saxis):
    o_ref, h_ref, a_ref = rest[-3:]
    s = pl.program_id(saxis)

    def up(h, f, r):
        a = jnp.maximum(jnp.dot(h, w1_ref[...], preferred_element_type=F32), 0.0)
        a_ref[f, r, :] = (a * a).astype(BF16)

    @pl.when(s == 0)
    def _():
        x = x_ref[...].reshape(rows, x_ref.shape[-1])
        half = rows // 2
        for r in (slice(0, half), slice(half, rows)):
            h = _norm_mod(x[r], nw_ref[...], mod_ref[3:4, :], mod_ref[4:5, :])
            h_ref[r, :] = h
            up(h, 0, r)

    @pl.when(jnp.logical_and(s > 0, s < nf))
    def _():
        up(h_ref[...], s, slice(None))

    @pl.when(s >= nf)
    def _():
        y = jnp.dot(a_ref[0], w2_ref[0], preferred_element_type=F32)
        for f in range(1, nf):
            y += jnp.dot(a_ref[f], w2_ref[f], preferred_element_type=F32)
        o_ref[...] = xc_ref[...] + (modc_ref[5:6, :] * y).reshape(o_ref.shape)


def _mlp(X, mod_l, nw, w1, w2, layer, n_lat, with_ctx):
    B, Lt, D = X.shape
    n_ctx = Lt - n_lat
    ff = w1.shape[2]
    tm, tf, tn = MLP_ROWS, MLP_FF_TILE, MLP_OUT_TILE
    nf, nn = ff // tf, D // tn
    ns = nf + nn
    w2r = w2.reshape(w2.shape[0], nf, tf, D)
    nwr = nw.reshape(1, D)
    col = lambda s: jnp.maximum(s - nf, 0)
    scratch = [pltpu.VMEM((tm, D), BF16), pltpu.VMEM((nf, tm, tf), BF16)]
    lat = pl.pallas_call(
        functools.partial(_mlp_kernel, nf=nf, rows=tm, saxis=2), grid=(B, n_lat // tm, ns),
        in_specs=[pl.BlockSpec((None, tm, D), lambda b, t, s: (b, t, 0)),
                  pl.BlockSpec((None, tm, tn), lambda b, t, s: (b, t, col(s))),
                  pl.BlockSpec((None, 6, D), lambda b, t, s: (b, 0, 0)),
                  pl.BlockSpec((None, 6, tn), lambda b, t, s: (b, 0, col(s))),
                  pl.BlockSpec((1, D), lambda b, t, s: (0, 0)),
                  pl.BlockSpec((None, D, tf), lambda b, t, s: (layer, 0, jnp.minimum(s, nf - 1))),
                  pl.BlockSpec((None, nf, tf, tn), lambda b, t, s: (layer, 0, 0, col(s)))],
        out_specs=pl.BlockSpec((None, tm, tn), lambda b, t, s: (b, t, col(s))),
        out_shape=jax.ShapeDtypeStruct((B, Lt if with_ctx else n_lat, D), F32),
        scratch_shapes=scratch,
        compiler_params=_cparams(("parallel", "parallel", "arbitrary"), 58), name="mlp_lat",
    )(X, X, mod_l, mod_l, nwr, w1, w2r)
    if not with_ctx:
        return lat
    g = tm // n_ctx
    cblk = n_lat // n_ctx
    return pl.pallas_call(
        functools.partial(_mlp_kernel, nf=nf, rows=tm, saxis=1), grid=(B // g, ns),
        in_specs=[pl.BlockSpec((g, n_ctx, D), lambda i, s: (i, cblk, 0)),
                  pl.BlockSpec((g, n_ctx, tn), lambda i, s: (i, cblk, col(s))),
                  pl.BlockSpec((None, 6, D), lambda i, s: (B, 0, 0)),
                  pl.BlockSpec((None, 6, tn), lambda i, s: (B, 0, col(s))),
                  pl.BlockSpec((1, D), lambda i, s: (0, 0)),
                  pl.BlockSpec((None, D, tf), lambda i, s: (layer, 0, jnp.minimum(s, nf - 1))),
                  pl.BlockSpec((None, nf, tf, tn), lambda i, s: (layer, 0, 0, col(s))),
                  pl.BlockSpec(memory_space=pl.ANY)],
        out_specs=pl.BlockSpec((g, n_ctx, tn), lambda i, s: (i, cblk, col(s))),
        out_shape=jax.ShapeDtypeStruct((B, Lt, D), F32),
        input_output_aliases={7: 0},
        scratch_shapes=scratch,
        compiler_params=_cparams(("parallel", "arbitrary"), 58), name="mlp_ctx",
    )(X, X, mod_l, mod_l, nwr, w1, w2r, lat)


def _pack_w_in(w_in):
    offs = [0, 512, 1024, 1536, 2048, 2560, 2576, 2960, 3088, 3152, 3664, 4176]
    u, mq, mk, mv, mo, mg, cq, ckv, kr, lx, lg = [w_in[..., offs[i]:offs[i + 1]] for i in range(11)]
    depth, d = w_in.shape[:2]
    mg = jnp.transpose(mg.reshape(depth, d, 2, 2, ML_HEADS), (0, 1, 4, 3, 2))[:, :, :, ::-1, :].reshape(depth, d, 16)
    pad = jnp.zeros((depth, d, MISC_W - MLA_ROPE - 16), w_in.dtype)
    w_p = jnp.concatenate([u, mq, mk, mv, mo, cq, ckv, lx, lg, kr, mg, pad], axis=-1).astype(BF16)
    return w_p, jnp.transpose(mg, (0, 2, 1)).astype(BF16), jnp.transpose(mv, (0, 2, 1)).astype(BF16)


def _pack_mla(q_a_norm, w_q_up, kv_a_norm, w_kv_up, q_norm, k_norm):
    depth = w_q_up.shape[0]
    wq = w_q_up.reshape(depth, MLA_Q_LORA, MLA_HEADS, MLA_QK)
    wq = jnp.pad(wq, ((0, 0), (0, 0), (0, 0), (0, MLA_PAD - MLA_QK))).reshape(depth, MLA_Q_LORA, -1).astype(BF16)
    wkv = w_kv_up.reshape(depth, MLA_KV_LORA, MLA_HEADS, MLA_NOPE + MLA_V)
    wkv = jnp.concatenate([wkv[..., :MLA_NOPE].reshape(depth, MLA_KV_LORA, -1),
                           wkv[..., MLA_NOPE:].reshape(depth, MLA_KV_LORA, -1)], axis=-1).astype(BF16)
    row = lambda a: a.astype(F32)[:, None, :]
    head = lambda a: jnp.pad(a.astype(F32), ((0, 0), (0, MLA_PAD - MLA_QK)))[:, None, :]
    return row(q_a_norm), wq, row(kv_a_norm), wkv, head(q_norm), head(k_norm)


def _pack_lru(wa, ba, wx, bx):
    wab = jnp.concatenate([wa, wx], axis=-1).astype(BF16)
    bias = jnp.stack([ba, bx], axis=2).astype(F32)
    return wab, bias


def kernel(x, c, ctx, c_ctx, ada_w, ada_b, norm1_w, norm2_w, w_in, w_out, s5_lam_re, s5_lam_im, s5_log_dt, s5_b_re, s5_b_im, s5_c_re, s5_c_im, s5_d, s5_glu_w, s5_glu_b, ml_ig_bias, ml_fg_bias, ml_out_norm, mla_q_a_norm, mla_w_q_up, mla_kv_a_norm, mla_w_kv_up, mla_q_norm, mla_k_norm, lru_conv_w, lru_conv_b, lru_wa, lru_ba, lru_wx, lru_bx, lru_lam, mlp_w1, mlp_w2):
    B, L, D = x.shape
    Lc = ctx.shape[1]
    depth = ada_w.shape[0]
    n_lat_tiles = L // TOKEN_TILE
    src = (x.astype(F32), ctx.astype(F32))
    mod = _ada_all(c, c_ctx, ada_w, ada_b)
    w_p, wgt, wvt = _pack_w_in(w_in)
    mla_w = _pack_mla(mla_q_a_norm, mla_w_q_up, mla_kv_a_norm, mla_w_kv_up, mla_q_norm, mla_k_norm)
    wab, lru_bias = _pack_lru(lru_wa, lru_ba, lru_wx, lru_bx)
    tables = _rope_tables(L, Lc)
    w_out_b, glu_w_b = w_out.astype(BF16), s5_glu_w.astype(BF16)
    w1_b, w2_b = mlp_w1.astype(BF16), mlp_w2.astype(BF16)
    for l in range(depth):
        u, mq, mk, mv, mo, lx, lg, misc, mgt, qp, kp, v = _win(src, mod[l], norm1_w[l], w_p, wgt, wvt, mla_w, tables,
                                                               l, n_lat_tiles)
        s5p = _s5_params(s5_lam_re[l], s5_lam_im[l], s5_log_dt[l], s5_b_re[l], s5_b_im[l], s5_c_re[l], s5_c_im[l])
        a_mix, r_mix = _recurrent(u, lx, lg, s5p, s5_d[l], glu_w_b[l], s5_glu_b[l], lru_conv_w[l], lru_conv_b[l],
                                  wab[l], lru_bias[l], lru_lam[l], L)
        b_mix = _mlstm(mq, mk, mv, mo, misc, mgt, ml_ig_bias[l], ml_fg_bias[l], ml_out_norm[l], L)
        m_mix = _mla_attn(qp, kp, v, L)
        X = _wout(src, mod[l], a_mix, b_mix, m_mix, r_mix, w_out_b, l, n_lat_tiles)
        X = _mlp(X, mod[l], norm2_w[l], w1_b, w2_b, l, L, with_ctx=l < depth - 1)
        src = (X,)
    return X
```

```python
import functools

import jax
import jax.numpy as jnp
from jax import lax
from jax.experimental import pallas as pl
from jax.experimental.pallas import tpu as pltpu

F32 = jnp.float32
BF16 = jnp.bfloat16
EPS = 1e-6
HIGHEST = lax.Precision.HIGHEST

GROUP_W = 512
TOKEN_TILE = 256
ROW_BATCHES = 2
S5_STATE = 64
S5_CH = 16
S5_STRIPS = 4
S5_STRIP_STATES = 512
S5_MIN_NEG = -1e-4
S5_CHUNK = 64
ML_HEADS = 4
ML_DH = 128
ML_CHUNK = 128
ML_HEADS_PER_STEP = 4
MLA_HEADS = 4
MLA_NOPE = 128
MLA_ROPE = 64
MLA_QK = MLA_NOPE + MLA_ROPE
MLA_V = 128
MLA_Q_LORA = 384
MLA_KV_LORA = 128
MLA_PAD = 256
ATTN_SCALE = MLA_QK ** -0.5
ROPE_BASE = 10000.0
GRID_W = 64
Q_TILE = 1024
Q_SUB = 256
LRU_BLOCKS = 4
LRU_BLOCK_W = 128
LRU_CONV = 4
LRU_C = 8.0
MLP_FF_TILE = 2048
MLP_OUT_TILE = 512
MLP_ROWS = 512

COL_U, COL_MQ, COL_MK, COL_MV, COL_MO = 0, 512, 1024, 1536, 2048
COL_CQ, COL_CKV, COL_LX, COL_LG, COL_MISC = 2560, 2944, 3072, 3584, 4096
IN_COLS_PACKED = 4224
MISC_W = 128


def _cparams(sem, vmem_mb):
    return pltpu.CompilerParams(dimension_semantics=sem, vmem_limit_bytes=vmem_mb << 20)


def _norm_mod(x, nw, shift, scale):
    ms = jnp.mean(x * x, axis=-1, keepdims=True)
    y = (x * lax.rsqrt(ms + EPS)) * nw
    return (y * (1.0 + scale) + shift).astype(BF16)


def _log_sigmoid(x):
    return jnp.minimum(x, 0.0) - jnp.log1p(jnp.exp(-jnp.abs(x)))


def _softplus(x):
    return jnp.maximum(x, 0.0) + jnp.log1p(jnp.exp(-jnp.abs(x)))


def _ada_kernel(c_ref, w_ref, b_ref, o_ref):
    c = c_ref[...]
    act = (c * jax.nn.sigmoid(c)).astype(BF16)
    o_ref[...] = jnp.dot(act, w_ref[...].astype(BF16), preferred_element_type=F32) + b_ref[...]


def _ada_all(c, c_ctx, ada_w, ada_b):
    depth, d, n6 = ada_w.shape
    nb = c.shape[0]
    rows = nb + 8
    cc = jnp.concatenate([c.astype(F32), jnp.broadcast_to(c_ctx.astype(F32)[None], (8, d))], axis=0)
    tn = 1024
    out = pl.pallas_call(
        _ada_kernel,
        grid=(depth, n6 // tn),
        in_specs=[pl.BlockSpec((rows, d), lambda l, n: (0, 0)),
                  pl.BlockSpec((None, d, tn), lambda l, n: (l, 0, n)),
                  pl.BlockSpec((None, 1, tn), lambda l, n: (l, 0, n))],
        out_specs=pl.BlockSpec((None, rows, tn), lambda l, n: (l, 0, n)),
        out_shape=jax.ShapeDtypeStruct((depth, rows, n6), F32),
        compiler_params=_cparams(("parallel", "parallel"), 40),
        name="ada_mod",
    )(cc, ada_w, ada_b.reshape(depth, 1, n6))
    return out.reshape(depth, rows, 6, d)


def _tile_rows(x_refs, n_lat_tiles):
    if len(x_refs) == 1:
        return x_refs[0][...]
    return jnp.where(pl.program_id(1) >= n_lat_tiles, x_refs[1][...], x_refs[0][...])


def _win_kernel(*refs, n_src, n_lat_tiles):
    x_refs, refs = refs[:n_src], refs[n_src:]
    (mod_ref, nw_ref, w_ref, wgt_ref, wvt_ref, cos_ref, sin_ref, qan_ref, wq_ref, kvan_ref, wkv_ref, qn_ref, kn_ref,
     u_ref, mq_ref, mk_ref, mvt_ref, mo_ref, lx_ref, lg_ref, misc_ref, mgt_ref, qp_ref, kp_ref, v_ref) = refs
    g, tm, d = x_refs[0].shape
    h = _norm_mod(_tile_rows(x_refs, n_lat_tiles), nw_ref[...], mod_ref[:, 0:1, :], mod_ref[:, 1:2, :])
    h = h.reshape(g * tm, d)
    nt_dims = (((1,), (1,)), ((), ()))

    def mm(lo, width):
        return jnp.dot(h, w_ref[:, lo:lo + width], preferred_element_type=F32)

    def store(ref, lo, scale=None):
        def run():
            y = mm(lo, ref.shape[-1])
            ref[...] = (y if scale is None else y * scale).astype(ref.dtype).reshape(ref.shape)
        return run

    def values_t():
        mvt = lax.dot_general(wvt_ref[...], h, nt_dims, preferred_element_type=F32).astype(BF16)
        for bi in range(g):
            for cc in range(mvt_ref.shape[1]):
                lo = bi * tm + cc * ML_CHUNK
                mvt_ref[bi, cc] = mvt[:, lo:lo + ML_CHUNK]

    def gates_t():
        mgt = lax.dot_general(wgt_ref[...], h, nt_dims, preferred_element_type=F32)
        for bi in range(g):
            mgt_ref[bi] = mgt[:, bi * tm:(bi + 1) * tm]

    misc = mm(COL_MISC, MISC_W)
    misc_ref[...] = misc.reshape(misc_ref.shape)
    prep = _mla_prep_parts(mm(COL_CQ, MLA_Q_LORA), mm(COL_CKV, MLA_KV_LORA), misc, cos_ref, sin_ref, qan_ref,
                           wq_ref, kvan_ref, wkv_ref, qn_ref, kn_ref, qp_ref, kp_ref, v_ref)
    proj = [store(u_ref, COL_U), store(mq_ref, COL_MQ), store(mk_ref, COL_MK, ML_DH ** -0.5), values_t,
            store(mo_ref, COL_MO), store(lx_ref, COL_LX), store(lg_ref, COL_LG), gates_t]
    for k in range(max(len(prep), len(proj))):
        if k < len(proj):
            proj[k]()
        if k < len(prep):
            prep[k]()


def _residual_specs(src, tm, n_lat_tiles, g=None):
    D = src[0].shape[-1]
    if len(src) == 1:
        return [pl.BlockSpec((g, tm, D), lambda b, t: (b, t, 0))]
    return [pl.BlockSpec((g, tm, D), lambda b, t: (b, jnp.minimum(t, n_lat_tiles - 1), 0)),
            pl.BlockSpec((g, tm, D), lambda b, t: (b, jnp.maximum(t - n_lat_tiles, 0), 0),
                         pipeline_mode=pl.Buffered(1))]


def _win(src, mod_l, nw, w_p, wgt, wvt, mla_w, tables, layer, n_lat_tiles):
    B, _, D = src[0].shape
    Lt = sum(a.shape[1] for a in src)
    tm, g = TOKEN_TILE, ROW_BATCHES
    nt = Lt // tm
    hp = MLA_HEADS * MLA_PAD
    bm = lambda b, t: (b, t, 0)
    per_layer = lambda *shape: pl.BlockSpec((None,) + shape, lambda b, t: (layer,) + (0,) * len(shape))
    table = pl.BlockSpec((tm, 128), lambda b, t: (t, 0))
    in_specs = _residual_specs(src, tm, n_lat_tiles, g) + [
        pl.BlockSpec((g, 6, D), lambda b, t: (jnp.where(t >= n_lat_tiles, B // g, b), 0, 0)),
        pl.BlockSpec((1, D), lambda b, t: (0, 0)),
        pl.BlockSpec((None, D, IN_COLS_PACKED), lambda b, t: (layer, 0, 0), pipeline_mode=pl.Buffered(1)),
        per_layer(16, D), per_layer(GROUP_W, D), table, table,
        per_layer(1, MLA_Q_LORA), per_layer(MLA_Q_LORA, hp),
        per_layer(1, MLA_KV_LORA), per_layer(MLA_KV_LORA, MLA_HEADS * (MLA_NOPE + MLA_V)),
        per_layer(1, MLA_PAD), per_layer(1, MLA_PAD),
    ]
    cpt = tm // ML_CHUNK
    out_specs = [
        pl.BlockSpec((g, tm, GROUP_W), bm),
        pl.BlockSpec((g, tm, GROUP_W), bm),
        pl.BlockSpec((g, tm, GROUP_W), bm),
        pl.BlockSpec((g, cpt, GROUP_W, ML_CHUNK), lambda b, t: (b, t, 0, 0)),
        pl.BlockSpec((g, tm, GROUP_W), bm),
        pl.BlockSpec((g, tm, GROUP_W), bm),
        pl.BlockSpec((g, tm, GROUP_W), bm),
        pl.BlockSpec((g, tm, MISC_W), bm),
        pl.BlockSpec((g, 16, tm), lambda b, t: (b, 0, t)),
        pl.BlockSpec((g, tm, hp), bm),
        pl.BlockSpec((g, tm, hp), bm),
        pl.BlockSpec((g, tm, GROUP_W), bm),
    ]
    sds = jax.ShapeDtypeStruct
    out_shape = [
        sds((B, Lt, GROUP_W), F32),
        sds((B, Lt, GROUP_W), BF16), sds((B, Lt, GROUP_W), BF16),
        sds((B, Lt // ML_CHUNK, GROUP_W, ML_CHUNK), BF16),
        sds((B, Lt, GROUP_W), F32),
        sds((B, Lt, GROUP_W), F32), sds((B, Lt, GROUP_W), F32),
        sds((B, Lt, MISC_W), F32),
        sds((B, 16, Lt), F32),
        sds((B, Lt, hp), BF16), sds((B, Lt, hp), BF16), sds((B, Lt, GROUP_W), BF16),
    ]
    return pl.pallas_call(
        functools.partial(_win_kernel, n_src=len(src), n_lat_tiles=n_lat_tiles),
        grid=(B // g, nt), in_specs=in_specs, out_specs=out_specs, out_shape=out_shape,
        compiler_params=_cparams(("parallel", "parallel"), 58), name="in_proj",
    )(*src, mod_l, nw.reshape(1, D), w_p, wgt, wvt, tables[0], tables[1], *mla_w)


def _rec_kernel(*refs, tc, nb, nlc, nc, reverse, final):
    if final:
        (ub_ref, yp_ref, wd_ref, a_ref, wc_ref, dsk_ref, gw_ref, gb_ref,
         x_ref, xp_ref, xn_ref, hp_ref, g_ref, cw_ref, cb_ref, w_ref, bias_ref, lam_ref,
         oa_ref, or_ref, u_ref, buf, st, xpad, abuf, bbuf, lst, ybuf, o3) = refs
    else:
        (ub_ref, wd_ref, a_ref, wc_ref,
         x_ref, xp_ref, xn_ref, cw_ref, cb_ref, w_ref, bias_ref, lam_ref,
         oa_ref, or_ref, u_ref, buf, st, xpad, abuf, bbuf, lst) = refs
    i = pl.program_id(0)
    c = (nc - 1 - i) if reverse else lax.rem(i + nlc, nc)
    R = tc * nb
    RB = min(R, 512)
    RL = min(R, 256)
    NS = S5_STRIP_STATES
    nh = nb // 8
    left = LRU_CONV // 2
    halo = xp_ref.shape[1]

    @pl.when(i == 0)
    def _():
        st[...] = jnp.zeros_like(st)
        lst[...] = jnp.zeros_like(lst)

    starts = jnp.logical_or(c == 0, c == nlc)
    ends = jnp.logical_or(c == nlc - 1, c == nc - 1)
    for j in range(left):
        xpad[j * nb:(j + 1) * nb, :] = jnp.where(starts, 0.0, xp_ref[:, halo - left + j, :])
    for t in range(tc):
        u_ref[t * nb:(t + 1) * nb, :] = ub_ref[:, t, :]
        xpad[(left + t) * nb:(left + t + 1) * nb, :] = x_ref[:, t, :]
    for j in range(LRU_CONV - 1 - left):
        xpad[(left + tc + j) * nb:(left + tc + j + 1) * nb, :] = jnp.where(ends, 0.0, xn_ref[:, j, :])
    sp = _softplus(-lam_ref[...])

    def lru_gate_parts(rb):
        r0 = rb * RL
        cache = {}

        def conv():
            if "xs" not in cache:
                xs = cb_ref[...]
                for j in range(LRU_CONV):
                    xs = xs + xpad[r0 + j * nb:r0 + j * nb + RL, :] * cw_ref[j:j + 1, :]
                cache["xs"] = xs
            return cache["xs"]

        def block(n):
            cs = slice(n * LRU_BLOCK_W, (n + 1) * LRU_BLOCK_W)
            xs = conv()[:, cs]
            z = jnp.dot(xs.astype(BF16), w_ref[n], preferred_element_type=F32)
            r = jax.nn.sigmoid(z[:, :LRU_BLOCK_W] + bias_ref[0:1, cs])
            ig = jax.nn.sigmoid(z[:, LRU_BLOCK_W:] + bias_ref[1:2, cs])
            log_a = (-LRU_C * r) * sp[:, cs]
            abuf[r0:r0 + RL, cs] = jnp.exp(log_a)
            th = jnp.tanh(log_a)
            bbuf[r0:r0 + RL, cs] = jnp.sqrt(-2.0 * th / (1.0 - th)) * (ig * xs)

        return [functools.partial(block, n) for n in range(LRU_BLOCKS)]

    def s5_drive_parts(s):
        cs = slice(s * 128, (s + 1) * 128)

        def part(rb):
            rs = slice(rb * RB, (rb + 1) * RB)
            buf[s % 2, rs, :] = jnp.dot(u_ref[rs, cs].astype(BF16), wd_ref[s], preferred_element_type=F32)

        return [functools.partial(part, rb) for rb in range(R // RB)]

    def s5_scan(s):
        sb = buf.at[s % 2]
        ar = jnp.broadcast_to(a_ref[s, 0:1, :], (8, NS))
        ai = jnp.broadcast_to(a_ref[s, 1:2, :], (8, NS))
        init = []
        for hh in range(nh):
            init += [st[s, hh * 8:(hh + 1) * 8, 0:NS], st[s, hh * 8:(hh + 1) * 8, NS:2 * NS]]

        def body(k, carry):
            t = (tc - 1 - k) if reverse else k
            out = []
            for hh in range(nh):
                sr, si = carry[2 * hh], carry[2 * hh + 1]
                row = pl.multiple_of(t * nb + hh * 8, 8)
                br = sb[pl.ds(row, 8), 0:NS]
                bi = sb[pl.ds(row, 8), NS:2 * NS]
                nsr = ar * sr - ai * si + br
                nsi = ar * si + ai * sr + bi
                sb[pl.ds(row, 8), 0:NS] = nsr
                sb[pl.ds(row, 8), NS:2 * NS] = nsi
                out += [nsr, nsi]
            return tuple(out)

        fin = lax.fori_loop(0, tc, body, tuple(init), unroll=2)
        for hh in range(nh):
            st[s, hh * 8:(hh + 1) * 8, 0:NS] = fin[2 * hh]
            st[s, hh * 8:(hh + 1) * 8, NS:2 * NS] = fin[2 * hh + 1]

    def s5_readout_parts(s):
        cs = slice(s * 128, (s + 1) * 128)

        def part(rb):
            rs = slice(rb * RB, (rb + 1) * RB)
            y = jnp.dot(buf[s % 2, rs, :].astype(BF16), wc_ref[s], preferred_element_type=F32)
            if final:
                ybuf[rs, cs] = y
            else:
                oa_ref[rs, cs] = y

        return [functools.partial(part, rb) for rb in range(R // RB)]

    def emit_interleaved(mxu_parts, vpu_parts):
        n = max(len(mxu_parts), len(vpu_parts))
        for k in range(n):
            if k < len(mxu_parts):
                mxu_parts[k]()
            if k < len(vpu_parts):
                vpu_parts[k]()

    n_gate_blocks = R // RL
    for s in range(S5_STRIPS + 1):
        mxu = (s5_readout_parts(s - 1) if s > 0 else []) + (s5_drive_parts(s) if s < S5_STRIPS else [])
        vpu = []
        for rb in range(n_gate_blocks):
            if rb % S5_STRIPS == s:
                vpu += lru_gate_parts(rb)
        emit_interleaved(mxu, vpu)
        if s < S5_STRIPS:
            s5_scan(s)

    init = tuple(lst[hh * 8:(hh + 1) * 8, :] for hh in range(nh))

    def lru_body(k, carry):
        t = (tc - 1 - k) if reverse else k
        out = []
        for hh in range(nh):
            row = pl.multiple_of(t * nb + hh * 8, 8)
            hnew = abuf[pl.ds(row, 8), :] * carry[hh] + bbuf[pl.ds(row, 8), :]
            bbuf[pl.ds(row, 8), :] = hnew
            out.append(hnew)
        return tuple(out)

    fin = lax.fori_loop(0, tc, lru_body, init, unroll=4)
    for hh in range(nh):
        lst[hh * 8:(hh + 1) * 8, :] = fin[hh]

    if not final:
        or_ref[...] = bbuf[...]
        return
    for rb in range(R // RB):
        rs = slice(rb * RB, (rb + 1) * RB)
        y = ybuf[rs, :] + yp_ref[rs, :] + dsk_ref[...] * u_ref[rs, :]
        g = jax.nn.gelu(y)
        z = jnp.dot(g.astype(BF16), gw_ref[...], preferred_element_type=F32) + gb_ref[...]
        res = g * jax.nn.sigmoid(z)
        for tt in range(RB // nb):
            o3[:, rb * (RB // nb) + tt, :] = res[tt * nb:(tt + 1) * nb, :]
    oa_ref[...] = o3[...].astype(BF16)
    for t in range(tc):
        rows = slice(t * nb, (t + 1) * nb)
        o3[:, t, :] = hp_ref[rows, :] + bbuf[rows, :]
    or_ref[...] = (o3[...] * jax.nn.gelu(g_ref[...])).astype(BF16)


def _chunk_order(nc, nlc, reverse):
    if reverse:
        return lambda i: nc - 1 - i
    return lambda i: lax.rem(i + nlc, nc)


def _recurrent(u, lx, lg, s5p, d_skip, glu_w, glu_b, conv_w, conv_b, wab, bias, lam, n_lat):
    nb, lt, _ = u.shape
    rows = nb * lt
    tc = S5_CHUNK
    R = tc * nb
    nc, nlc = lt // tc, n_lat // tc
    halo = 8
    wd, a, wc = s5p
    fwd = None
    for d in range(2):
        final = d == 1
        cidx = _chunk_order(nc, nlc, reverse=final)
        row_spec = pl.BlockSpec((R, GROUP_W), lambda i: (cidx(i), 0))
        bt_spec = pl.BlockSpec((nb, tc, GROUP_W), lambda i: (0, cidx(i), 0))
        prev_spec = pl.BlockSpec((nb, halo, GROUP_W),
                                 lambda i: (0, jnp.maximum(cidx(i) * (tc // halo) - 1, 0), 0))
        next_spec = pl.BlockSpec((nb, halo, GROUP_W),
                                 lambda i: (0, jnp.minimum((cidx(i) + 1) * (tc // halo), lt // halo - 1), 0))
        full = lambda shape: pl.BlockSpec(shape, lambda i: (0,) * len(shape))
        s5_w = [full((S5_STRIPS, 128, 2 * S5_STRIP_STATES)), full((S5_STRIPS, 2, S5_STRIP_STATES)),
                full((S5_STRIPS, 2 * S5_STRIP_STATES, 128))]
        lru_w = [full((LRU_CONV, GROUP_W)), full((1, GROUP_W)), full((LRU_BLOCKS, LRU_BLOCK_W, 2 * LRU_BLOCK_W)),
                 full((2, GROUP_W)), full((1, GROUP_W))]
        lru_args = [conv_w, conv_b.reshape(1, GROUP_W), wab[d], bias[d], lam[d].reshape(1, GROUP_W)]
        if final:
            in_specs = ([bt_spec, row_spec] + s5_w + [full((1, GROUP_W)), full((GROUP_W, GROUP_W)), full((1, GROUP_W))]
                        + [bt_spec, prev_spec, next_spec, row_spec, bt_spec] + lru_w)
            args = ([u, fwd[0], wd[d], a[d], wc[d], d_skip.reshape(1, GROUP_W), glu_w, glu_b.reshape(1, GROUP_W)]
                    + [lx, lx, lx, fwd[1], lg] + lru_args)
            out_specs = [bt_spec, bt_spec]
            out_shape = [jax.ShapeDtypeStruct((nb, lt, GROUP_W), BF16)] * 2
        else:
            in_specs = [bt_spec] + s5_w + [bt_spec, prev_spec, next_spec] + lru_w
            args = [u, wd[d], a[d], wc[d], lx, lx, lx] + lru_args
            out_specs = [row_spec, row_spec]
            out_shape = [jax.ShapeDtypeStruct((rows, GROUP_W), F32)] * 2
        scratch = [pltpu.VMEM((R, GROUP_W), F32), pltpu.VMEM((2, R, 2 * S5_STRIP_STATES), F32),
                   pltpu.VMEM((S5_STRIPS, nb, 2 * S5_STRIP_STATES), F32),
                   pltpu.VMEM((R + 3 * nb, GROUP_W), F32), pltpu.VMEM((R, GROUP_W), F32),
                   pltpu.VMEM((R, GROUP_W), F32), pltpu.VMEM((nb, GROUP_W), F32)]
        if final:
            scratch += [pltpu.VMEM((R, GROUP_W), F32), pltpu.VMEM((nb, tc, GROUP_W), F32)]
        fwd = pl.pallas_call(
            functools.partial(_rec_kernel, tc=tc, nb=nb, nlc=nlc, nc=nc, reverse=final, final=final),
            grid=(nc,), in_specs=in_specs, out_specs=out_specs, out_shape=out_shape,
            scratch_shapes=scratch,
            compiler_params=_cparams(("arbitrary",), 56), name="rec_rev" if final else "rec_fwd",
        )(*args)
    return fwd


def _s5_disc_kernel(lr_ref, li_ref, ldt_ref, bre_ref, bim_ref, ar_ref, ai_ref, bbr_ref, bbi_ref):
    lr = jnp.minimum(lr_ref[...], S5_MIN_NEG)
    li = li_ref[...]
    dt = jnp.exp(ldt_ref[...])
    mag = jnp.exp(lr * dt)
    ar, ai = mag * jnp.cos(li * dt), mag * jnp.sin(li * dt)
    den = lr * lr + li * li
    fr = ((ar - 1.0) * lr + ai * li) / den
    fi = (ai * lr - (ar - 1.0) * li) / den
    ar_ref[...] = ar
    ai_ref[...] = ai
    bbr_ref[...] = fr * bre_ref[...] - fi * bim_ref[...]
    bbi_ref[...] = fr * bim_ref[...] + fi * bre_ref[...]


def _s5_params(lam_re, lam_im, log_dt, b_re, b_im, c_re, c_im):
    n, rb = lam_re.size, 512
    col = lambda t: t.astype(F32).reshape(n, 1)
    ldt = jnp.broadcast_to(log_dt.astype(F32)[..., None], lam_re.shape)
    cspec = pl.BlockSpec((rb, 1), lambda i: (i, 0))
    bspec = pl.BlockSpec((rb, S5_CH), lambda i: (i, 0))
    ar, ai, bbr, bbi = pl.pallas_call(
        _s5_disc_kernel, grid=(n // rb,), in_specs=[cspec, cspec, cspec, bspec, bspec],
        out_specs=[cspec, cspec, bspec, bspec],
        out_shape=[jax.ShapeDtypeStruct((n, 1), F32)] * 2 + [jax.ShapeDtypeStruct((n, S5_CH), F32)] * 2,
        compiler_params=_cparams(("parallel",), 32), name="s5_discretise",
    )(col(lam_re), col(lam_im), col(ldt), b_re.astype(F32).reshape(n, S5_CH), b_im.astype(F32).reshape(n, S5_CH))
    ar, ai = ar.reshape(lam_re.shape), ai.reshape(lam_re.shape)
    bbr, bbi = bbr.reshape(b_re.shape), bbi.reshape(b_re.shape)
    gs = 128 // S5_CH
    eye = jnp.eye(gs, dtype=F32)

    def drive(bb):
        t = bb.reshape(2, S5_STRIPS, gs, S5_STATE, S5_CH)
        return jnp.einsum('dsgpc,gh->dsgchp', t, eye).reshape(2, S5_STRIPS, 128, S5_STRIP_STATES)

    def read(cc):
        t = cc.reshape(2, S5_STRIPS, gs, S5_CH, S5_STATE)
        return jnp.einsum('dsgcp,gh->dsgphc', t, eye).reshape(2, S5_STRIPS, S5_STRIP_STATES, 128)

    wd = jnp.concatenate([drive(bbr), drive(bbi)], axis=-1).astype(BF16)
    wc = jnp.concatenate([read(c_re.astype(F32)), read(-c_im.astype(F32))], axis=-2).astype(BF16)
    a = jnp.stack([ar.reshape(2, S5_STRIPS, S5_STRIP_STATES), ai.reshape(2, S5_STRIPS, S5_STRIP_STATES)], axis=2)
    return wd, a, wc


def _mlstm_kernel(q_ref, k_ref, vt_ref, o_ref, gr_ref, gc_ref, br_ref, bc_ref, nw_ref, out_ref,
                  xb_scr, row_scr, nt_scr, u_scr, vec_scr, *, nch, nlat, hp):
    T = ML_CHUNK
    nt_dims = (((1,), (1,)), ((), ()))
    rid = lax.broadcasted_iota(jnp.int32, (T, T), 0)
    cid = lax.broadcasted_iota(jnp.int32, (T, T), 1)
    incl_rows = (rid <= cid).astype(F32)
    incl_cols = (rid >= cid).astype(F32)

    for j in range(hp):
        g = (gr_ref[j] + br_ref[j]).reshape(nch * 8, T)
        kind = lax.broadcasted_iota(jnp.int32, (nch * 8, T), 0) % 8
        lf = _log_sigmoid(g)
        pre = jnp.dot(lf, incl_rows, precision=HIGHEST, preferred_element_type=F32)
        b = jnp.where(kind == 1, pre[:, T - 1:T] - pre + lf, pre)
        x = g - pltpu.roll(b, 2, axis=0)
        row_scr[j] = jnp.where(kind < 2, b, x).reshape(nch, 8, T)
        gl = gc_ref[j] + bc_ref[j]
        kind = lax.broadcasted_iota(jnp.int32, (T, 128), 1) % 4
        lf = _log_sigmoid(gl)
        pre = jnp.dot(incl_cols, lf, precision=HIGHEST, preferred_element_type=F32)
        b = jnp.where(kind == 1, pre[T - 1:T, :] - pre + lf, pre)
        x = gl - pltpu.roll(b, 2, axis=1)
        for c in range(nch):
            for d in range(2):
                lane = c * 4 + 2 + d
                xb_scr[j, d, c] = jnp.broadcast_to(x[:, lane:lane + 1], (T, T))

    def independent(c, carry):
        rows = pl.ds(pl.multiple_of(c * T, T), T)
        pairs = [(j, d) for j in range(hp) for d in range(2)]
        hs = lambda j: slice(j * ML_DH, (j + 1) * ML_DH)
        swept = lambda d: (rid <= cid) if d == 0 else (rid >= cid)
        kq = {j: lax.dot_general(k_ref[rows, hs(j)], q_ref[rows, hs(j)], nt_dims, preferred_element_type=F32)
              for j in range(hp)}
        xm = {p: jnp.where(swept(p[1]), xb_scr[p[0], p[1], c], -jnp.inf) for p in pairs}
        a_row = {p: jnp.max(xm[p], axis=0, keepdims=True) for p in pairs}
        w0 = {}
        for j, d in pairs:
            a_last = a_row[j, d][:, T - 1:T] if d == 0 else a_row[j, d][:, 0:1]
            w0[j, d] = jnp.exp(row_scr[j, c, 2 + d:3 + d, :] - a_last)
        s0 = {p: kq[p[0]] * jnp.exp(xm[p] - a_row[p]) for p in pairs}
        for j, d in pairs:
            vt, kc = vt_ref[c, hs(j), :], k_ref[rows, hs(j)]
            nt_scr[j, d, c] = jnp.dot(vt, s0[j, d].astype(BF16), preferred_element_type=F32)
            wv = (vt.astype(F32) * w0[j, d]).astype(BF16)
            u_scr[j, d, c] = jnp.dot(wv, kc, preferred_element_type=F32)
            vec_scr[j, d, c, 0:1, :] = a_row[j, d]
            vec_scr[j, d, c, 1:2, :] = jnp.sum(s0[j, d], axis=0, keepdims=True)
            vec_scr[j, d, c, 2:3, :] = jnp.dot(jnp.broadcast_to(w0[j, d], (8, T)).astype(BF16), kc,
                                               preferred_element_type=F32)[0:1, :]
        return carry

    lax.fori_loop(0, nch, independent, 0)

    def pass2(c, d, j, state):
        cm, n, m = state
        hs = slice(j * ML_DH, (j + 1) * ML_DH)
        qc = q_ref[pl.ds(pl.multiple_of(c * T, T), T), hs]
        a_row, d0, n0 = vec_scr[j, d, c, 0:1, :], vec_scr[j, d, c, 1:2, :], vec_scr[j, d, c, 2:3, :]
        b_row = row_scr[j, c, d:d + 1, :]
        mu = jnp.maximum(m, a_row)
        inter, r = jnp.exp(m - mu), jnp.exp(a_row - mu)
        qct = lax.dot_general(cm.astype(BF16), qc, nt_dims, preferred_element_type=F32)
        qn = lax.dot_general(jnp.broadcast_to(n, (8, ML_DH)).astype(BF16), qc, nt_dims,
                             preferred_element_type=F32)[0:1, :]
        den = inter * qn + r * d0
        inv = 1.0 / jnp.maximum(jnp.abs(den), jnp.exp(-b_row - mu))
        nt_scr[j, d, c] = (inter * qct + r * nt_scr[j, d, c]) * inv
        a_last = a_row[:, T - 1:T] if d == 0 else a_row[:, 0:1]
        b_last = b_row[:, T - 1:T] if d == 0 else b_row[:, 0:1]
        mul = jnp.maximum(m, a_last)
        decay, rl = jnp.exp(m - mul), jnp.exp(a_last - mul)
        return decay * cm + rl * u_scr[j, d, c], decay * n + rl * n0, b_last + mul

    def states(i, carry):
        out = []
        for j in range(hp):
            out.append(pass2(lax.rem(i + nlat, nch), 0, j, carry[2 * j]))
            out.append(pass2(nch - 1 - i, 1, j, carry[2 * j + 1]))
        return tuple(out)

    zero = (jnp.zeros((ML_DH, ML_DH), F32), jnp.zeros((1, ML_DH), F32), jnp.zeros((1, 1), F32))
    lax.fori_loop(0, nch, states, (zero,) * (2 * hp))

    for c in range(nch):
        rows = slice(c * T, (c + 1) * T)
        for j in range(hp):
            hs = slice(j * ML_DH, (j + 1) * ML_DH)
            ht = nt_scr[j, 0, c] + nt_scr[j, 1, c]
            hn = ht * lax.rsqrt(jnp.mean(ht * ht, axis=0, keepdims=True) + EPS) * nw_ref[j]
            out_ref[rows, hs] = (hn.T * jax.nn.sigmoid(o_ref[rows, hs])).astype(BF16)


def _mlstm(mq, mk, mvt, mo, misc, mgt, ig_bias, fg_bias, out_norm, n_lat):
    B, Lt, _ = mq.shape
    T = ML_CHUNK
    nch, nlat = Lt // T, n_lat // T
    hp = ML_HEADS_PER_STEP
    assert 4 * nch <= 128
    gr = jnp.transpose(mgt.reshape(B, ML_HEADS, 4, nch, T), (0, 1, 3, 2, 4))
    gr = jnp.pad(gr, ((0, 0), (0, 0), (0, 0), (0, 4), (0, 0)))
    gc = misc[:, :, MLA_ROPE:MLA_ROPE + 16].reshape(B, nch, T, ML_HEADS, 4)
    gc = jnp.transpose(gc, (0, 3, 2, 1, 4)).reshape(B, ML_HEADS, T, nch * 4)
    gc = jnp.pad(gc, ((0, 0), (0, 0), (0, 0), (0, 128 - nch * 4)))
    kinds = jnp.concatenate([fg_bias.astype(F32), ig_bias.astype(F32)], axis=0).T
    br = jnp.broadcast_to(jnp.pad(kinds, ((0, 0), (0, 4)))[:, :, None], (ML_HEADS, 8, T))
    bc = jnp.pad(jnp.tile(kinds, (1, nch)), ((0, 0), (0, 128 - nch * 4)))[:, None, :]
    nwb = jnp.broadcast_to(out_norm.astype(F32)[:, :, None], (ML_HEADS, ML_DH, T))
    head_spec = pl.BlockSpec((None, Lt, hp * ML_DH), lambda b, h: (b, 0, h))
    slab = lambda: pltpu.VMEM((hp, 2, nch, T, T), F32)
    return pl.pallas_call(
        functools.partial(_mlstm_kernel, nch=nch, nlat=nlat, hp=hp),
        grid=(B, ML_HEADS // hp),
        in_specs=[head_spec, head_spec,
                  pl.BlockSpec((None, nch, hp * ML_DH, T), lambda b, h: (b, 0, h, 0)),
                  head_spec,
                  pl.BlockSpec((None, hp, nch, 8, T), lambda b, h: (b, h, 0, 0, 0)),
                  pl.BlockSpec((None, hp, T, 128), lambda b, h: (b, h, 0, 0)),
                  pl.BlockSpec((hp, 8, T), lambda b, h: (h, 0, 0)),
                  pl.BlockSpec((hp, 1, 128), lambda b, h: (h, 0, 0)),
                  pl.BlockSpec((hp, ML_DH, T), lambda b, h: (h, 0, 0))],
        out_specs=head_spec,
        out_shape=jax.ShapeDtypeStruct((B, Lt, GROUP_W), BF16),
        scratch_shapes=[slab(), pltpu.VMEM((hp, nch, 8, T), F32), slab(), slab(),
                        pltpu.VMEM((hp, 2, nch, 8, T), F32)],
        compiler_params=_cparams(("parallel", "parallel"), 60), name="mlstm",
    )(mq, mk, mvt, mo, gr, gc, br, bc, nwb)


def _mla_prep_parts(cq, ckv, misc, cos_ref, sin_ref, qan_ref, wq_ref, kvan_ref, wkv_ref, qn_ref, kn_ref,
                    qp_ref, kp_ref, v_ref):
    def rms(x, w):
        return (x * lax.rsqrt(jnp.mean(x * x, axis=-1, keepdims=True) + EPS)) * w

    rows = cq.shape[0]
    g = rows // cos_ref.shape[0]
    lane = lax.broadcasted_iota(jnp.int32, (rows, 128), 1)
    first = (lane % (MLA_ROPE // 2)) < (MLA_ROPE // 4)
    hw = MLA_HEADS * MLA_NOPE
    vals = {}

    def put(ref, lo, val):
        ref[:, :, lo:lo + val.shape[-1]] = val.astype(BF16).reshape(ref.shape[0], ref.shape[1], val.shape[-1])

    def rope(x):
        sw = jnp.where(first, pltpu.roll(x, 128 - MLA_ROPE // 4, axis=1), pltpu.roll(x, MLA_ROPE // 4, axis=1))
        return x * vals["cos"] + sw * vals["sin"]

    def project():
        vals["cos"] = jnp.concatenate([cos_ref[...]] * g, axis=0)
        vals["sin"] = jnp.concatenate([sin_ref[...]] * g, axis=0)
        vals["q"] = jnp.dot(rms(cq, qan_ref[...]).astype(BF16), wq_ref[...], preferred_element_type=F32)
        kv = jnp.dot(rms(ckv, kvan_ref[...]).astype(BF16), wkv_ref[...], preferred_element_type=F32)
        vals["k"] = kv[:, :hw]
        put(v_ref, 0, kv[:, hw:])
        kr = jnp.where(lane < MLA_ROPE, misc, 0.0)
        vals["kr_sq"] = jnp.sum(kr * kr, axis=-1, keepdims=True)
        vals["kr_rot"] = rope(kr * kn_ref[:, 128:256])

    def head(h):
        base = h * MLA_PAD
        qa, qb = vals["q"][:, base:base + 128], vals["q"][:, base + 128:base + 256]
        inv = lax.rsqrt(jnp.sum(qa * qa + qb * qb, axis=-1, keepdims=True) / MLA_QK + EPS)
        put(qp_ref, base, (qa * inv) * (qn_ref[:, 0:128] * ATTN_SCALE))
        put(qp_ref, base + 128, rope((qb * inv) * (qn_ref[:, 128:256] * ATTN_SCALE)))
        ka = vals["k"][:, h * MLA_NOPE:(h + 1) * MLA_NOPE]
        inv = lax.rsqrt((jnp.sum(ka * ka, axis=-1, keepdims=True) + vals["kr_sq"]) / MLA_QK + EPS)
        put(kp_ref, base, (ka * inv) * kn_ref[:, 0:128])
        put(kp_ref, base + 128, vals["kr_rot"] * inv)

    return [project] + [functools.partial(head, h) for h in range(MLA_HEADS)]


def _attn_kernel(q_ref, k_ref, v_ref, o_ref, *, n_lat, n_lat_tiles, sub):
    def attend(n_sub, k, v):
        for first in range(0, n_sub, 2):
            blocks = [slice(i * sub, (i + 1) * sub) for i in range(first, min(first + 2, n_sub))]
            s = [lax.dot_general(q_ref[r, :], k, (((1,), (1,)), ((), ())), preferred_element_type=F32)
                 for r in blocks]
            p = [jnp.exp(si - jnp.max(si, axis=-1, keepdims=True)) for si in s]
            for r, pi in zip(blocks, p):
                den = jnp.sum(pi, axis=-1, keepdims=True)
                o_ref[r, :] = (jnp.dot(pi.astype(BF16), v, preferred_element_type=F32) / den).astype(BF16)

    qi = pl.program_id(2)

    @pl.when(qi < n_lat_tiles)
    def _():
        attend(q_ref.shape[0] // sub, k_ref[...], v_ref[...])

    @pl.when(qi >= n_lat_tiles)
    def _():
        attend((k_ref.shape[0] - n_lat) // sub, k_ref[n_lat:, :], v_ref[n_lat:, :])


def _rope_tables(n_lat, n_ctx):
    n_rows = n_lat // GRID_W
    rows = jnp.repeat(jnp.arange(n_rows, dtype=F32), GRID_W)
    cols = jnp.tile(jnp.arange(GRID_W, dtype=F32), n_rows)
    n_freq = MLA_ROPE // 4
    inv_freq = ROPE_BASE ** (-jnp.arange(n_freq, dtype=F32) / n_freq)
    ar, ac = rows[:, None] * inv_freq, cols[:, None] * inv_freq
    cos = jnp.concatenate([jnp.cos(ar), jnp.cos(ar), jnp.cos(ac), jnp.cos(ac)], axis=1)
    sins = jnp.concatenate([-jnp.sin(ar), jnp.sin(ar), -jnp.sin(ac), jnp.sin(ac)], axis=1)
    cos = jnp.pad(cos, ((0, n_ctx), (0, 128 - MLA_ROPE)), constant_values=1.0)
    sins = jnp.pad(sins, ((0, n_ctx), (0, 128 - MLA_ROPE)))
    return cos, sins


def _mla_attn(qp, kp, v, n_lat):
    B, Lt, _ = v.shape
    tq = min(Q_TILE, n_lat)
    return pl.pallas_call(
        functools.partial(_attn_kernel, n_lat=n_lat, n_lat_tiles=n_lat // tq, sub=Q_SUB),
        grid=(B, MLA_HEADS, pl.cdiv(Lt, tq)),
        in_specs=[pl.BlockSpec((None, tq, MLA_PAD), lambda b, h, t: (b, t, h)),
                  pl.BlockSpec((None, Lt, MLA_PAD), lambda b, h, t: (b, 0, h)),
                  pl.BlockSpec((None, Lt, MLA_V), lambda b, h, t: (b, 0, h))],
        out_specs=pl.BlockSpec((None, tq, MLA_V), lambda b, h, t: (b, t, h)),
        out_shape=jax.ShapeDtypeStruct((B, Lt, GROUP_W), BF16),
        compiler_params=_cparams(("parallel", "parallel", "arbitrary"), 40), name="mla_attn",
    )(qp, kp, v)


def _wout_kernel(*refs, n_src, n_lat_tiles):
    x_refs, (mod_ref, a_ref, b_ref, m_ref, r_ref, w_ref, o_ref) = refs[:n_src], refs[n_src:]
    g, tm, d = o_ref.shape
    acc = None
    for k, ref in enumerate((a_ref, b_ref, m_ref, r_ref)):
        part = jnp.dot(ref[...].reshape(g * tm, GROUP_W), w_ref[k * GROUP_W:(k + 1) * GROUP_W, :],
                       preferred_element_type=F32)
        acc = part if acc is None else acc + part
    o_ref[...] = _tile_rows(x_refs, n_lat_tiles) + mod_ref[:, 2:3, :] * acc.reshape(g, tm, d)


def _wout(src, mod_l, a, b, m, r, w, layer, n_lat_tiles):
    B, _, D = src[0].shape
    Lt = sum(s.shape[1] for s in src)
    tm, g = TOKEN_TILE, ROW_BATCHES
    bm = lambda b, t: (b, t, 0)
    mix = pl.BlockSpec((g, tm, GROUP_W), bm)
    return pl.pallas_call(
        functools.partial(_wout_kernel, n_src=len(src), n_lat_tiles=n_lat_tiles), grid=(B // g, Lt // tm),
        in_specs=_residual_specs(src, tm, n_lat_tiles, g) + [
            pl.BlockSpec((g, 6, D), lambda b, t: (jnp.where(t >= n_lat_tiles, B // g, b), 0, 0)),
            mix, mix, mix, mix,
            pl.BlockSpec((None, 4 * GROUP_W, D), lambda b, t: (layer, 0, 0))],
        out_specs=pl.BlockSpec((g, tm, D), bm),
        out_shape=jax.ShapeDtypeStruct((B, Lt, D), F32),
        compiler_params=_cparams(("parallel", "parallel"), 52), name="out_proj",
    )(*src, mod_l, a, b, m, r, w)


def _mlp_kernel(x_ref, xc_ref, mod_ref, modc_ref, nw_ref, w1_ref, w2_ref, *rest, nf, rows, saxis):
    o_ref, h_ref, a_ref = rest[-3:]
    s = pl.program_id(saxis)

    def up(h, f, r):
        a = jnp.maximum(jnp.dot(h, w1_ref[...], preferred_element_type=F32), 0.0)
        a_ref[f, r, :] = (a * a).astype(BF16)

    @pl.when(s == 0)
    def _():
        x = x_ref[...].reshape(rows, x_ref.shape[-1])
        half = rows // 2
        for r in (slice(0, half), slice(half, rows)):
            h = _norm_mod(x[r], nw_ref[...], mod_ref[3:4, :], mod_ref[4:5, :])
            h_ref[r, :] = h
            up(h, 0, r)

    @pl.when(jnp.logical_and(s > 0, s < nf))
    def _():
        up(h_ref[...], s, slice(None))

    @pl.when(s >= nf)
    def _():
        y = jnp.dot(a_ref[0], w2_ref[0], preferred_element_type=F32)
        for f in range(1, nf):
            y += jnp.dot(a_ref[f], w2_ref[f], preferred_element_type=F32)
        o_ref[...] = xc_ref[...] + (modc_ref[5:6, :] * y).reshape(o_ref.shape)


def _mlp(X, mod_l, nw, w1, w2, layer, n_lat, with_ctx):
    B, Lt, D = X.shape
    n_ctx = Lt - n_lat
    ff = w1.shape[2]
    tm, tf, tn = MLP_ROWS, MLP_FF_TILE, MLP_OUT_TILE
    nf, nn = ff // tf, D // tn
    ns = nf + nn
    w2r = w2.reshape(w2.shape[0], nf, tf, D)
    nwr = nw.reshape(1, D)
    col = lambda s: jnp.maximum(s - nf, 0)
    scratch = [pltpu.VMEM((tm, D), BF16), pltpu.VMEM((nf, tm, tf), BF16)]
    lat = pl.pallas_call(
        functools.partial(_mlp_kernel, nf=nf, rows=tm, saxis=2), grid=(B, n_lat // tm, ns),
        in_specs=[pl.BlockSpec((None, tm, D), lambda b, t, s: (b, t, 0)),
                  pl.BlockSpec((None, tm, tn), lambda b, t, s: (b, t, col(s))),
                  pl.BlockSpec((None, 6, D), lambda b, t, s: (b, 0, 0)),
                  pl.BlockSpec((None, 6, tn), lambda b, t, s: (b, 0, col(s))),
                  pl.BlockSpec((1, D), lambda b, t, s: (0, 0)),
                  pl.BlockSpec((None, D, tf), lambda b, t, s: (layer, 0, jnp.minimum(s, nf - 1))),
                  pl.BlockSpec((None, nf, tf, tn), lambda b, t, s: (layer, 0, 0, col(s)))],
        out_specs=pl.BlockSpec((None, tm, tn), lambda b, t, s: (b, t, col(s))),
        out_shape=jax.ShapeDtypeStruct((B, Lt if with_ctx else n_lat, D), F32),
        scratch_shapes=scratch,
        compiler_params=_cparams(("parallel", "parallel", "arbitrary"), 58), name="mlp_lat",
    )(X, X, mod_l, mod_l, nwr, w1, w2r)
    if not with_ctx:
        return lat
    g = tm // n_ctx
    cblk = n_lat // n_ctx
    return pl.pallas_call(
        functools.partial(_mlp_kernel, nf=nf, rows=tm, saxis=1), grid=(B // g, ns),
        in_specs=[pl.BlockSpec((g, n_ctx, D), lambda i, s: (i, cblk, 0)),
                  pl.BlockSpec((g, n_ctx, tn), lambda i, s: (i, cblk, col(s))),
                  pl.BlockSpec((None, 6, D), lambda i, s: (B, 0, 0)),
                  pl.BlockSpec((None, 6, tn), lambda i, s: (B, 0, col(s))),
                  pl.BlockSpec((1, D), lambda i, s: (0, 0)),
                  pl.BlockSpec((None, D, tf), lambda i, s: (layer, 0, jnp.minimum(s, nf - 1))),
                  pl.BlockSpec((None, nf, tf, tn), lambda i, s: (layer, 0, 0, col(s))),
                  pl.BlockSpec(memory_space=pl.ANY)],
        out_specs=pl.BlockSpec((g, n_ctx, tn), lambda i, s: (i, cblk, col(s))),
        out_shape=jax.ShapeDtypeStruct((B, Lt, D), F32),
        input_output_aliases={7: 0},
        scratch_shapes=scratch,
        compiler_params=_cparams(("parallel", "arbitrary"), 58), name="mlp_ctx",
    )(X, X, mod_l, mod_l, nwr, w1, w2r, lat)


def _pack_w_in(w_in):
    offs = [0, 512, 1024, 1536, 2048, 2560, 2576, 2960, 3088, 3152, 3664, 4176]
    u, mq, mk, mv, mo, mg, cq, ckv, kr, lx, lg = [w_in[..., offs[i]:offs[i + 1]] for i in range(11)]
    depth, d = w_in.shape[:2]
    mg = jnp.transpose(mg.reshape(depth, d, 2, 2, ML_HEADS), (0, 1, 4, 3, 2))[:, :, :, ::-1, :].reshape(depth, d, 16)
    pad = jnp.zeros((depth, d, MISC_W - MLA_ROPE - 16), w_in.dtype)
    w_p = jnp.concatenate([u, mq, mk, mv, mo, cq, ckv, lx, lg, kr, mg, pad], axis=-1).astype(BF16)
    return w_p, jnp.transpose(mg, (0, 2, 1)).astype(BF16), jnp.transpose(mv, (0, 2, 1)).astype(BF16)


def _pack_mla(q_a_norm, w_q_up, kv_a_norm, w_kv_up, q_norm, k_norm):
    depth = w_q_up.shape[0]
    wq = w_q_up.reshape(depth, MLA_Q_LORA, MLA_HEADS, MLA_QK)
    wq = jnp.pad(wq, ((0, 0), (0, 0), (0, 0), (0, MLA_PAD - MLA_QK))).reshape(depth, MLA_Q_LORA, -1).astype(BF16)
    wkv = w_kv_up.reshape(depth, MLA_KV_LORA, MLA_HEADS, MLA_NOPE + MLA_V)
    wkv = jnp.concatenate([wkv[..., :MLA_NOPE].reshape(depth, MLA_KV_LORA, -1),
                           wkv[..., MLA_NOPE:].reshape(depth, MLA_KV_LORA, -1)], axis=-1).astype(BF16)
    row = lambda a: a.astype(F32)[:, None, :]
    head = lambda a: jnp.pad(a.astype(F32), ((0, 0), (0, MLA_PAD - MLA_QK)))[:, None, :]
    return row(q_a_norm), wq, row(kv_a_norm), wkv, head(q_norm), head(k_norm)


def _pack_lru(wa, ba, wx, bx):
    wab = jnp.concatenate([wa, wx], axis=-1).astype(BF16)
    bias = jnp.stack([ba, bx], axis=2).astype(F32)
    return wab, bias


def kernel(x, c, ctx, c_ctx, ada_w, ada_b, norm1_w, norm2_w, w_in, w_out, s5_lam_re, s5_lam_im, s5_log_dt, s5_b_re, s5_b_im, s5_c_re, s5_c_im, s5_d, s5_glu_w, s5_glu_b, ml_ig_bias, ml_fg_bias, ml_out_norm, mla_q_a_norm, mla_w_q_up, mla_kv_a_norm, mla_w_kv_up, mla_q_norm, mla_k_norm, lru_conv_w, lru_conv_b, lru_wa, lru_ba, lru_wx, lru_bx, lru_lam, mlp_w1, mlp_w2):
    B, L, D = x.shape
    Lc = ctx.shape[1]
    depth = ada_w.shape[0]
    n_lat_tiles = L // TOKEN_TILE
    src = (x.astype(F32), ctx.astype(F32))
    mod = _ada_all(c, c_ctx, ada_w, ada_b)
    w_p, wgt, wvt = _pack_w_in(w_in)
    mla_w = _pack_mla(mla_q_a_norm, mla_w_q_up, mla_kv_a_norm, mla_w_kv_up, mla_q_norm, mla_k_norm)
    wab, lru_bias = _pack_lru(lru_wa, lru_ba, lru_wx, lru_bx)
    tables = _rope_tables(L, Lc)
    w_out_b, glu_w_b = w_out.astype(BF16), s5_glu_w.astype(BF16)
    w1_b, w2_b = mlp_w1.astype(BF16), mlp_w2.astype(BF16)
    for l in range(depth):
        u, mq, mk, mv, mo, lx, lg, misc, mgt, qp, kp, v = _win(src, mod[l], norm1_w[l], w_p, wgt, wvt, mla_w, tables,
                                                               l, n_lat_tiles)
        s5p = _s5_params(s5_lam_re[l], s5_lam_im[l], s5_log_dt[l], s5_b_re[l], s5_b_im[l], s5_c_re[l], s5_c_im[l])
        a_mix, r_mix = _recurrent(u, lx, lg, s5p, s5_d[l], glu_w_b[l], s5_glu_b[l], lru_conv_w[l], lru_conv_b[l],
                                  wab[l], lru_bias[l], lru_lam[l], L)
        b_mix = _mlstm(mq, mk, mv, mo, misc, mgt, ml_ig_bias[l], ml_fg_bias[l], ml_out_norm[l], L)
        m_mix = _mla_attn(qp, kp, v, L)
        X = _wout(src, mod[l], a_mix, b_mix, m_mix, r_mix, w_out_b, l, n_lat_tiles)
        X = _mlp(X, mod[l], norm2_w[l], w1_b, w2_b, l, L, with_ctx=l < depth - 1)
        src = (X,)
    return X
```

```python
import functools

import jax
import jax.numpy as jnp
from jax import lax
from jax.experimental import pallas as pl
from jax.experimental.pallas import tpu as pltpu

F32 = jnp.float32
BF16 = jnp.bfloat16
EPS = 1e-6
HIGHEST = lax.Precision.HIGHEST

GROUP_W = 512
TOKEN_TILE = 256
ROW_BATCHES = 2
S5_STATE = 64
S5_CH = 16
S5_STRIPS = 4
S5_STRIP_STATES = 512
S5_MIN_NEG = -1e-4
S5_CHUNK = 64
ML_HEADS = 4
ML_DH = 128
ML_CHUNK = 128
ML_HEADS_PER_STEP = 4
MLA_HEADS = 4
MLA_NOPE = 128
MLA_ROPE = 64
MLA_QK = MLA_NOPE + MLA_ROPE
MLA_V = 128
MLA_Q_LORA = 384
MLA_KV_LORA = 128
MLA_PAD = 256
ATTN_SCALE = MLA_QK ** -0.5
ROPE_BASE = 10000.0
GRID_W = 64
Q_TILE = 1024
Q_SUB = 256
LRU_BLOCKS = 4
LRU_BLOCK_W = 128
LRU_CONV = 4
LRU_C = 8.0
MLP_FF_TILE = 2048
MLP_OUT_TILE = 512
MLP_ROWS = 512

COL_U, COL_MQ, COL_MK, COL_MV, COL_MO = 0, 512, 1024, 1536, 2048
COL_CQ, COL_CKV, COL_LX, COL_LG, COL_MISC = 2560, 2944, 3072, 3584, 4096
IN_COLS_PACKED = 4224
MISC_W = 128


def _cparams(sem, vmem_mb):
    return pltpu.CompilerParams(dimension_semantics=sem, vmem_limit_bytes=vmem_mb << 20)


def _norm_mod(x, nw, shift, scale):
    ms = jnp.mean(x * x, axis=-1, keepdims=True)
    y = (x * lax.rsqrt(ms + EPS)) * nw
    return (y * (1.0 + scale) + shift).astype(BF16)


def _log_sigmoid(x):
    return jnp.minimum(x, 0.0) - jnp.log1p(jnp.exp(-jnp.abs(x)))


def _softplus(x):
    return jnp.maximum(x, 0.0) + jnp.log1p(jnp.exp(-jnp.abs(x)))


def _ada_kernel(c_ref, w_ref, b_ref, o_ref):
    c = c_ref[...]
    act = (c * jax.nn.sigmoid(c)).astype(BF16)
    o_ref[...] = jnp.dot(act, w_ref[...].astype(BF16), preferred_element_type=F32) + b_ref[...]


def _ada_all(c, c_ctx, ada_w, ada_b):
    depth, d, n6 = ada_w.shape
    nb = c.shape[0]
    rows = nb + 8
    cc = jnp.concatenate([c.astype(F32), jnp.broadcast_to(c_ctx.astype(F32)[None], (8, d))], axis=0)
    tn = 1024
    out = pl.pallas_call(
        _ada_kernel,
        grid=(depth, n6 // tn),
        in_specs=[pl.BlockSpec((rows, d), lambda l, n: (0, 0)),
                  pl.BlockSpec((None, d, tn), lambda l, n: (l, 0, n)),
                  pl.BlockSpec((None, 1, tn), lambda l, n: (l, 0, n))],
        out_specs=pl.BlockSpec((None, rows, tn), lambda l, n: (l, 0, n)),
        out_shape=jax.ShapeDtypeStruct((depth, rows, n6), F32),
        compiler_params=_cparams(("parallel", "parallel"), 40),
        name="ada_mod",
    )(cc, ada_w, ada_b.reshape(depth, 1, n6))
    return out.reshape(depth, rows, 6, d)


def _tile_rows(x_refs, n_lat_tiles):
    if len(x_refs) == 1:
        return x_refs[0][...]
    return jnp.where(pl.program_id(1) >= n_lat_tiles, x_refs[1][...], x_refs[0][...])


def _win_kernel(*refs, n_src, n_lat_tiles):
    x_refs, refs = refs[:n_src], refs[n_src:]
    (mod_ref, nw_ref, w_ref, wgt_ref, wvt_ref, cos_ref, sin_ref, qan_ref, wq_ref, kvan_ref, wkv_ref, qn_ref, kn_ref,
     u_ref, mq_ref, mk_ref, mvt_ref, mo_ref, lx_ref, lg_ref, misc_ref, mgt_ref, qp_ref, kp_ref, v_ref) = refs
    g, tm, d = x_refs[0].shape
    h = _norm_mod(_tile_rows(x_refs, n_lat_tiles), nw_ref[...], mod_ref[:, 0:1, :], mod_ref[:, 1:2, :])
    h = h.reshape(g * tm, d)
    nt_dims = (((1,), (1,)), ((), ()))

    def mm(lo, width):
        return jnp.dot(h, w_ref[:, lo:lo + width], preferred_element_type=F32)

    def store(ref, lo, scale=None):
        def run():
            y = mm(lo, ref.shape[-1])
            ref[...] = (y if scale is None else y * scale).astype(ref.dtype).reshape(ref.shape)
        return run

    def values_t():
        mvt = lax.dot_general(wvt_ref[...], h, nt_dims, preferred_element_type=F32).astype(BF16)
        for bi in range(g):
            for cc in range(mvt_ref.shape[1]):
                lo = bi * tm + cc * ML_CHUNK
                mvt_ref[bi, cc] = mvt[:, lo:lo + ML_CHUNK]

    def gates_t():
        mgt = lax.dot_general(wgt_ref[...], h, nt_dims, preferred_element_type=F32)
        for bi in range(g):
            mgt_ref[bi] = mgt[:, bi * tm:(bi + 1) * tm]

    misc = mm(COL_MISC, MISC_W)
    misc_ref[...] = misc.reshape(misc_ref.shape)
    prep = _mla_prep_parts(mm(COL_CQ, MLA_Q_LORA), mm(COL_CKV, MLA_KV_LORA), misc, cos_ref, sin_ref, qan_ref,
                           wq_ref, kvan_ref, wkv_ref, qn_ref, kn_ref, qp_ref, kp_ref, v_ref)
    proj = [store(u_ref, COL_U), store(mq_ref, COL_MQ), store(mk_ref, COL_MK, ML_DH ** -0.5), values_t,
            store(mo_ref, COL_MO), store(lx_ref, COL_LX), store(lg_ref, COL_LG), gates_t]
    for k in range(max(len(prep), len(proj))):
        if k < len(proj):
            proj[k]()
        if k < len(prep):
            prep[k]()


def _residual_specs(src, tm, n_lat_tiles, g=None):
    D = src[0].shape[-1]
    if len(src) == 1:
        return [pl.BlockSpec((g, tm, D), lambda b, t: (b, t, 0))]
    return [pl.BlockSpec((g, tm, D), lambda b, t: (b, jnp.minimum(t, n_lat_tiles - 1), 0)),
            pl.BlockSpec((g, tm, D), lambda b, t: (b, jnp.maximum(t - n_lat_tiles, 0), 0),
                         pipeline_mode=pl.Buffered(1))]


def _win(src, mod_l, nw, w_p, wgt, wvt, mla_w, tables, layer, n_lat_tiles):
    B, _, D = src[0].shape
    Lt = sum(a.shape[1] for a in src)
    tm, g = TOKEN_TILE, ROW_BATCHES
    nt = Lt // tm
    hp = MLA_HEADS * MLA_PAD
    bm = lambda b, t: (b, t, 0)
    per_layer = lambda *shape: pl.BlockSpec((None,) + shape, lambda b, t: (layer,) + (0,) * len(shape))
    table = pl.BlockSpec((tm, 128), lambda b, t: (t, 0))
    in_specs = _residual_specs(src, tm, n_lat_tiles, g) + [
        pl.BlockSpec((g, 6, D), lambda b, t: (jnp.where(t >= n_lat_tiles, B // g, b), 0, 0)),
        pl.BlockSpec((1, D), lambda b, t: (0, 0)),
        pl.BlockSpec((None, D, IN_COLS_PACKED), lambda b, t: (layer, 0, 0), pipeline_mode=pl.Buffered(1)),
        per_layer(16, D), per_layer(GROUP_W, D), table, table,
        per_layer(1, MLA_Q_LORA), per_layer(MLA_Q_LORA, hp),
        per_layer(1, MLA_KV_LORA), per_layer(MLA_KV_LORA, MLA_HEADS * (MLA_NOPE + MLA_V)),
        per_layer(1, MLA_PAD), per_layer(1, MLA_PAD),
    ]
    cpt = tm // ML_CHUNK
    out_specs = [
        pl.BlockSpec((g, tm, GROUP_W), bm),
        pl.BlockSpec((g, tm, GROUP_W), bm),
        pl.BlockSpec((g, tm, GROUP_W), bm),
        pl.BlockSpec((g, cpt, GROUP_W, ML_CHUNK), lambda b, t: (b, t, 0, 0)),
        pl.BlockSpec((g, tm, GROUP_W), bm),
        pl.BlockSpec((g, tm, GROUP_W), bm),
        pl.BlockSpec((g, tm, GROUP_W), bm),
        pl.BlockSpec((g, tm, MISC_W), bm),
        pl.BlockSpec((g, 16, tm), lambda b, t: (b, 0, t)),
        pl.BlockSpec((g, tm, hp), bm),
        pl.BlockSpec((g, tm, hp), bm),
        pl.BlockSpec((g, tm, GROUP_W), bm),
    ]
    sds = jax.ShapeDtypeStruct
    out_shape = [
        sds((B, Lt, GROUP_W), F32),
        sds((B, Lt, GROUP_W), BF16), sds((B, Lt, GROUP_W), BF16),
        sds((B, Lt // ML_CHUNK, GROUP_W, ML_CHUNK), BF16),
        sds((B, Lt, GROUP_W), F32),
        sds((B, Lt, GROUP_W), F32), sds((B, Lt, GROUP_W), F32),
        sds((B, Lt, MISC_W), F32),
        sds((B, 16, Lt), F32),
        sds((B, Lt, hp), BF16), sds((B, Lt, hp), BF16), sds((B, Lt, GROUP_W), BF16),
    ]
    return pl.pallas_call(
        functools.partial(_win_kernel, n_src=len(src), n_lat_tiles=n_lat_tiles),
        grid=(B // g, nt), in_specs=in_specs, out_specs=out_specs, out_shape=out_shape,
        compiler_params=_cparams(("parallel", "parallel"), 58), name="in_proj",
    )(*src, mod_l, nw.reshape(1, D), w_p, wgt, wvt, tables[0], tables[1], *mla_w)


def _rec_kernel(*refs, tc, nb, nlc, nc, reverse, final):
    if final:
        (ub_ref, yp_ref, wd_ref, a_ref, wc_ref, dsk_ref, gw_ref, gb_ref,
         x_ref, xp_ref, xn_ref, hp_ref, g_ref, cw_ref, cb_ref, w_ref, bias_ref, lam_ref,
         oa_ref, or_ref, u_ref, buf, st, xpad, abuf, bbuf, lst, ybuf, o3) = refs
    else:
        (ub_ref, wd_ref, a_ref, wc_ref,
         x_ref, xp_ref, xn_ref, cw_ref, cb_ref, w_ref, bias_ref, lam_ref,
         oa_ref, or_ref, u_ref, buf, st, xpad, abuf, bbuf, lst) = refs
    i = pl.program_id(0)
    c = (nc - 1 - i) if reverse else lax.rem(i + nlc, nc)
    R = tc * nb
    RB = min(R, 512)
    RL = min(R, 256)
    NS = S5_STRIP_STATES
    nh = nb // 8
    left = LRU_CONV // 2
    halo = xp_ref.shape[1]

    @pl.when(i == 0)
    def _():
        st[...] = jnp.zeros_like(st)
        lst[...] = jnp.zeros_like(lst)

    starts = jnp.logical_or(c == 0, c == nlc)
    ends = jnp.logical_or(c == nlc - 1, c == nc - 1)
    for j in range(left):
        xpad[j * nb:(j + 1) * nb, :] = jnp.where(starts, 0.0, xp_ref[:, halo - left + j, :])
    for t in range(tc):
        u_ref[t * nb:(t + 1) * nb, :] = ub_ref[:, t, :]
        xpad[(left + t) * nb:(left + t + 1) * nb, :] = x_ref[:, t, :]
    for j in range(LRU_CONV - 1 - left):
        xpad[(left + tc + j) * nb:(left + tc + j + 1) * nb, :] = jnp.where(ends, 0.0, xn_ref[:, j, :])
    sp = _softplus(-lam_ref[...])

    def lru_gate_parts(rb):
        r0 = rb * RL
        cache = {}

        def conv():
            if "xs" not in cache:
                xs = cb_ref[...]
                for j in range(LRU_CONV):
                    xs = xs + xpad[r0 + j * nb:r0 + j * nb + RL, :] * cw_ref[j:j + 1, :]
                cache["xs"] = xs
            return cache["xs"]

        def block(n):
            cs = slice(n * LRU_BLOCK_W, (n + 1) * LRU_BLOCK_W)
            xs = conv()[:, cs]
            z = jnp.dot(xs.astype(BF16), w_ref[n], preferred_element_type=F32)
            r = jax.nn.sigmoid(z[:, :LRU_BLOCK_W] + bias_ref[0:1, cs])
            ig = jax.nn.sigmoid(z[:, LRU_BLOCK_W:] + bias_ref[1:2, cs])
            log_a = (-LRU_C * r) * sp[:, cs]
            abuf[r0:r0 + RL, cs] = jnp.exp(log_a)
            th = jnp.tanh(log_a)
            bbuf[r0:r0 + RL, cs] = jnp.sqrt(-2.0 * th / (1.0 - th)) * (ig * xs)

        return [functools.partial(block, n) for n in range(LRU_BLOCKS)]

    def s5_drive_parts(s):
        cs = slice(s * 128, (s + 1) * 128)

        def part(rb):
            rs = slice(rb * RB, (rb + 1) * RB)
            buf[s % 2, rs, :] = jnp.dot(u_ref[rs, cs].astype(BF16), wd_ref[s], preferred_element_type=F32)

        return [functools.partial(part, rb) for rb in range(R // RB)]

    def s5_scan(s):
        sb = buf.at[s % 2]
        ar = jnp.broadcast_to(a_ref[s, 0:1, :], (8, NS))
        ai = jnp.broadcast_to(a_ref[s, 1:2, :], (8, NS))
        init = []
        for hh in range(nh):
            init += [st[s, hh * 8:(hh + 1) * 8, 0:NS], st[s, hh * 8:(hh + 1) * 8, NS:2 * NS]]

        def body(k, carry):
            t = (tc - 1 - k) if reverse else k
            out = []
            for hh in range(nh):
                sr, si = carry[2 * hh], carry[2 * hh + 1]
                row = pl.multiple_of(t * nb + hh * 8, 8)
                br = sb[pl.ds(row, 8), 0:NS]
                bi = sb[pl.ds(row, 8), NS:2 * NS]
                nsr = ar * sr - ai * si + br
                nsi = ar * si + ai * sr + bi
                sb[pl.ds(row, 8), 0:NS] = nsr
                sb[pl.ds(row, 8), NS:2 * NS] = nsi
                out += [nsr, nsi]
            return tuple(out)

        fin = lax.fori_loop(0, tc, body, tuple(init), unroll=2)
        for hh in range(nh):
            st[s, hh * 8:(hh + 1) * 8, 0:NS] = fin[2 * hh]
            st[s, hh * 8:(hh + 1) * 8, NS:2 * NS] = fin[2 * hh + 1]

    def s5_readout_parts(s):
        cs = slice(s * 128, (s + 1) * 128)

        def part(rb):
            rs = slice(rb * RB, (rb + 1) * RB)
            y = jnp.dot(buf[s % 2, rs, :].astype(BF16), wc_ref[s], preferred_element_type=F32)
            if final:
                ybuf[rs, cs] = y
            else:
                oa_ref[rs, cs] = y

        return [functools.partial(part, rb) for rb in range(R // RB)]

    def emit_interleaved(mxu_parts, vpu_parts):
        n = max(len(mxu_parts), len(vpu_parts))
        for k in range(n):
            if k < len(mxu_parts):
                mxu_parts[k]()
            if k < len(vpu_parts):
                vpu_parts[k]()

    n_gate_blocks = R // RL
    for s in range(S5_STRIPS + 1):
        mxu = (s5_readout_parts(s - 1) if s > 0 else []) + (s5_drive_parts(s) if s < S5_STRIPS else [])
        vpu = []
        for rb in range(n_gate_blocks):
            if rb % S5_STRIPS == s:
                vpu += lru_gate_parts(rb)
        emit_interleaved(mxu, vpu)
        if s < S5_STRIPS:
            s5_scan(s)

    init = tuple(lst[hh * 8:(hh + 1) * 8, :] for hh in range(nh))

    def lru_body(k, carry):
        t = (tc - 1 - k) if reverse else k
        out = []
        for hh in range(nh):
            row = pl.multiple_of(t * nb + hh * 8, 8)
            hnew = abuf[pl.ds(row, 8), :] * carry[hh] + bbuf[pl.ds(row, 8), :]
            bbuf[pl.ds(row, 8), :] = hnew
            out.append(hnew)
        return tuple(out)

    fin = lax.fori_loop(0, tc, lru_body, init, unroll=4)
    for hh in range(nh):
        lst[hh * 8:(hh + 1) * 8, :] = fin[hh]

    if not final:
        or_ref[...] = bbuf[...]
        return
    for rb in range(R // RB):
        rs = slice(rb * RB, (rb + 1) * RB)
        y = ybuf[rs, :] + yp_ref[rs, :] + dsk_ref[...] * u_ref[rs, :]
        g = jax.nn.gelu(y)
        z = jnp.dot(g.astype(BF16), gw_ref[...], preferred_element_type=F32) + gb_ref[...]
        res = g * jax.nn.sigmoid(z)
        for tt in range(RB // nb):
            o3[:, rb * (RB // nb) + tt, :] = res[tt * nb:(tt + 1) * nb, :]
    oa_ref[...] = o3[...].astype(BF16)
    for t in range(tc):
        rows = slice(t * nb, (t + 1) * nb)
        o3[:, t, :] = hp_ref[rows, :] + bbuf[rows, :]
    or_ref[...] = (o3[...] * jax.nn.gelu(g_ref[...])).astype(BF16)


def _chunk_order(nc, nlc, reverse):
    if reverse:
        return lambda i: nc - 1 - i
    return lambda i: lax.rem(i + nlc, nc)


def _recurrent(u, lx, lg, s5p, d_skip, glu_w, glu_b, conv_w, conv_b, wab, bias, lam, n_lat):
    nb, lt, _ = u.shape
    rows = nb * lt
    tc = S5_CHUNK
    R = tc * nb
    nc, nlc = lt // tc, n_lat // tc
    halo = 8
    wd, a, wc = s5p
    fwd = None
    for d in range(2):
        final = d == 1
        cidx = _chunk_order(nc, nlc, reverse=final)
        row_spec = pl.BlockSpec((R, GROUP_W), lambda i: (cidx(i), 0))
        bt_spec = pl.BlockSpec((nb, tc, GROUP_W), lambda i: (0, cidx(i), 0))
        prev_spec = pl.BlockSpec((nb, halo, GROUP_W),
                                 lambda i: (0, jnp.maximum(cidx(i) * (tc // halo) - 1, 0), 0))
        next_spec = pl.BlockSpec((nb, halo, GROUP_W),
                                 lambda i: (0, jnp.minimum((cidx(i) + 1) * (tc // halo), lt // halo - 1), 0))
        full = lambda shape: pl.BlockSpec(shape, lambda i: (0,) * len(shape))
        s5_w = [full((S5_STRIPS, 128, 2 * S5_STRIP_STATES)), full((S5_STRIPS, 2, S5_STRIP_STATES)),
                full((S5_STRIPS, 2 * S5_STRIP_STATES, 128))]
        lru_w = [full((LRU_CONV, GROUP_W)), full((1, GROUP_W)), full((LRU_BLOCKS, LRU_BLOCK_W, 2 * LRU_BLOCK_W)),
                 full((2, GROUP_W)), full((1, GROUP_W))]
        lru_args = [conv_w, conv_b.reshape(1, GROUP_W), wab[d], bias[d], lam[d].reshape(1, GROUP_W)]
        if final:
            in_specs = ([bt_spec, row_spec] + s5_w + [full((1, GROUP_W)), full((GROUP_W, GROUP_W)), full((1, GROUP_W))]
                        + [bt_spec, prev_spec, next_spec, row_spec, bt_spec] + lru_w)
            args = ([u, fwd[0], wd[d], a[d], wc[d], d_skip.reshape(1, GROUP_W), glu_w, glu_b.reshape(1, GROUP_W)]
                    + [lx, lx, lx, fwd[1], lg] + lru_args)
            out_specs = [bt_spec, bt_spec]
            out_shape = [jax.ShapeDtypeStruct((nb, lt, GROUP_W), BF16)] * 2
        else:
            in_specs = [bt_spec] + s5_w + [bt_spec, prev_spec, next_spec] + lru_w
            args = [u, wd[d], a[d], wc[d], lx, lx, lx] + lru_args
            out_specs = [row_spec, row_spec]
            out_shape = [jax.ShapeDtypeStruct((rows, GROUP_W), F32)] * 2
        scratch = [pltpu.VMEM((R, GROUP_W), F32), pltpu.VMEM((2, R, 2 * S5_STRIP_STATES), F32),
                   pltpu.VMEM((S5_STRIPS, nb, 2 * S5_STRIP_STATES), F32),
                   pltpu.VMEM((R + 3 * nb, GROUP_W), F32), pltpu.VMEM((R, GROUP_W), F32),
                   pltpu.VMEM((R, GROUP_W), F32), pltpu.VMEM((nb, GROUP_W), F32)]
        if final:
            scratch += [pltpu.VMEM((R, GROUP_W), F32), pltpu.VMEM((nb, tc, GROUP_W), F32)]
        fwd = pl.pallas_call(
            functools.partial(_rec_kernel, tc=tc, nb=nb, nlc=nlc, nc=nc, reverse=final, final=final),
            grid=(nc,), in_specs=in_specs, out_specs=out_specs, out_shape=out_shape,
            scratch_shapes=scratch,
            compiler_params=_cparams(("arbitrary",), 56), name="rec_rev" if final else "rec_fwd",
        )(*args)
    return fwd


def _s5_pole_kernel(lr_ref, li_ref, ldt_ref, ar_ref, ai_ref, fr_ref, fi_ref):
    lr = jnp.minimum(lr_ref[...], S5_MIN_NEG)
    li = li_ref[...]
    dt = jnp.exp(ldt_ref[...])
    mag = jnp.exp(lr * dt)
    ar, ai = mag * jnp.cos(li * dt), mag * jnp.sin(li * dt)
    den = lr * lr + li * li
    ar_ref[...] = ar
    ai_ref[...] = ai
    fr_ref[...] = ((ar - 1.0) * lr + ai * li) / den
    fi_ref[...] = (ai * lr - (ar - 1.0) * li) / den


def _s5_input_kernel(fr_ref, fi_ref, bre_ref, bim_ref, bbr_ref, bbi_ref):
    fr, fi = fr_ref[...], fi_ref[...]
    bbr_ref[...] = fr * bre_ref[...] - fi * bim_ref[...]
    bbi_ref[...] = fr * bim_ref[...] + fi * bre_ref[...]


def _s5_discretise(lam_re, lam_im, log_dt, b_re, b_im):
    shape = lam_re.shape
    rows, p = lam_re.size // shape[-1], shape[-1]
    flat = lambda t: t.astype(F32).reshape(rows, p)
    ldt = jnp.broadcast_to(log_dt.astype(F32)[..., None], shape)
    ar, ai, fr, fi = pl.pallas_call(
        _s5_pole_kernel, out_shape=[jax.ShapeDtypeStruct((rows, p), F32)] * 4, name="s5_poles",
    )(flat(lam_re), flat(lam_im), flat(ldt))
    n, rb = lam_re.size, 2048
    cspec = pl.BlockSpec((rb, 1), lambda i: (i, 0))
    bspec = pl.BlockSpec((rb, S5_CH), lambda i: (i, 0))
    bbr, bbi = pl.pallas_call(
        _s5_input_kernel, grid=(n // rb,), in_specs=[cspec, cspec, bspec, bspec], out_specs=[bspec, bspec],
        out_shape=[jax.ShapeDtypeStruct((n, S5_CH), F32)] * 2,
        compiler_params=_cparams(("parallel",), 32), name="s5_input_matrix",
    )(fr.reshape(n, 1), fi.reshape(n, 1), b_re.astype(F32).reshape(n, S5_CH), b_im.astype(F32).reshape(n, S5_CH))
    return ar.reshape(shape), ai.reshape(shape), bbr.reshape(b_re.shape), bbi.reshape(b_re.shape)


def _s5_params(ar, ai, bbr, bbi, c_re, c_im):
    gs = 128 // S5_CH
    eye = jnp.eye(gs, dtype=F32)

    def drive(bb):
        t = bb.reshape(2, S5_STRIPS, gs, S5_STATE, S5_CH)
        return jnp.einsum('dsgpc,gh->dsgchp', t, eye).reshape(2, S5_STRIPS, 128, S5_STRIP_STATES)

    def read(cc):
        t = cc.reshape(2, S5_STRIPS, gs, S5_CH, S5_STATE)
        return jnp.einsum('dsgcp,gh->dsgphc', t, eye).reshape(2, S5_STRIPS, S5_STRIP_STATES, 128)

    wd = jnp.concatenate([drive(bbr), drive(bbi)], axis=-1).astype(BF16)
    wc = jnp.concatenate([read(c_re.astype(F32)), read(-c_im.astype(F32))], axis=-2).astype(BF16)
    a = jnp.stack([ar.reshape(2, S5_STRIPS, S5_STRIP_STATES), ai.reshape(2, S5_STRIPS, S5_STRIP_STATES)], axis=2)
    return wd, a, wc


def _mlstm_kernel(q_ref, k_ref, vt_ref, o_ref, gr_ref, gc_ref, br_ref, bc_ref, nw_ref, out_ref,
                  xb_scr, row_scr, nt_scr, u_scr, vec_scr, *, nch, nlat, hp):
    T = ML_CHUNK
    nt_dims = (((1,), (1,)), ((), ()))
    rid = lax.broadcasted_iota(jnp.int32, (T, T), 0)
    cid = lax.broadcasted_iota(jnp.int32, (T, T), 1)
    incl_rows = (rid <= cid).astype(F32)
    incl_cols = (rid >= cid).astype(F32)

    for j in range(hp):
        g = (gr_ref[j] + br_ref[j]).reshape(nch * 8, T)
        kind = lax.broadcasted_iota(jnp.int32, (nch * 8, T), 0) % 8
        lf = _log_sigmoid(g)
        pre = jnp.dot(lf, incl_rows, precision=HIGHEST, preferred_element_type=F32)
        b = jnp.where(kind == 1, pre[:, T - 1:T] - pre + lf, pre)
        x = g - pltpu.roll(b, 2, axis=0)
        row_scr[j] = jnp.where(kind < 2, b, x).reshape(nch, 8, T)
        gl = gc_ref[j] + bc_ref[j]
        kind = lax.broadcasted_iota(jnp.int32, (T, 128), 1) % 4
        lf = _log_sigmoid(gl)
        pre = jnp.dot(incl_cols, lf, precision=HIGHEST, preferred_element_type=F32)
        b = jnp.where(kind == 1, pre[T - 1:T, :] - pre + lf, pre)
        x = gl - pltpu.roll(b, 2, axis=1)
        for c in range(nch):
            for d in range(2):
                lane = c * 4 + 2 + d
                xb_scr[j, d, c] = jnp.broadcast_to(x[:, lane:lane + 1], (T, T))

    def independent(c, carry):
        rows = pl.ds(pl.multiple_of(c * T, T), T)
        pairs = [(j, d) for j in range(hp) for d in range(2)]
        hs = lambda j: slice(j * ML_DH, (j + 1) * ML_DH)
        swept = lambda d: (rid <= cid) if d == 0 else (rid >= cid)
        kq = {j: lax.dot_general(k_ref[rows, hs(j)], q_ref[rows, hs(j)], nt_dims, preferred_element_type=F32)
              for j in range(hp)}
        xm = {p: jnp.where(swept(p[1]), xb_scr[p[0], p[1], c], -jnp.inf) for p in pairs}
        a_row = {p: jnp.max(xm[p], axis=0, keepdims=True) for p in pairs}
        w0 = {}
        for j, d in pairs:
            a_last = a_row[j, d][:, T - 1:T] if d == 0 else a_row[j, d][:, 0:1]
            w0[j, d] = jnp.exp(row_scr[j, c, 2 + d:3 + d, :] - a_last)
        s0 = {p: kq[p[0]] * jnp.exp(xm[p] - a_row[p]) for p in pairs}
        for j, d in pairs:
            vt, kc = vt_ref[c, hs(j), :], k_ref[rows, hs(j)]
            nt_scr[j, d, c] = jnp.dot(vt, s0[j, d].astype(BF16), preferred_element_type=F32)
            wv = (vt.astype(F32) * w0[j, d]).astype(BF16)
            u_scr[j, d, c] = jnp.dot(wv, kc, preferred_element_type=F32)
            vec_scr[j, d, c, 0:1, :] = a_row[j, d]
            vec_scr[j, d, c, 1:2, :] = jnp.sum(s0[j, d], axis=0, keepdims=True)
            vec_scr[j, d, c, 2:3, :] = jnp.dot(jnp.broadcast_to(w0[j, d], (8, T)).astype(BF16), kc,
                                               preferred_element_type=F32)[0:1, :]
        return carry

    lax.fori_loop(0, nch, independent, 0)

    def pass2(c, d, j, state):
        cm, n, m = state
        hs = slice(j * ML_DH, (j + 1) * ML_DH)
        qc = q_ref[pl.ds(pl.multiple_of(c * T, T), T), hs]
        a_row, d0, n0 = vec_scr[j, d, c, 0:1, :], vec_scr[j, d, c, 1:2, :], vec_scr[j, d, c, 2:3, :]
        b_row = row_scr[j, c, d:d + 1, :]
        mu = jnp.maximum(m, a_row)
        inter, r = jnp.exp(m - mu), jnp.exp(a_row - mu)
        qct = lax.dot_general(cm.astype(BF16), qc, nt_dims, preferred_element_type=F32)
        qn = lax.dot_general(jnp.broadcast_to(n, (8, ML_DH)).astype(BF16), qc, nt_dims,
                             preferred_element_type=F32)[0:1, :]
        den = inter * qn + r * d0
        inv = 1.0 / jnp.maximum(jnp.abs(den), jnp.exp(-b_row - mu))
        nt_scr[j, d, c] = (inter * qct + r * nt_scr[j, d, c]) * inv
        a_last = a_row[:, T - 1:T] if d == 0 else a_row[:, 0:1]
        b_last = b_row[:, T - 1:T] if d == 0 else b_row[:, 0:1]
        mul = jnp.maximum(m, a_last)
        decay, rl = jnp.exp(m - mul), jnp.exp(a_last - mul)
        return decay * cm + rl * u_scr[j, d, c], decay * n + rl * n0, b_last + mul

    def states(i, carry):
        out = []
        for j in range(hp):
            out.append(pass2(lax.rem(i + nlat, nch), 0, j, carry[2 * j]))
            out.append(pass2(nch - 1 - i, 1, j, carry[2 * j + 1]))
        return tuple(out)

    zero = (jnp.zeros((ML_DH, ML_DH), F32), jnp.zeros((1, ML_DH), F32), jnp.zeros((1, 1), F32))
    lax.fori_loop(0, nch, states, (zero,) * (2 * hp))

    for c in range(nch):
        rows = slice(c * T, (c + 1) * T)
        for j in range(hp):
            hs = slice(j * ML_DH, (j + 1) * ML_DH)
            ht = nt_scr[j, 0, c] + nt_scr[j, 1, c]
            hn = ht * lax.rsqrt(jnp.mean(ht * ht, axis=0, keepdims=True) + EPS) * nw_ref[j]
            out_ref[rows, hs] = (hn.T * jax.nn.sigmoid(o_ref[rows, hs])).astype(BF16)


def _mlstm(mq, mk, mvt, mo, misc, mgt, ig_bias, fg_bias, out_norm, n_lat):
    B, Lt, _ = mq.shape
    T = ML_CHUNK
    nch, nlat = Lt // T, n_lat // T
    hp = ML_HEADS_PER_STEP
    assert 4 * nch <= 128
    gr = jnp.transpose(mgt.reshape(B, ML_HEADS, 4, nch, T), (0, 1, 3, 2, 4))
    gr = jnp.pad(gr, ((0, 0), (0, 0), (0, 0), (0, 4), (0, 0)))
    gc = misc[:, :, MLA_ROPE:MLA_ROPE + 16].reshape(B, nch, T, ML_HEADS, 4)
    gc = jnp.transpose(gc, (0, 3, 2, 1, 4)).reshape(B, ML_HEADS, T, nch * 4)
    gc = jnp.pad(gc, ((0, 0), (0, 0), (0, 0), (0, 128 - nch * 4)))
    kinds = jnp.concatenate([fg_bias.astype(F32), ig_bias.astype(F32)], axis=0).T
    br = jnp.broadcast_to(jnp.pad(kinds, ((0, 0), (0, 4)))[:, :, None], (ML_HEADS, 8, T))
    bc = jnp.pad(jnp.tile(kinds, (1, nch)), ((0, 0), (0, 128 - nch * 4)))[:, None, :]
    nwb = jnp.broadcast_to(out_norm.astype(F32)[:, :, None], (ML_HEADS, ML_DH, T))
    head_spec = pl.BlockSpec((None, Lt, hp * ML_DH), lambda b, h: (b, 0, h))
    slab = lambda: pltpu.VMEM((hp, 2, nch, T, T), F32)
    return pl.pallas_call(
        functools.partial(_mlstm_kernel, nch=nch, nlat=nlat, hp=hp),
        grid=(B, ML_HEADS // hp),
        in_specs=[head_spec, head_spec,
                  pl.BlockSpec((None, nch, hp * ML_DH, T), lambda b, h: (b, 0, h, 0)),
                  head_spec,
                  pl.BlockSpec((None, hp, nch, 8, T), lambda b, h: (b, h, 0, 0, 0)),
                  pl.BlockSpec((None, hp, T, 128), lambda b, h: (b, h, 0, 0)),
                  pl.BlockSpec((hp, 8, T), lambda b, h: (h, 0, 0)),
                  pl.BlockSpec((hp, 1, 128), lambda b, h: (h, 0, 0)),
                  pl.BlockSpec((hp, ML_DH, T), lambda b, h: (h, 0, 0))],
        out_specs=head_spec,
        out_shape=jax.ShapeDtypeStruct((B, Lt, GROUP_W), BF16),
        scratch_shapes=[slab(), pltpu.VMEM((hp, nch, 8, T), F32), slab(), slab(),
                        pltpu.VMEM((hp, 2, nch, 8, T), F32)],
        compiler_params=_cparams(("parallel", "parallel"), 60), name="mlstm",
    )(mq, mk, mvt, mo, gr, gc, br, bc, nwb)


def _mla_prep_parts(cq, ckv, misc, cos_ref, sin_ref, qan_ref, wq_ref, kvan_ref, wkv_ref, qn_ref, kn_ref,
                    qp_ref, kp_ref, v_ref):
    def rms(x, w):
        return (x * lax.rsqrt(jnp.mean(x * x, axis=-1, keepdims=True) + EPS)) * w

    rows = cq.shape[0]
    g = rows // cos_ref.shape[0]
    lane = lax.broadcasted_iota(jnp.int32, (rows, 128), 1)
    first = (lane % (MLA_ROPE // 2)) < (MLA_ROPE // 4)
    hw = MLA_HEADS * MLA_NOPE
    vals = {}

    def put(ref, lo, val):
        ref[:, :, lo:lo + val.shape[-1]] = val.astype(BF16).reshape(ref.shape[0], ref.shape[1], val.shape[-1])

    def rope(x):
        sw = jnp.where(first, pltpu.roll(x, 128 - MLA_ROPE // 4, axis=1), pltpu.roll(x, MLA_ROPE // 4, axis=1))
        return x * vals["cos"] + sw * vals["sin"]

    def project():
        vals["cos"] = jnp.concatenate([cos_ref[...]] * g, axis=0)
        vals["sin"] = jnp.concatenate([sin_ref[...]] * g, axis=0)
        vals["q"] = jnp.dot(rms(cq, qan_ref[...]).astype(BF16), wq_ref[...], preferred_element_type=F32)
        kv = jnp.dot(rms(ckv, kvan_ref[...]).astype(BF16), wkv_ref[...], preferred_element_type=F32)
        vals["k"] = kv[:, :hw]
        put(v_ref, 0, kv[:, hw:])
        kr = jnp.where(lane < MLA_ROPE, misc, 0.0)
        vals["kr_sq"] = jnp.sum(kr * kr, axis=-1, keepdims=True)
        vals["kr_rot"] = rope(kr * kn_ref[:, 128:256])

    def head(h):
        base = h * MLA_PAD
        qa, qb = vals["q"][:, base:base + 128], vals["q"][:, base + 128:base + 256]
        inv = lax.rsqrt(jnp.sum(qa * qa + qb * qb, axis=-1, keepdims=True) / MLA_QK + EPS)
        put(qp_ref, base, (qa * inv) * (qn_ref[:, 0:128] * ATTN_SCALE))
        put(qp_ref, base + 128, rope((qb * inv) * (qn_ref[:, 128:256] * ATTN_SCALE)))
        ka = vals["k"][:, h * MLA_NOPE:(h + 1) * MLA_NOPE]
        inv = lax.rsqrt((jnp.sum(ka * ka, axis=-1, keepdims=True) + vals["kr_sq"]) / MLA_QK + EPS)
        put(kp_ref, base, (ka * inv) * kn_ref[:, 0:128])
        put(kp_ref, base + 128, vals["kr_rot"] * inv)

    return [project] + [functools.partial(head, h) for h in range(MLA_HEADS)]


def _attn_kernel(q_ref, k_ref, v_ref, o_ref, *, n_lat, n_lat_tiles, sub):
    def attend(n_sub, k, v):
        for first in range(0, n_sub, 2):
            blocks = [slice(i * sub, (i + 1) * sub) for i in range(first, min(first + 2, n_sub))]
            s = [lax.dot_general(q_ref[r, :], k, (((1,), (1,)), ((), ())), preferred_element_type=F32)
                 for r in blocks]
            p = [jnp.exp(si - jnp.max(si, axis=-1, keepdims=True)) for si in s]
            for r, pi in zip(blocks, p):
                den = jnp.sum(pi, axis=-1, keepdims=True)
                o_ref[r, :] = (jnp.dot(pi.astype(BF16), v, preferred_element_type=F32) / den).astype(BF16)

    qi = pl.program_id(2)

    @pl.when(qi < n_lat_tiles)
    def _():
        attend(q_ref.shape[0] // sub, k_ref[...], v_ref[...])

    @pl.when(qi >= n_lat_tiles)
    def _():
        attend((k_ref.shape[0] - n_lat) // sub, k_ref[n_lat:, :], v_ref[n_lat:, :])


def _rope_tables(n_lat, n_ctx):
    n_rows = n_lat // GRID_W
    rows = jnp.repeat(jnp.arange(n_rows, dtype=F32), GRID_W)
    cols = jnp.tile(jnp.arange(GRID_W, dtype=F32), n_rows)
    n_freq = MLA_ROPE // 4
    inv_freq = ROPE_BASE ** (-jnp.arange(n_freq, dtype=F32) / n_freq)
    ar, ac = rows[:, None] * inv_freq, cols[:, None] * inv_freq
    cos = jnp.concatenate([jnp.cos(ar), jnp.cos(ar), jnp.cos(ac), jnp.cos(ac)], axis=1)
    sins = jnp.concatenate([-jnp.sin(ar), jnp.sin(ar), -jnp.sin(ac), jnp.sin(ac)], axis=1)
    cos = jnp.pad(cos, ((0, n_ctx), (0, 128 - MLA_ROPE)), constant_values=1.0)
    sins = jnp.pad(sins, ((0, n_ctx), (0, 128 - MLA_ROPE)))
    return cos, sins


def _mla_attn(qp, kp, v, n_lat):
    B, Lt, _ = v.shape
    tq = min(Q_TILE, n_lat)
    return pl.pallas_call(
        functools.partial(_attn_kernel, n_lat=n_lat, n_lat_tiles=n_lat // tq, sub=Q_SUB),
        grid=(B, MLA_HEADS, pl.cdiv(Lt, tq)),
        in_specs=[pl.BlockSpec((None, tq, MLA_PAD), lambda b, h, t: (b, t, h)),
                  pl.BlockSpec((None, Lt, MLA_PAD), lambda b, h, t: (b, 0, h)),
                  pl.BlockSpec((None, Lt, MLA_V), lambda b, h, t: (b, 0, h))],
        out_specs=pl.BlockSpec((None, tq, MLA_V), lambda b, h, t: (b, t, h)),
        out_shape=jax.ShapeDtypeStruct((B, Lt, GROUP_W), BF16),
        compiler_params=_cparams(("parallel", "parallel", "arbitrary"), 40), name="mla_attn",
    )(qp, kp, v)


def _wout_kernel(*refs, n_src, n_lat_tiles):
    x_refs, (mod_ref, a_ref, b_ref, m_ref, r_ref, w_ref, o_ref) = refs[:n_src], refs[n_src:]
    g, tm, d = o_ref.shape
    acc = None
    for k, ref in enumerate((a_ref, b_ref, m_ref, r_ref)):
        part = jnp.dot(ref[...].reshape(g * tm, GROUP_W), w_ref[k * GROUP_W:(k + 1) * GROUP_W, :],
                       preferred_element_type=F32)
        acc = part if acc is None else acc + part
    o_ref[...] = _tile_rows(x_refs, n_lat_tiles) + mod_ref[:, 2:3, :] * acc.reshape(g, tm, d)


def _wout(src, mod_l, a, b, m, r, w, layer, n_lat_tiles):
    B, _, D = src[0].shape
    Lt = sum(s.shape[1] for s in src)
    tm, g = TOKEN_TILE, ROW_BATCHES
    bm = lambda b, t: (b, t, 0)
    mix = pl.BlockSpec((g, tm, GROUP_W), bm)
    return pl.pallas_call(
        functools.partial(_wout_kernel, n_src=len(src), n_lat_tiles=n_lat_tiles), grid=(B // g, Lt // tm),
        in_specs=_residual_specs(src, tm, n_lat_tiles, g) + [
            pl.BlockSpec((g, 6, D), lambda b, t: (jnp.where(t >= n_lat_tiles, B // g, b), 0, 0)),
            mix, mix, mix, mix,
            pl.BlockSpec((None, 4 * GROUP_W, D), lambda b, t: (layer, 0, 0))],
        out_specs=pl.BlockSpec((g, tm, D), bm),
        out_shape=jax.ShapeDtypeStruct((B, Lt, D), F32),
        compiler_params=_cparams(("parallel", "parallel"), 52), name="out_proj",
    )(*src, mod_l, a, b, m, r, w)


def _mlp_kernel(x_ref, xc_ref, mod_ref, modc_ref, nw_ref, w1_ref, w2_ref, *rest, nf, rows, saxis):
    o_ref, h_ref, a_ref = rest[-3:]
    s = pl.program_id(saxis)

    def up(h, f, r):
        a = jnp.maximum(jnp.dot(h, w1_ref[...], preferred_element_type=F32), 0.0)
        a_ref[f, r, :] = (a * a).astype(BF16)

    @pl.when(s == 0)
    def _():
        x = x_ref[...].reshape(rows, x_ref.shape[-1])
        half = rows // 2
        for r in (slice(0, half), slice(half, rows)):
            h = _norm_mod(x[r], nw_ref[...], mod_ref[3:4, :], mod_ref[4:5, :])
            h_ref[r, :] = h
            up(h, 0, r)

    @pl.when(jnp.logical_and(s > 0, s < nf))
    def _():
        up(h_ref[...], s, slice(None))

    @pl.when(s >= nf)
    def _():
        y = jnp.dot(a_ref[0], w2_ref[0], preferred_element_type=F32)
        for f in range(1, nf):
            y += jnp.dot(a_ref[f], w2_ref[f], preferred_element_type=F32)
        o_ref[...] = xc_ref[...] + (modc_ref[5:6, :] * y).reshape(o_ref.shape)


def _mlp(X, mod_l, nw, w1, w2, layer, n_lat, with_ctx):
    B, Lt, D = X.shape
    n_ctx = Lt - n_lat
    ff = w1.shape[2]
    tm, tf, tn = MLP_ROWS, MLP_FF_TILE, MLP_OUT_TILE
    nf, nn = ff // tf, D // tn
    ns = nf + nn
    w2r = w2.reshape(w2.shape[0], nf, tf, D)
    nwr = nw.reshape(1, D)
    col = lambda s: jnp.maximum(s - nf, 0)
    scratch = [pltpu.VMEM((tm, D), BF16), pltpu.VMEM((nf, tm, tf), BF16)]
    lat = pl.pallas_call(
        functools.partial(_mlp_kernel, nf=nf, rows=tm, saxis=2), grid=(B, n_lat // tm, ns),
        in_specs=[pl.BlockSpec((None, tm, D), lambda b, t, s: (b, t, 0)),
                  pl.BlockSpec((None, tm, tn), lambda b, t, s: (b, t, col(s))),
                  pl.BlockSpec((None, 6, D), lambda b, t, s: (b, 0, 0)),
                  pl.BlockSpec((None, 6, tn), lambda b, t, s: (b, 0, col(s))),
                  pl.BlockSpec((1, D), lambda b, t, s: (0, 0)),
                  pl.BlockSpec((None, D, tf), lambda b, t, s: (layer, 0, jnp.minimum(s, nf - 1))),
                  pl.BlockSpec((None, nf, tf, tn), lambda b, t, s: (layer, 0, 0, col(s)))],
        out_specs=pl.BlockSpec((None, tm, tn), lambda b, t, s: (b, t, col(s))),
        out_shape=jax.ShapeDtypeStruct((B, Lt if with_ctx else n_lat, D), F32),
        scratch_shapes=scratch,
        compiler_params=_cparams(("parallel", "parallel", "arbitrary"), 58), name="mlp_lat",
    )(X, X, mod_l, mod_l, nwr, w1, w2r)
    if not with_ctx:
        return lat
    g = tm // n_ctx
    cblk = n_lat // n_ctx
    return pl.pallas_call(
        functools.partial(_mlp_kernel, nf=nf, rows=tm, saxis=1), grid=(B // g, ns),
        in_specs=[pl.BlockSpec((g, n_ctx, D), lambda i, s: (i, cblk, 0)),
                  pl.BlockSpec((g, n_ctx, tn), lambda i, s: (i, cblk, col(s))),
                  pl.BlockSpec((None, 6, D), lambda i, s: (B, 0, 0)),
                  pl.BlockSpec((None, 6, tn), lambda i, s: (B, 0, col(s))),
                  pl.BlockSpec((1, D), lambda i, s: (0, 0)),
                  pl.BlockSpec((None, D, tf), lambda i, s: (layer, 0, jnp.minimum(s, nf - 1))),
                  pl.BlockSpec((None, nf, tf, tn), lambda i, s: (layer, 0, 0, col(s))),
                  pl.BlockSpec(memory_space=pl.ANY)],
        out_specs=pl.BlockSpec((g, n_ctx, tn), lambda i, s: (i, cblk, col(s))),
        out_shape=jax.ShapeDtypeStruct((B, Lt, D), F32),
        input_output_aliases={7: 0},
        scratch_shapes=scratch,
        compiler_params=_cparams(("parallel", "arbitrary"), 58), name="mlp_ctx",
    )(X, X, mod_l, mod_l, nwr, w1, w2r, lat)


def _pack_w_in(w_in):
    offs = [0, 512, 1024, 1536, 2048, 2560, 2576, 2960, 3088, 3152, 3664, 4176]
    u, mq, mk, mv, mo, mg, cq, ckv, kr, lx, lg = [w_in[..., offs[i]:offs[i + 1]] for i in range(11)]
    depth, d = w_in.shape[:2]
    mg = jnp.transpose(mg.reshape(depth, d, 2, 2, ML_HEADS), (0, 1, 4, 3, 2))[:, :, :, ::-1, :].reshape(depth, d, 16)
    pad = jnp.zeros((depth, d, MISC_W - MLA_ROPE - 16), w_in.dtype)
    w_p = jnp.concatenate([u, mq, mk, mv, mo, cq, ckv, lx, lg, kr, mg, pad], axis=-1).astype(BF16)
    return w_p, jnp.transpose(mg, (0, 2, 1)).astype(BF16), jnp.transpose(mv, (0, 2, 1)).astype(BF16)


def _pack_mla(q_a_norm, w_q_up, kv_a_norm, w_kv_up, q_norm, k_norm):
    depth = w_q_up.shape[0]
    wq = w_q_up.reshape(depth, MLA_Q_LORA, MLA_HEADS, MLA_QK)
    wq = jnp.pad(wq, ((0, 0), (0, 0), (0, 0), (0, MLA_PAD - MLA_QK))).reshape(depth, MLA_Q_LORA, -1).astype(BF16)
    wkv = w_kv_up.reshape(depth, MLA_KV_LORA, MLA_HEADS, MLA_NOPE + MLA_V)
    wkv = jnp.concatenate([wkv[..., :MLA_NOPE].reshape(depth, MLA_KV_LORA, -1),
                           wkv[..., MLA_NOPE:].reshape(depth, MLA_KV_LORA, -1)], axis=-1).astype(BF16)
    row = lambda a: a.astype(F32)[:, None, :]
    head = lambda a: jnp.pad(a.astype(F32), ((0, 0), (0, MLA_PAD - MLA_QK)))[:, None, :]
    return row(q_a_norm), wq, row(kv_a_norm), wkv, head(q_norm), head(k_norm)


def _pack_lru(wa, ba, wx, bx):
    wab = jnp.concatenate([wa, wx], axis=-1).astype(BF16)
    bias = jnp.stack([ba, bx], axis=2).astype(F32)
    return wab, bias


def kernel(x, c, ctx, c_ctx, ada_w, ada_b, norm1_w, norm2_w, w_in, w_out, s5_lam_re, s5_lam_im, s5_log_dt, s5_b_re, s5_b_im, s5_c_re, s5_c_im, s5_d, s5_glu_w, s5_glu_b, ml_ig_bias, ml_fg_bias, ml_out_norm, mla_q_a_norm, mla_w_q_up, mla_kv_a_norm, mla_w_kv_up, mla_q_norm, mla_k_norm, lru_conv_w, lru_conv_b, lru_wa, lru_ba, lru_wx, lru_bx, lru_lam, mlp_w1, mlp_w2):
    B, L, D = x.shape
    Lc = ctx.shape[1]
    depth = ada_w.shape[0]
    n_lat_tiles = L // TOKEN_TILE
    src = (x.astype(F32), ctx.astype(F32))
    mod = _ada_all(c, c_ctx, ada_w, ada_b)
    w_p, wgt, wvt = _pack_w_in(w_in)
    mla_w = _pack_mla(mla_q_a_norm, mla_w_q_up, mla_kv_a_norm, mla_w_kv_up, mla_q_norm, mla_k_norm)
    wab, lru_bias = _pack_lru(lru_wa, lru_ba, lru_wx, lru_bx)
    tables = _rope_tables(L, Lc)
    w_out_b, glu_w_b = w_out.astype(BF16), s5_glu_w.astype(BF16)
    w1_b, w2_b = mlp_w1.astype(BF16), mlp_w2.astype(BF16)
    s5_ar, s5_ai, s5_bbr, s5_bbi = _s5_discretise(s5_lam_re, s5_lam_im, s5_log_dt, s5_b_re, s5_b_im)
    for l in range(depth):
        u, mq, mk, mv, mo, lx, lg, misc, mgt, qp, kp, v = _win(src, mod[l], norm1_w[l], w_p, wgt, wvt, mla_w, tables,
                                                               l, n_lat_tiles)
        s5p = _s5_params(s5_ar[l], s5_ai[l], s5_bbr[l], s5_bbi[l], s5_c_re[l], s5_c_im[l])
        a_mix, r_mix = _recurrent(u, lx, lg, s5p, s5_d[l], glu_w_b[l], s5_glu_b[l], lru_conv_w[l], lru_conv_b[l],
                                  wab[l], lru_bias[l], lru_lam[l], L)
        b_mix = _mlstm(mq, mk, mv, mo, misc, mgt, ml_ig_bias[l], ml_fg_bias[l], ml_out_norm[l], L)
        m_mix = _mla_attn(qp, kp, v, L)
        X = _wout(src, mod[l], a_mix, b_mix, m_mix, r_mix, w_out_b, l, n_lat_tiles)
        X = _mlp(X, mod[l], norm2_w[l], w1_b, w2_b, l, L, with_ctx=l < depth - 1)
        src = (X,)
    return X
```

```python
import functools

import jax
import jax.numpy as jnp
from jax import lax
from jax.experimental import pallas as pl
from jax.experimental.pallas import tpu as pltpu

F32 = jnp.float32
BF16 = jnp.bfloat16
EPS = 1e-6
HIGHEST = lax.Precision.HIGHEST

GROUP_W = 512
TOKEN_TILE = 256
ROW_BATCHES = 2
S5_STATE = 64
S5_CH = 16
S5_STRIPS = 4
S5_STRIP_STATES = 512
S5_MIN_NEG = -1e-4
S5_CHUNK = 64
ML_HEADS = 4
ML_DH = 128
ML_CHUNK = 128
ML_HEADS_PER_STEP = 4
MLA_HEADS = 4
MLA_NOPE = 128
MLA_ROPE = 64
MLA_QK = MLA_NOPE + MLA_ROPE
MLA_V = 128
MLA_Q_LORA = 384
MLA_KV_LORA = 128
MLA_PAD = 256
ATTN_SCALE = MLA_QK ** -0.5
ROPE_BASE = 10000.0
GRID_W = 64
Q_TILE = 2048
Q_SUB = 256
LRU_BLOCKS = 4
LRU_BLOCK_W = 128
LRU_CONV = 4
LRU_C = 8.0
MLP_FF_TILE = 2048
MLP_OUT_TILE = 512
MLP_ROWS = 512

COL_U, COL_MQ, COL_MK, COL_MV, COL_MO = 0, 512, 1024, 1536, 2048
COL_CQ, COL_CKV, COL_LX, COL_LG, COL_MISC = 2560, 2944, 3072, 3584, 4096
IN_COLS_PACKED = 4224
MISC_W = 128


def _cparams(sem, vmem_mb):
    return pltpu.CompilerParams(dimension_semantics=sem, vmem_limit_bytes=vmem_mb << 20)


def _norm_mod(x, nw, shift, scale):
    ms = jnp.mean(x * x, axis=-1, keepdims=True)
    y = (x * lax.rsqrt(ms + EPS)) * nw
    return (y * (1.0 + scale) + shift).astype(BF16)


def _log_sigmoid(x):
    return jnp.minimum(x, 0.0) - jnp.log1p(jnp.exp(-jnp.abs(x)))


def _softplus(x):
    return jnp.maximum(x, 0.0) + jnp.log1p(jnp.exp(-jnp.abs(x)))


def _ada_kernel(c_ref, w_ref, b_ref, o_ref):
    c = c_ref[...]
    act = (c * jax.nn.sigmoid(c)).astype(BF16)
    o_ref[...] = jnp.dot(act, w_ref[...].astype(BF16), preferred_element_type=F32) + b_ref[...]


def _ada_all(c, c_ctx, ada_w, ada_b):
    depth, d, n6 = ada_w.shape
    nb = c.shape[0]
    rows = nb + 8
    cc = jnp.concatenate([c.astype(F32), jnp.broadcast_to(c_ctx.astype(F32)[None], (8, d))], axis=0)
    tn = 1024
    out = pl.pallas_call(
        _ada_kernel,
        grid=(depth, n6 // tn),
        in_specs=[pl.BlockSpec((rows, d), lambda l, n: (0, 0)),
                  pl.BlockSpec((None, d, tn), lambda l, n: (l, 0, n)),
                  pl.BlockSpec((None, 1, tn), lambda l, n: (l, 0, n))],
        out_specs=pl.BlockSpec((None, rows, tn), lambda l, n: (l, 0, n)),
        out_shape=jax.ShapeDtypeStruct((depth, rows, n6), F32),
        compiler_params=_cparams(("parallel", "parallel"), 40),
        name="ada_mod",
    )(cc, ada_w, ada_b.reshape(depth, 1, n6))
    return out.reshape(depth, rows, 6, d)


def _tile_rows(x_refs, n_lat_tiles):
    if len(x_refs) == 1:
        return x_refs[0][...]
    return jnp.where(pl.program_id(1) >= n_lat_tiles, x_refs[1][...], x_refs[0][...])


def _win_kernel(*refs, n_src, n_lat_tiles):
    x_refs, refs = refs[:n_src], refs[n_src:]
    (mod_ref, nw_ref, w_ref, wgt_ref, wvt_ref, cos_ref, sin_ref, qan_ref, wq_ref, kvan_ref, wkv_ref, qn_ref, kn_ref,
     u_ref, mq_ref, mk_ref, mvt_ref, mo_ref, lx_ref, lg_ref, misc_ref, mgt_ref, qp_ref, kp_ref, v_ref) = refs
    g, tm, d = x_refs[0].shape
    h = _norm_mod(_tile_rows(x_refs, n_lat_tiles), nw_ref[...], mod_ref[:, 0:1, :], mod_ref[:, 1:2, :])
    h = h.reshape(g * tm, d)
    nt_dims = (((1,), (1,)), ((), ()))

    def mm(lo, width):
        return jnp.dot(h, w_ref[:, lo:lo + width], preferred_element_type=F32)

    def store(ref, lo, scale=None):
        def run():
            y = mm(lo, ref.shape[-1])
            ref[...] = (y if scale is None else y * scale).astype(ref.dtype).reshape(ref.shape)
        return run

    def values_t():
        mvt = lax.dot_general(wvt_ref[...], h, nt_dims, preferred_element_type=F32).astype(BF16)
        for bi in range(g):
            for cc in range(mvt_ref.shape[1]):
                lo = bi * tm + cc * ML_CHUNK
                mvt_ref[bi, cc] = mvt[:, lo:lo + ML_CHUNK]

    def gates_t():
        mgt = lax.dot_general(wgt_ref[...], h, nt_dims, preferred_element_type=F32)
        for bi in range(g):
            mgt_ref[bi] = mgt[:, bi * tm:(bi + 1) * tm]

    misc = mm(COL_MISC, MISC_W)
    misc_ref[...] = misc.reshape(misc_ref.shape)
    prep = _mla_prep_parts(mm(COL_CQ, MLA_Q_LORA), mm(COL_CKV, MLA_KV_LORA), misc, cos_ref, sin_ref, qan_ref,
                           wq_ref, kvan_ref, wkv_ref, qn_ref, kn_ref, qp_ref, kp_ref, v_ref)
    proj = [store(u_ref, COL_U), store(mq_ref, COL_MQ), store(mk_ref, COL_MK, ML_DH ** -0.5), values_t,
            store(mo_ref, COL_MO), store(lx_ref, COL_LX), store(lg_ref, COL_LG), gates_t]
    for k in range(max(len(prep), len(proj))):
        if k < len(proj):
            proj[k]()
        if k < len(prep):
            prep[k]()


def _residual_specs(src, tm, n_lat_tiles, g=None):
    D = src[0].shape[-1]
    if len(src) == 1:
        return [pl.BlockSpec((g, tm, D), lambda b, t: (b, t, 0))]
    return [pl.BlockSpec((g, tm, D), lambda b, t: (b, jnp.minimum(t, n_lat_tiles - 1), 0)),
            pl.BlockSpec((g, tm, D), lambda b, t: (b, jnp.maximum(t - n_lat_tiles, 0), 0),
                         pipeline_mode=pl.Buffered(1))]


def _win(src, mod_l, nw, w_p, wgt, wvt, mla_w, tables, layer, n_lat_tiles):
    B, _, D = src[0].shape
    Lt = sum(a.shape[1] for a in src)
    tm, g = TOKEN_TILE, ROW_BATCHES
    nt = Lt // tm
    hp = MLA_HEADS * MLA_PAD
    bm = lambda b, t: (b, t, 0)
    per_layer = lambda *shape: pl.BlockSpec((None,) + shape, lambda b, t: (layer,) + (0,) * len(shape))
    table = pl.BlockSpec((tm, 128), lambda b, t: (t, 0))
    in_specs = _residual_specs(src, tm, n_lat_tiles, g) + [
        pl.BlockSpec((g, 6, D), lambda b, t: (jnp.where(t >= n_lat_tiles, B // g, b), 0, 0)),
        pl.BlockSpec((1, D), lambda b, t: (0, 0)),
        pl.BlockSpec((None, D, IN_COLS_PACKED), lambda b, t: (layer, 0, 0), pipeline_mode=pl.Buffered(1)),
        per_layer(16, D), per_layer(GROUP_W, D), table, table,
        per_layer(1, MLA_Q_LORA), per_layer(MLA_Q_LORA, hp),
        per_layer(1, MLA_KV_LORA), per_layer(MLA_KV_LORA, MLA_HEADS * (MLA_NOPE + MLA_V)),
        per_layer(1, MLA_PAD), per_layer(1, MLA_PAD),
    ]
    cpt = tm // ML_CHUNK
    out_specs = [
        pl.BlockSpec((g, tm, GROUP_W), bm),
        pl.BlockSpec((g, tm, GROUP_W), bm),
        pl.BlockSpec((g, tm, GROUP_W), bm),
        pl.BlockSpec((g, cpt, GROUP_W, ML_CHUNK), lambda b, t: (b, t, 0, 0)),
        pl.BlockSpec((g, tm, GROUP_W), bm),
        pl.BlockSpec((g, tm, GROUP_W), bm),
        pl.BlockSpec((g, tm, GROUP_W), bm),
        pl.BlockSpec((g, tm, MISC_W), bm),
        pl.BlockSpec((g, 16, tm), lambda b, t: (b, 0, t)),
        pl.BlockSpec((g, tm, hp), bm),
        pl.BlockSpec((g, tm, hp), bm),
        pl.BlockSpec((g, tm, GROUP_W), bm),
    ]
    sds = jax.ShapeDtypeStruct
    out_shape = [
        sds((B, Lt, GROUP_W), F32),
        sds((B, Lt, GROUP_W), BF16), sds((B, Lt, GROUP_W), BF16),
        sds((B, Lt // ML_CHUNK, GROUP_W, ML_CHUNK), BF16),
        sds((B, Lt, GROUP_W), F32),
        sds((B, Lt, GROUP_W), F32), sds((B, Lt, GROUP_W), F32),
        sds((B, Lt, MISC_W), F32),
        sds((B, 16, Lt), F32),
        sds((B, Lt, hp), BF16), sds((B, Lt, hp), BF16), sds((B, Lt, GROUP_W), BF16),
    ]
    return pl.pallas_call(
        functools.partial(_win_kernel, n_src=len(src), n_lat_tiles=n_lat_tiles),
        grid=(B // g, nt), in_specs=in_specs, out_specs=out_specs, out_shape=out_shape,
        compiler_params=_cparams(("parallel", "parallel"), 58), name="in_proj",
    )(*src, mod_l, nw.reshape(1, D), w_p, wgt, wvt, tables[0], tables[1], *mla_w)


def _rec_kernel(*refs, tc, nb, nlc, nc, reverse, final):
    if final:
        (ub_ref, yp_ref, wd_ref, a_ref, wc_ref, dsk_ref, gw_ref, gb_ref,
         x_ref, xp_ref, xn_ref, hp_ref, g_ref, cw_ref, cb_ref, w_ref, bias_ref, lam_ref,
         oa_ref, or_ref, u_ref, buf, st, xpad, abuf, bbuf, lst, ybuf, o3) = refs
    else:
        (ub_ref, wd_ref, a_ref, wc_ref,
         x_ref, xp_ref, xn_ref, cw_ref, cb_ref, w_ref, bias_ref, lam_ref,
         oa_ref, or_ref, u_ref, buf, st, xpad, abuf, bbuf, lst) = refs
    i = pl.program_id(0)
    c = (nc - 1 - i) if reverse else lax.rem(i + nlc, nc)
    R = tc * nb
    RB = min(R, 512)
    RL = min(R, 256)
    NS = S5_STRIP_STATES
    nh = nb // 8
    left = LRU_CONV // 2
    halo = xp_ref.shape[1]

    @pl.when(i == 0)
    def _():
        st[...] = jnp.zeros_like(st)
        lst[...] = jnp.zeros_like(lst)

    starts = jnp.logical_or(c == 0, c == nlc)
    ends = jnp.logical_or(c == nlc - 1, c == nc - 1)
    for j in range(left):
        xpad[j * nb:(j + 1) * nb, :] = jnp.where(starts, 0.0, xp_ref[:, halo - left + j, :])
    for t in range(tc):
        u_ref[t * nb:(t + 1) * nb, :] = ub_ref[:, t, :]
        xpad[(left + t) * nb:(left + t + 1) * nb, :] = x_ref[:, t, :]
    for j in range(LRU_CONV - 1 - left):
        xpad[(left + tc + j) * nb:(left + tc + j + 1) * nb, :] = jnp.where(ends, 0.0, xn_ref[:, j, :])
    sp = _softplus(-lam_ref[...])

    def lru_gate_parts(rb):
        r0 = rb * RL
        cache = {}

        def conv():
            if "xs" not in cache:
                xs = cb_ref[...]
                for j in range(LRU_CONV):
                    xs = xs + xpad[r0 + j * nb:r0 + j * nb + RL, :] * cw_ref[j:j + 1, :]
                cache["xs"] = xs
            return cache["xs"]

        def block(n):
            cs = slice(n * LRU_BLOCK_W, (n + 1) * LRU_BLOCK_W)
            xs = conv()[:, cs]
            z = jnp.dot(xs.astype(BF16), w_ref[n], preferred_element_type=F32)
            r = jax.nn.sigmoid(z[:, :LRU_BLOCK_W] + bias_ref[0:1, cs])
            ig = jax.nn.sigmoid(z[:, LRU_BLOCK_W:] + bias_ref[1:2, cs])
            log_a = (-LRU_C * r) * sp[:, cs]
            abuf[r0:r0 + RL, cs] = jnp.exp(log_a)
            th = jnp.tanh(log_a)
            bbuf[r0:r0 + RL, cs] = jnp.sqrt(-2.0 * th / (1.0 - th)) * (ig * xs)

        return [functools.partial(block, n) for n in range(LRU_BLOCKS)]

    def s5_drive_parts(s):
        cs = slice(s * 128, (s + 1) * 128)

        def part(rb):
            rs = slice(rb * RB, (rb + 1) * RB)
            buf[s % 2, rs, :] = jnp.dot(u_ref[rs, cs].astype(BF16), wd_ref[s], preferred_element_type=F32)

        return [functools.partial(part, rb) for rb in range(R // RB)]

    def s5_scan(s):
        sb = buf.at[s % 2]
        ar = jnp.broadcast_to(a_ref[s, 0:1, :], (8, NS))
        ai = jnp.broadcast_to(a_ref[s, 1:2, :], (8, NS))
        init = []
        for hh in range(nh):
            init += [st[s, hh * 8:(hh + 1) * 8, 0:NS], st[s, hh * 8:(hh + 1) * 8, NS:2 * NS]]

        def body(k, carry):
            t = (tc - 1 - k) if reverse else k
            out = []
            for hh in range(nh):
                sr, si = carry[2 * hh], carry[2 * hh + 1]
                row = pl.multiple_of(t * nb + hh * 8, 8)
                br = sb[pl.ds(row, 8), 0:NS]
                bi = sb[pl.ds(row, 8), NS:2 * NS]
                nsr = ar * sr - ai * si + br
                nsi = ar * si + ai * sr + bi
                sb[pl.ds(row, 8), 0:NS] = nsr
                sb[pl.ds(row, 8), NS:2 * NS] = nsi
                out += [nsr, nsi]
            return tuple(out)

        fin = lax.fori_loop(0, tc, body, tuple(init), unroll=2)
        for hh in range(nh):
            st[s, hh * 8:(hh + 1) * 8, 0:NS] = fin[2 * hh]
            st[s, hh * 8:(hh + 1) * 8, NS:2 * NS] = fin[2 * hh + 1]

    def s5_readout_parts(s):
        cs = slice(s * 128, (s + 1) * 128)

        def part(rb):
            rs = slice(rb * RB, (rb + 1) * RB)
            y = jnp.dot(buf[s % 2, rs, :].astype(BF16), wc_ref[s], preferred_element_type=F32)
            if final:
                ybuf[rs, cs] = y
            else:
                oa_ref[rs, cs] = y

        return [functools.partial(part, rb) for rb in range(R // RB)]

    def emit_interleaved(mxu_parts, vpu_parts):
        n = max(len(mxu_parts), len(vpu_parts))
        for k in range(n):
            if k < len(mxu_parts):
                mxu_parts[k]()
            if k < len(vpu_parts):
                vpu_parts[k]()

    n_gate_blocks = R // RL
    for s in range(S5_STRIPS + 1):
        mxu = (s5_readout_parts(s - 1) if s > 0 else []) + (s5_drive_parts(s) if s < S5_STRIPS else [])
        vpu = []
        for rb in range(n_gate_blocks):
            if rb % S5_STRIPS == s:
                vpu += lru_gate_parts(rb)
        emit_interleaved(mxu, vpu)
        if s < S5_STRIPS:
            s5_scan(s)

    init = tuple(lst[hh * 8:(hh + 1) * 8, :] for hh in range(nh))

    def lru_body(k, carry):
        t = (tc - 1 - k) if reverse else k
        out = []
        for hh in range(nh):
            row = pl.multiple_of(t * nb + hh * 8, 8)
            hnew = abuf[pl.ds(row, 8), :] * carry[hh] + bbuf[pl.ds(row, 8), :]
            bbuf[pl.ds(row, 8), :] = hnew
            out.append(hnew)
        return tuple(out)

    fin = lax.fori_loop(0, tc, lru_body, init, unroll=4)
    for hh in range(nh):
        lst[hh * 8:(hh + 1) * 8, :] = fin[hh]

    if not final:
        or_ref[...] = bbuf[...]
        return
    for rb in range(R // RB):
        rs = slice(rb * RB, (rb + 1) * RB)
        y = ybuf[rs, :] + yp_ref[rs, :] + dsk_ref[...] * u_ref[rs, :]
        g = jax.nn.gelu(y)
        z = jnp.dot(g.astype(BF16), gw_ref[...], preferred_element_type=F32) + gb_ref[...]
        res = g * jax.nn.sigmoid(z)
        for tt in range(RB // nb):
            o3[:, rb * (RB // nb) + tt, :] = res[tt * nb:(tt + 1) * nb, :]
    oa_ref[...] = o3[...].astype(BF16)
    for t in range(tc):
        rows = slice(t * nb, (t + 1) * nb)
        o3[:, t, :] = hp_ref[rows, :] + bbuf[rows, :]
    or_ref[...] = (o3[...] * jax.nn.gelu(g_ref[...])).astype(BF16)


def _chunk_order(nc, nlc, reverse):
    if reverse:
        return lambda i: nc - 1 - i
    return lambda i: lax.rem(i + nlc, nc)


def _recurrent(u, lx, lg, s5p, d_skip, glu_w, glu_b, conv_w, conv_b, wab, bias, lam, n_lat):
    nb, lt, _ = u.shape
    rows = nb * lt
    tc = S5_CHUNK
    R = tc * nb
    nc, nlc = lt // tc, n_lat // tc
    halo = 8
    wd, a, wc = s5p
    fwd = None
    for d in range(2):
        final = d == 1
        cidx = _chunk_order(nc, nlc, reverse=final)
        row_spec = pl.BlockSpec((R, GROUP_W), lambda i: (cidx(i), 0))
        bt_spec = pl.BlockSpec((nb, tc, GROUP_W), lambda i: (0, cidx(i), 0))
        prev_spec = pl.BlockSpec((nb, halo, GROUP_W),
                                 lambda i: (0, jnp.maximum(cidx(i) * (tc // halo) - 1, 0), 0))
        next_spec = pl.BlockSpec((nb, halo, GROUP_W),
                                 lambda i: (0, jnp.minimum((cidx(i) + 1) * (tc // halo), lt // halo - 1), 0))
        full = lambda shape: pl.BlockSpec(shape, lambda i: (0,) * len(shape))
        s5_w = [full((S5_STRIPS, 128, 2 * S5_STRIP_STATES)), full((S5_STRIPS, 2, S5_STRIP_STATES)),
                full((S5_STRIPS, 2 * S5_STRIP_STATES, 128))]
        lru_w = [full((LRU_CONV, GROUP_W)), full((1, GROUP_W)), full((LRU_BLOCKS, LRU_BLOCK_W, 2 * LRU_BLOCK_W)),
                 full((2, GROUP_W)), full((1, GROUP_W))]
        lru_args = [conv_w, conv_b.reshape(1, GROUP_W), wab[d], bias[d], lam[d].reshape(1, GROUP_W)]
        if final:
            in_specs = ([bt_spec, row_spec] + s5_w + [full((1, GROUP_W)), full((GROUP_W, GROUP_W)), full((1, GROUP_W))]
                        + [bt_spec, prev_spec, next_spec, row_spec, bt_spec] + lru_w)
            args = ([u, fwd[0], wd[d], a[d], wc[d], d_skip.reshape(1, GROUP_W), glu_w, glu_b.reshape(1, GROUP_W)]
                    + [lx, lx, lx, fwd[1], lg] + lru_args)
            out_specs = [bt_spec, bt_spec]
            out_shape = [jax.ShapeDtypeStruct((nb, lt, GROUP_W), BF16)] * 2
        else:
            in_specs = [bt_spec] + s5_w + [bt_spec, prev_spec, next_spec] + lru_w
            args = [u, wd[d], a[d], wc[d], lx, lx, lx] + lru_args
            out_specs = [row_spec, row_spec]
            out_shape = [jax.ShapeDtypeStruct((rows, GROUP_W), F32)] * 2
        scratch = [pltpu.VMEM((R, GROUP_W), F32), pltpu.VMEM((2, R, 2 * S5_STRIP_STATES), F32),
                   pltpu.VMEM((S5_STRIPS, nb, 2 * S5_STRIP_STATES), F32),
                   pltpu.VMEM((R + 3 * nb, GROUP_W), F32), pltpu.VMEM((R, GROUP_W), F32),
                   pltpu.VMEM((R, GROUP_W), F32), pltpu.VMEM((nb, GROUP_W), F32)]
        if final:
            scratch += [pltpu.VMEM((R, GROUP_W), F32), pltpu.VMEM((nb, tc, GROUP_W), F32)]
        fwd = pl.pallas_call(
            functools.partial(_rec_kernel, tc=tc, nb=nb, nlc=nlc, nc=nc, reverse=final, final=final),
            grid=(nc,), in_specs=in_specs, out_specs=out_specs, out_shape=out_shape,
            scratch_shapes=scratch,
            compiler_params=_cparams(("arbitrary",), 56), name="rec_rev" if final else "rec_fwd",
        )(*args)
    return fwd


def _s5_pole_kernel(lr_ref, li_ref, ldt_ref, ar_ref, ai_ref, fr_ref, fi_ref):
    lr = jnp.minimum(lr_ref[...], S5_MIN_NEG)
    li = li_ref[...]
    dt = jnp.exp(ldt_ref[...])
    mag = jnp.exp(lr * dt)
    ar, ai = mag * jnp.cos(li * dt), mag * jnp.sin(li * dt)
    den = lr * lr + li * li
    ar_ref[...] = ar
    ai_ref[...] = ai
    fr_ref[...] = ((ar - 1.0) * lr + ai * li) / den
    fi_ref[...] = (ai * lr - (ar - 1.0) * li) / den


def _s5_input_kernel(fr_ref, fi_ref, bre_ref, bim_ref, bbr_ref, bbi_ref):
    fr, fi = fr_ref[...], fi_ref[...]
    bbr_ref[...] = fr * bre_ref[...] - fi * bim_ref[...]
    bbi_ref[...] = fr * bim_ref[...] + fi * bre_ref[...]


def _s5_discretise(lam_re, lam_im, log_dt, b_re, b_im):
    shape = lam_re.shape
    rows, p = lam_re.size // shape[-1], shape[-1]
    flat = lambda t: t.astype(F32).reshape(rows, p)
    ldt = jnp.broadcast_to(log_dt.astype(F32)[..., None], shape)
    ar, ai, fr, fi = pl.pallas_call(
        _s5_pole_kernel, out_shape=[jax.ShapeDtypeStruct((rows, p), F32)] * 4, name="s5_poles",
    )(flat(lam_re), flat(lam_im), flat(ldt))
    n, rb = lam_re.size, 2048
    cspec = pl.BlockSpec((rb, 1), lambda i: (i, 0))
    bspec = pl.BlockSpec((rb, S5_CH), lambda i: (i, 0))
    bbr, bbi = pl.pallas_call(
        _s5_input_kernel, grid=(n // rb,), in_specs=[cspec, cspec, bspec, bspec], out_specs=[bspec, bspec],
        out_shape=[jax.ShapeDtypeStruct((n, S5_CH), F32)] * 2,
        compiler_params=_cparams(("parallel",), 32), name="s5_input_matrix",
    )(fr.reshape(n, 1), fi.reshape(n, 1), b_re.astype(F32).reshape(n, S5_CH), b_im.astype(F32).reshape(n, S5_CH))
    return ar.reshape(shape), ai.reshape(shape), bbr.reshape(b_re.shape), bbi.reshape(b_re.shape)


def _s5_params(ar, ai, bbr, bbi, c_re, c_im):
    gs = 128 // S5_CH
    eye = jnp.eye(gs, dtype=F32)

    def drive(bb):
        t = bb.reshape(2, S5_STRIPS, gs, S5_STATE, S5_CH)
        return jnp.einsum('dsgpc,gh->dsgchp', t, eye).reshape(2, S5_STRIPS, 128, S5_STRIP_STATES)

    def read(cc):
        t = cc.reshape(2, S5_STRIPS, gs, S5_CH, S5_STATE)
        return jnp.einsum('dsgcp,gh->dsgphc', t, eye).reshape(2, S5_STRIPS, S5_STRIP_STATES, 128)

    wd = jnp.concatenate([drive(bbr), drive(bbi)], axis=-1).astype(BF16)
    wc = jnp.concatenate([read(c_re.astype(F32)), read(-c_im.astype(F32))], axis=-2).astype(BF16)
    a = jnp.stack([ar.reshape(2, S5_STRIPS, S5_STRIP_STATES), ai.reshape(2, S5_STRIPS, S5_STRIP_STATES)], axis=2)
    return wd, a, wc


def _mlstm_kernel(q_ref, k_ref, vt_ref, o_ref, gr_ref, gc_ref, br_ref, bc_ref, nw_ref, out_ref,
                  xb_scr, row_scr, nt_scr, u_scr, vec_scr, *, nch, nlat, hp):
    T = ML_CHUNK
    nt_dims = (((1,), (1,)), ((), ()))
    rid = lax.broadcasted_iota(jnp.int32, (T, T), 0)
    cid = lax.broadcasted_iota(jnp.int32, (T, T), 1)
    incl_rows = (rid <= cid).astype(F32)
    incl_cols = (rid >= cid).astype(F32)

    for j in range(hp):
        g = (gr_ref[j] + br_ref[j]).reshape(nch * 8, T)
        kind = lax.broadcasted_iota(jnp.int32, (nch * 8, T), 0) % 8
        lf = _log_sigmoid(g)
        pre = jnp.dot(lf, incl_rows, precision=HIGHEST, preferred_element_type=F32)
        b = jnp.where(kind == 1, pre[:, T - 1:T] - pre + lf, pre)
        x = g - pltpu.roll(b, 2, axis=0)
        row_scr[j] = jnp.where(kind < 2, b, x).reshape(nch, 8, T)
        gl = gc_ref[j] + bc_ref[j]
        kind = lax.broadcasted_iota(jnp.int32, (T, 128), 1) % 4
        lf = _log_sigmoid(gl)
        pre = jnp.dot(incl_cols, lf, precision=HIGHEST, preferred_element_type=F32)
        b = jnp.where(kind == 1, pre[T - 1:T, :] - pre + lf, pre)
        x = gl - pltpu.roll(b, 2, axis=1)
        for c in range(nch):
            for d in range(2):
                lane = c * 4 + 2 + d
                xb_scr[j, d, c] = jnp.broadcast_to(x[:, lane:lane + 1], (T, T))

    def independent(c, carry):
        rows = pl.ds(pl.multiple_of(c * T, T), T)
        pairs = [(j, d) for j in range(hp) for d in range(2)]
        hs = lambda j: slice(j * ML_DH, (j + 1) * ML_DH)
        swept = lambda d: (rid <= cid) if d == 0 else (rid >= cid)
        kq = {j: lax.dot_general(k_ref[rows, hs(j)], q_ref[rows, hs(j)], nt_dims, preferred_element_type=F32)
              for j in range(hp)}
        xm = {p: jnp.where(swept(p[1]), xb_scr[p[0], p[1], c], -jnp.inf) for p in pairs}
        a_row = {p: jnp.max(xm[p], axis=0, keepdims=True) for p in pairs}
        w0 = {}
        for j, d in pairs:
            a_last = a_row[j, d][:, T - 1:T] if d == 0 else a_row[j, d][:, 0:1]
            w0[j, d] = jnp.exp(row_scr[j, c, 2 + d:3 + d, :] - a_last)
        s0 = {p: kq[p[0]] * jnp.exp(xm[p] - a_row[p]) for p in pairs}
        for j, d in pairs:
            vt, kc = vt_ref[c, hs(j), :], k_ref[rows, hs(j)]
            nt_scr[j, d, c] = jnp.dot(vt, s0[j, d].astype(BF16), preferred_element_type=F32)
            wv = (vt.astype(F32) * w0[j, d]).astype(BF16)
            u_scr[j, d, c] = jnp.dot(wv, kc, preferred_element_type=F32)
            vec_scr[j, d, c, 0:1, :] = a_row[j, d]
            vec_scr[j, d, c, 1:2, :] = jnp.sum(s0[j, d], axis=0, keepdims=True)
            vec_scr[j, d, c, 2:3, :] = jnp.dot(jnp.broadcast_to(w0[j, d], (8, T)).astype(BF16), kc,
                                               preferred_element_type=F32)[0:1, :]
        return carry

    lax.fori_loop(0, nch, independent, 0)

    def pass2(c, d, j, state):
        cm, n, m = state
        hs = slice(j * ML_DH, (j + 1) * ML_DH)
        qc = q_ref[pl.ds(pl.multiple_of(c * T, T), T), hs]
        a_row, d0, n0 = vec_scr[j, d, c, 0:1, :], vec_scr[j, d, c, 1:2, :], vec_scr[j, d, c, 2:3, :]
        b_row = row_scr[j, c, d:d + 1, :]
        mu = jnp.maximum(m, a_row)
        inter, r = jnp.exp(m - mu), jnp.exp(a_row - mu)
        qct = lax.dot_general(cm.astype(BF16), qc, nt_dims, preferred_element_type=F32)
        qn = lax.dot_general(jnp.broadcast_to(n, (8, ML_DH)).astype(BF16), qc, nt_dims,
                             preferred_element_type=F32)[0:1, :]
        den = inter * qn + r * d0
        inv = 1.0 / jnp.maximum(jnp.abs(den), jnp.exp(-b_row - mu))
        nt_scr[j, d, c] = (inter * qct + r * nt_scr[j, d, c]) * inv
        a_last = a_row[:, T - 1:T] if d == 0 else a_row[:, 0:1]
        b_last = b_row[:, T - 1:T] if d == 0 else b_row[:, 0:1]
        mul = jnp.maximum(m, a_last)
        decay, rl = jnp.exp(m - mul), jnp.exp(a_last - mul)
        return decay * cm + rl * u_scr[j, d, c], decay * n + rl * n0, b_last + mul

    def states(i, carry):
        out = []
        for j in range(hp):
            out.append(pass2(lax.rem(i + nlat, nch), 0, j, carry[2 * j]))
            out.append(pass2(nch - 1 - i, 1, j, carry[2 * j + 1]))
        return tuple(out)

    zero = (jnp.zeros((ML_DH, ML_DH), F32), jnp.zeros((1, ML_DH), F32), jnp.zeros((1, 1), F32))
    lax.fori_loop(0, nch, states, (zero,) * (2 * hp))

    for c in range(nch):
        rows = slice(c * T, (c + 1) * T)
        for j in range(hp):
            hs = slice(j * ML_DH, (j + 1) * ML_DH)
            ht = nt_scr[j, 0, c] + nt_scr[j, 1, c]
            hn = ht * lax.rsqrt(jnp.mean(ht * ht, axis=0, keepdims=True) + EPS) * nw_ref[j]
            out_ref[rows, hs] = (hn.T * jax.nn.sigmoid(o_ref[rows, hs])).astype(BF16)


def _mlstm(mq, mk, mvt, mo, misc, mgt, ig_bias, fg_bias, out_norm, n_lat):
    B, Lt, _ = mq.shape
    T = ML_CHUNK
    nch, nlat = Lt // T, n_lat // T
    hp = ML_HEADS_PER_STEP
    assert 4 * nch <= 128
    gr = jnp.transpose(mgt.reshape(B, ML_HEADS, 4, nch, T), (0, 1, 3, 2, 4))
    gr = jnp.pad(gr, ((0, 0), (0, 0), (0, 0), (0, 4), (0, 0)))
    gc = misc[:, :, MLA_ROPE:MLA_ROPE + 16].reshape(B, nch, T, ML_HEADS, 4)
    gc = jnp.transpose(gc, (0, 3, 2, 1, 4)).reshape(B, ML_HEADS, T, nch * 4)
    gc = jnp.pad(gc, ((0, 0), (0, 0), (0, 0), (0, 128 - nch * 4)))
    kinds = jnp.concatenate([fg_bias.astype(F32), ig_bias.astype(F32)], axis=0).T
    br = jnp.broadcast_to(jnp.pad(kinds, ((0, 0), (0, 4)))[:, :, None], (ML_HEADS, 8, T))
    bc = jnp.pad(jnp.tile(kinds, (1, nch)), ((0, 0), (0, 128 - nch * 4)))[:, None, :]
    nwb = jnp.broadcast_to(out_norm.astype(F32)[:, :, None], (ML_HEADS, ML_DH, T))
    head_spec = pl.BlockSpec((None, Lt, hp * ML_DH), lambda b, h: (b, 0, h))
    slab = lambda: pltpu.VMEM((hp, 2, nch, T, T), F32)
    return pl.pallas_call(
        functools.partial(_mlstm_kernel, nch=nch, nlat=nlat, hp=hp),
        grid=(B, ML_HEADS // hp),
        in_specs=[head_spec, head_spec,
                  pl.BlockSpec((None, nch, hp * ML_DH, T), lambda b, h: (b, 0, h, 0)),
                  head_spec,
                  pl.BlockSpec((None, hp, nch, 8, T), lambda b, h: (b, h, 0, 0, 0)),
                  pl.BlockSpec((None, hp, T, 128), lambda b, h: (b, h, 0, 0)),
                  pl.BlockSpec((hp, 8, T), lambda b, h: (h, 0, 0)),
                  pl.BlockSpec((hp, 1, 128), lambda b, h: (h, 0, 0)),
                  pl.BlockSpec((hp, ML_DH, T), lambda b, h: (h, 0, 0))],
        out_specs=head_spec,
        out_shape=jax.ShapeDtypeStruct((B, Lt, GROUP_W), BF16),
        scratch_shapes=[slab(), pltpu.VMEM((hp, nch, 8, T), F32), slab(), slab(),
                        pltpu.VMEM((hp, 2, nch, 8, T), F32)],
        compiler_params=_cparams(("parallel", "parallel"), 60), name="mlstm",
    )(mq, mk, mvt, mo, gr, gc, br, bc, nwb)


def _mla_prep_parts(cq, ckv, misc, cos_ref, sin_ref, qan_ref, wq_ref, kvan_ref, wkv_ref, qn_ref, kn_ref,
                    qp_ref, kp_ref, v_ref):
    def rms(x, w):
        return (x * lax.rsqrt(jnp.mean(x * x, axis=-1, keepdims=True) + EPS)) * w

    rows = cq.shape[0]
    g = rows // cos_ref.shape[0]
    lane = lax.broadcasted_iota(jnp.int32, (rows, 128), 1)
    first = (lane % (MLA_ROPE // 2)) < (MLA_ROPE // 4)
    hw = MLA_HEADS * MLA_NOPE
    vals = {}

    def put(ref, lo, val):
        ref[:, :, lo:lo + val.shape[-1]] = val.astype(BF16).reshape(ref.shape[0], ref.shape[1], val.shape[-1])

    def rope(x):
        sw = jnp.where(first, pltpu.roll(x, 128 - MLA_ROPE // 4, axis=1), pltpu.roll(x, MLA_ROPE // 4, axis=1))
        return x * vals["cos"] + sw * vals["sin"]

    def project():
        vals["cos"] = jnp.concatenate([cos_ref[...]] * g, axis=0)
        vals["sin"] = jnp.concatenate([sin_ref[...]] * g, axis=0)
        vals["q"] = jnp.dot(rms(cq, qan_ref[...]).astype(BF16), wq_ref[...], preferred_element_type=F32)
        kv = jnp.dot(rms(ckv, kvan_ref[...]).astype(BF16), wkv_ref[...], preferred_element_type=F32)
        vals["k"] = kv[:, :hw]
        put(v_ref, 0, kv[:, hw:])
        kr = jnp.where(lane < MLA_ROPE, misc, 0.0)
        vals["kr_sq"] = jnp.sum(kr * kr, axis=-1, keepdims=True)
        vals["kr_rot"] = rope(kr * kn_ref[:, 128:256])

    def head(h):
        base = h * MLA_PAD
        qa, qb = vals["q"][:, base:base + 128], vals["q"][:, base + 128:base + 256]
        inv = lax.rsqrt(jnp.sum(qa * qa + qb * qb, axis=-1, keepdims=True) / MLA_QK + EPS)
        put(qp_ref, base, (qa * inv) * (qn_ref[:, 0:128] * ATTN_SCALE))
        put(qp_ref, base + 128, rope((qb * inv) * (qn_ref[:, 128:256] * ATTN_SCALE)))
        ka = vals["k"][:, h * MLA_NOPE:(h + 1) * MLA_NOPE]
        inv = lax.rsqrt((jnp.sum(ka * ka, axis=-1, keepdims=True) + vals["kr_sq"]) / MLA_QK + EPS)
        put(kp_ref, base, (ka * inv) * kn_ref[:, 0:128])
        put(kp_ref, base + 128, vals["kr_rot"] * inv)

    return [project] + [functools.partial(head, h) for h in range(MLA_HEADS)]


def _attn_kernel(q_ref, k_ref, v_ref, o_ref, *, n_lat, n_lat_tiles, sub):
    def attend(n_sub, k, v):
        for first in range(0, n_sub, 2):
            blocks = [slice(i * sub, (i + 1) * sub) for i in range(first, min(first + 2, n_sub))]
            s = [lax.dot_general(q_ref[r, :], k, (((1,), (1,)), ((), ())), preferred_element_type=F32)
                 for r in blocks]
            p = [jnp.exp(si - jnp.max(si, axis=-1, keepdims=True)) for si in s]
            for r, pi in zip(blocks, p):
                den = jnp.sum(pi, axis=-1, keepdims=True)
                o_ref[r, :] = (jnp.dot(pi.astype(BF16), v, preferred_element_type=F32) / den).astype(BF16)

    qi = pl.program_id(2)

    @pl.when(qi < n_lat_tiles)
    def _():
        attend(q_ref.shape[0] // sub, k_ref[...], v_ref[...])

    @pl.when(qi >= n_lat_tiles)
    def _():
        attend((k_ref.shape[0] - n_lat) // sub, k_ref[n_lat:, :], v_ref[n_lat:, :])


def _rope_tables(n_lat, n_ctx):
    n_rows = n_lat // GRID_W
    rows = jnp.repeat(jnp.arange(n_rows, dtype=F32), GRID_W)
    cols = jnp.tile(jnp.arange(GRID_W, dtype=F32), n_rows)
    n_freq = MLA_ROPE // 4
    inv_freq = ROPE_BASE ** (-jnp.arange(n_freq, dtype=F32) / n_freq)
    ar, ac = rows[:, None] * inv_freq, cols[:, None] * inv_freq
    cos = jnp.concatenate([jnp.cos(ar), jnp.cos(ar), jnp.cos(ac), jnp.cos(ac)], axis=1)
    sins = jnp.concatenate([-jnp.sin(ar), jnp.sin(ar), -jnp.sin(ac), jnp.sin(ac)], axis=1)
    cos = jnp.pad(cos, ((0, n_ctx), (0, 128 - MLA_ROPE)), constant_values=1.0)
    sins = jnp.pad(sins, ((0, n_ctx), (0, 128 - MLA_ROPE)))
    return cos, sins


def _mla_attn(qp, kp, v, n_lat):
    B, Lt, _ = v.shape
    tq = min(Q_TILE, n_lat)
    return pl.pallas_call(
        functools.partial(_attn_kernel, n_lat=n_lat, n_lat_tiles=n_lat // tq, sub=Q_SUB),
        grid=(B, MLA_HEADS, pl.cdiv(Lt, tq)),
        in_specs=[pl.BlockSpec((None, tq, MLA_PAD), lambda b, h, t: (b, t, h)),
                  pl.BlockSpec((None, Lt, MLA_PAD), lambda b, h, t: (b, 0, h)),
                  pl.BlockSpec((None, Lt, MLA_V), lambda b, h, t: (b, 0, h))],
        out_specs=pl.BlockSpec((None, tq, MLA_V), lambda b, h, t: (b, t, h)),
        out_shape=jax.ShapeDtypeStruct((B, Lt, GROUP_W), BF16),
        compiler_params=_cparams(("parallel", "parallel", "arbitrary"), 40), name="mla_attn",
    )(qp, kp, v)


def _wout_kernel(*refs, n_src, n_lat_tiles):
    x_refs, (mod_ref, a_ref, b_ref, m_ref, r_ref, w_ref, o_ref) = refs[:n_src], refs[n_src:]
    g, tm, d = o_ref.shape
    acc = None
    for k, ref in enumerate((a_ref, b_ref, m_ref, r_ref)):
        part = jnp.dot(ref[...].reshape(g * tm, GROUP_W), w_ref[k * GROUP_W:(k + 1) * GROUP_W, :],
                       preferred_element_type=F32)
        acc = part if acc is None else acc + part
    o_ref[...] = _tile_rows(x_refs, n_lat_tiles) + mod_ref[:, 2:3, :] * acc.reshape(g, tm, d)


def _wout(src, mod_l, a, b, m, r, w, layer, n_lat_tiles):
    B, _, D = src[0].shape
    Lt = sum(s.shape[1] for s in src)
    tm, g = TOKEN_TILE, ROW_BATCHES
    bm = lambda b, t: (b, t, 0)
    mix = pl.BlockSpec((g, tm, GROUP_W), bm)
    return pl.pallas_call(
        functools.partial(_wout_kernel, n_src=len(src), n_lat_tiles=n_lat_tiles), grid=(B // g, Lt // tm),
        in_specs=_residual_specs(src, tm, n_lat_tiles, g) + [
            pl.BlockSpec((g, 6, D), lambda b, t: (jnp.where(t >= n_lat_tiles, B // g, b), 0, 0)),
            mix, mix, mix, mix,
            pl.BlockSpec((None, 4 * GROUP_W, D), lambda b, t: (layer, 0, 0))],
        out_specs=pl.BlockSpec((g, tm, D), bm),
        out_shape=jax.ShapeDtypeStruct((B, Lt, D), F32),
        compiler_params=_cparams(("parallel", "parallel"), 52), name="out_proj",
    )(*src, mod_l, a, b, m, r, w)


def _mlp_kernel(x_ref, xc_ref, mod_ref, modc_ref, nw_ref, w1_ref, w2_ref, *rest, nf, rows, saxis):
    o_ref, h_ref, a_ref = rest[-3:]
    s = pl.program_id(saxis)

    def up(h, f, r):
        a = jnp.maximum(jnp.dot(h, w1_ref[...], preferred_element_type=F32), 0.0)
        a_ref[f, r, :] = (a * a).astype(BF16)

    @pl.when(s == 0)
    def _():
        x = x_ref[...].reshape(rows, x_ref.shape[-1])
        half = rows // 2
        for r in (slice(0, half), slice(half, rows)):
            h = _norm_mod(x[r], nw_ref[...], mod_ref[3:4, :], mod_ref[4:5, :])
            h_ref[r, :] = h
            up(h, 0, r)

    @pl.when(jnp.logical_and(s > 0, s < nf))
    def _():
        up(h_ref[...], s, slice(None))

    @pl.when(s >= nf)
    def _():
        y = jnp.dot(a_ref[0], w2_ref[0], preferred_element_type=F32)
        for f in range(1, nf):
            y += jnp.dot(a_ref[f], w2_ref[f], preferred_element_type=F32)
        o_ref[...] = xc_ref[...] + (modc_ref[5:6, :] * y).reshape(o_ref.shape)


def _mlp(X, mod_l, nw, w1, w2, layer, n_lat, with_ctx):
    B, Lt, D = X.shape
    n_ctx = Lt - n_lat
    ff = w1.shape[2]
    tm, tf, tn = MLP_ROWS, MLP_FF_TILE, MLP_OUT_TILE
    nf, nn = ff // tf, D // tn
    ns = nf + nn
    w2r = w2.reshape(w2.shape[0], nf, tf, D)
    nwr = nw.reshape(1, D)
    col = lambda s: jnp.maximum(s - nf, 0)
    scratch = [pltpu.VMEM((tm, D), BF16), pltpu.VMEM((nf, tm, tf), BF16)]
    lat = pl.pallas_call(
        functools.partial(_mlp_kernel, nf=nf, rows=tm, saxis=2), grid=(B, n_lat // tm, ns),
        in_specs=[pl.BlockSpec((None, tm, D), lambda b, t, s: (b, t, 0)),
                  pl.BlockSpec((None, tm, tn), lambda b, t, s: (b, t, col(s))),
                  pl.BlockSpec((None, 6, D), lambda b, t, s: (b, 0, 0)),
                  pl.BlockSpec((None, 6, tn), lambda b, t, s: (b, 0, col(s))),
                  pl.BlockSpec((1, D), lambda b, t, s: (0, 0)),
                  pl.BlockSpec((None, D, tf), lambda b, t, s: (layer, 0, jnp.minimum(s, nf - 1))),
                  pl.BlockSpec((None, nf, tf, tn), lambda b, t, s: (layer, 0, 0, col(s)))],
        out_specs=pl.BlockSpec((None, tm, tn), lambda b, t, s: (b, t, col(s))),
        out_shape=jax.ShapeDtypeStruct((B, Lt if with_ctx else n_lat, D), F32),
        scratch_shapes=scratch,
        compiler_params=_cparams(("parallel", "parallel", "arbitrary"), 58), name="mlp_lat",
    )(X, X, mod_l, mod_l, nwr, w1, w2r)
    if not with_ctx:
        return lat
    g = tm // n_ctx
    cblk = n_lat // n_ctx
    return pl.pallas_call(
        functools.partial(_mlp_kernel, nf=nf, rows=tm, saxis=1), grid=(B // g, ns),
        in_specs=[pl.BlockSpec((g, n_ctx, D), lambda i, s: (i, cblk, 0)),
                  pl.BlockSpec((g, n_ctx, tn), lambda i, s: (i, cblk, col(s))),
                  pl.BlockSpec((None, 6, D), lambda i, s: (B, 0, 0)),
                  pl.BlockSpec((None, 6, tn), lambda i, s: (B, 0, col(s))),
                  pl.BlockSpec((1, D), lambda i, s: (0, 0)),
                  pl.BlockSpec((None, D, tf), lambda i, s: (layer, 0, jnp.minimum(s, nf - 1))),
                  pl.BlockSpec((None, nf, tf, tn), lambda i, s: (layer, 0, 0, col(s))),
                  pl.BlockSpec(memory_space=pl.ANY)],
        out_specs=pl.BlockSpec((g, n_ctx, tn), lambda i, s: (i, cblk, col(s))),
        out_shape=jax.ShapeDtypeStruct((B, Lt, D), F32),
        input_output_aliases={7: 0},
        scratch_shapes=scratch,
        compiler_params=_cparams(("parallel", "arbitrary"), 58), name="mlp_ctx",
    )(X, X, mod_l, mod_l, nwr, w1, w2r, lat)


def _pack_w_in(w_in):
    offs = [0, 512, 1024, 1536, 2048, 2560, 2576, 2960, 3088, 3152, 3664, 4176]
    u, mq, mk, mv, mo, mg, cq, ckv, kr, lx, lg = [w_in[..., offs[i]:offs[i + 1]] for i in range(11)]
    depth, d = w_in.shape[:2]
    mg = jnp.transpose(mg.reshape(depth, d, 2, 2, ML_HEADS), (0, 1, 4, 3, 2))[:, :, :, ::-1, :].reshape(depth, d, 16)
    pad = jnp.zeros((depth, d, MISC_W - MLA_ROPE - 16), w_in.dtype)
    w_p = jnp.concatenate([u, mq, mk, mv, mo, cq, ckv, lx, lg, kr, mg, pad], axis=-1).astype(BF16)
    return w_p, jnp.transpose(mg, (0, 2, 1)).astype(BF16), jnp.transpose(mv, (0, 2, 1)).astype(BF16)


def _pack_mla(q_a_norm, w_q_up, kv_a_norm, w_kv_up, q_norm, k_norm):
    depth = w_q_up.shape[0]
    wq = w_q_up.reshape(depth, MLA_Q_LORA, MLA_HEADS, MLA_QK)
    wq = jnp.pad(wq, ((0, 0), (0, 0), (0, 0), (0, MLA_PAD - MLA_QK))).reshape(depth, MLA_Q_LORA, -1).astype(BF16)
    wkv = w_kv_up.reshape(depth, MLA_KV_LORA, MLA_HEADS, MLA_NOPE + MLA_V)
    wkv = jnp.concatenate([wkv[..., :MLA_NOPE].reshape(depth, MLA_KV_LORA, -1),
                           wkv[..., MLA_NOPE:].reshape(depth, MLA_KV_LORA, -1)], axis=-1).astype(BF16)
    row = lambda a: a.astype(F32)[:, None, :]
    head = lambda a: jnp.pad(a.astype(F32), ((0, 0), (0, MLA_PAD - MLA_QK)))[:, None, :]
    return row(q_a_norm), wq, row(kv_a_norm), wkv, head(q_norm), head(k_norm)


def _pack_lru(wa, ba, wx, bx):
    wab = jnp.concatenate([wa, wx], axis=-1).astype(BF16)
    bias = jnp.stack([ba, bx], axis=2).astype(F32)
    return wab, bias


def kernel(x, c, ctx, c_ctx, ada_w, ada_b, norm1_w, norm2_w, w_in, w_out, s5_lam_re, s5_lam_im, s5_log_dt, s5_b_re, s5_b_im, s5_c_re, s5_c_im, s5_d, s5_glu_w, s5_glu_b, ml_ig_bias, ml_fg_bias, ml_out_norm, mla_q_a_norm, mla_w_q_up, mla_kv_a_norm, mla_w_kv_up, mla_q_norm, mla_k_norm, lru_conv_w, lru_conv_b, lru_wa, lru_ba, lru_wx, lru_bx, lru_lam, mlp_w1, mlp_w2):
    B, L, D = x.shape
    Lc = ctx.shape[1]
    depth = ada_w.shape[0]
    n_lat_tiles = L // TOKEN_TILE
    src = (x.astype(F32), ctx.astype(F32))
    mod = _ada_all(c, c_ctx, ada_w, ada_b)
    w_p, wgt, wvt = _pack_w_in(w_in)
    mla_w = _pack_mla(mla_q_a_norm, mla_w_q_up, mla_kv_a_norm, mla_w_kv_up, mla_q_norm, mla_k_norm)
    wab, lru_bias = _pack_lru(lru_wa, lru_ba, lru_wx, lru_bx)
    tables = _rope_tables(L, Lc)
    w_out_b, glu_w_b = w_out.astype(BF16), s5_glu_w.astype(BF16)
    w1_b, w2_b = mlp_w1.astype(BF16), mlp_w2.astype(BF16)
    s5_ar, s5_ai, s5_bbr, s5_bbi = _s5_discretise(s5_lam_re, s5_lam_im, s5_log_dt, s5_b_re, s5_b_im)
    for l in range(depth):
        u, mq, mk, mv, mo, lx, lg, misc, mgt, qp, kp, v = _win(src, mod[l], norm1_w[l], w_p, wgt, wvt, mla_w, tables,
                                                               l, n_lat_tiles)
        s5p = _s5_params(s5_ar[l], s5_ai[l], s5_bbr[l], s5_bbi[l], s5_c_re[l], s5_c_im[l])
        a_mix, r_mix = _recurrent(u, lx, lg, s5p, s5_d[l], glu_w_b[l], s5_glu_b[l], lru_conv_w[l], lru_conv_b[l],
                                  wab[l], lru_bias[l], lru_lam[l], L)
        b_mix = _mlstm(mq, mk, mv, mo, misc, mgt, ml_ig_bias[l], ml_fg_bias[l], ml_out_norm[l], L)
        m_mix = _mla_attn(qp, kp, v, L)
        X = _wout(src, mod[l], a_mix, b_mix, m_mix, r_mix, w_out_b, l, n_lat_tiles)
        X = _mlp(X, mod[l], norm2_w[l], w1_b, w2_b, l, L, with_ctx=l < depth - 1)
        src = (X,)
    return X
```

```python
import functools

import jax
import jax.numpy as jnp
from jax import lax
from jax.experimental import pallas as pl
from jax.experimental.pallas import tpu as pltpu

F32 = jnp.float32
BF16 = jnp.bfloat16
EPS = 1e-6
HIGHEST = lax.Precision.HIGHEST

GROUP_W = 512
TOKEN_TILE = 256
ROW_BATCHES = 2
S5_STATE = 64
S5_CH = 16
S5_STRIPS = 4
S5_STRIP_STATES = 512
S5_MIN_NEG = -1e-4
S5_CHUNK = 64
ML_HEADS = 4
ML_DH = 128
ML_CHUNK = 128
ML_HEADS_PER_STEP = 4
MLA_HEADS = 4
MLA_NOPE = 128
MLA_ROPE = 64
MLA_QK = MLA_NOPE + MLA_ROPE
MLA_V = 128
MLA_Q_LORA = 384
MLA_KV_LORA = 128
MLA_PAD = 256
ATTN_SCALE = MLA_QK ** -0.5
ROPE_BASE = 10000.0
GRID_W = 64
Q_TILE = 2048
Q_SUB = 256
LRU_BLOCKS = 4
LRU_BLOCK_W = 128
LRU_CONV = 4
LRU_C = 8.0
MLP_FF_TILE = 1024
MLP_OUT_TILE = 256
MLP_ROWS = 1024

COL_U, COL_MQ, COL_MK, COL_MV, COL_MO = 0, 512, 1024, 1536, 2048
COL_CQ, COL_CKV, COL_LX, COL_LG, COL_MISC = 2560, 2944, 3072, 3584, 4096
IN_COLS_PACKED = 4224
MISC_W = 128


def _cparams(sem, vmem_mb):
    return pltpu.CompilerParams(dimension_semantics=sem, vmem_limit_bytes=vmem_mb << 20)


def _norm_mod(x, nw, shift, scale):
    ms = jnp.mean(x * x, axis=-1, keepdims=True)
    y = (x * lax.rsqrt(ms + EPS)) * nw
    return (y * (1.0 + scale) + shift).astype(BF16)


def _log_sigmoid(x):
    return jnp.minimum(x, 0.0) - jnp.log1p(jnp.exp(-jnp.abs(x)))


def _softplus(x):
    return jnp.maximum(x, 0.0) + jnp.log1p(jnp.exp(-jnp.abs(x)))


def _ada_kernel(c_ref, w_ref, b_ref, o_ref):
    c = c_ref[...]
    act = (c * jax.nn.sigmoid(c)).astype(BF16)
    o_ref[...] = jnp.dot(act, w_ref[...].astype(BF16), preferred_element_type=F32) + b_ref[...]


def _ada_all(c, c_ctx, ada_w, ada_b):
    depth, d, n6 = ada_w.shape
    nb = c.shape[0]
    rows = nb + 8
    cc = jnp.concatenate([c.astype(F32), jnp.broadcast_to(c_ctx.astype(F32)[None], (8, d))], axis=0)
    tn = 1024
    out = pl.pallas_call(
        _ada_kernel,
        grid=(depth, n6 // tn),
        in_specs=[pl.BlockSpec((rows, d), lambda l, n: (0, 0)),
                  pl.BlockSpec((None, d, tn), lambda l, n: (l, 0, n)),
                  pl.BlockSpec((None, 1, tn), lambda l, n: (l, 0, n))],
        out_specs=pl.BlockSpec((None, rows, tn), lambda l, n: (l, 0, n)),
        out_shape=jax.ShapeDtypeStruct((depth, rows, n6), F32),
        compiler_params=_cparams(("parallel", "parallel"), 40),
        name="ada_mod",
    )(cc, ada_w, ada_b.reshape(depth, 1, n6))
    return out.reshape(depth, rows, 6, d)


def _tile_rows(x_refs, n_lat_tiles):
    if len(x_refs) == 1:
        return x_refs[0][...]
    return jnp.where(pl.program_id(1) >= n_lat_tiles, x_refs[1][...], x_refs[0][...])


def _win_kernel(*refs, n_src, n_lat_tiles):
    x_refs, refs = refs[:n_src], refs[n_src:]
    (mod_ref, nw_ref, w_ref, wgt_ref, wvt_ref, cos_ref, sin_ref, qan_ref, wq_ref, kvan_ref, wkv_ref, qn_ref, kn_ref,
     u_ref, mq_ref, mk_ref, mvt_ref, mo_ref, lx_ref, lg_ref, misc_ref, mgt_ref, qp_ref, kp_ref, v_ref) = refs
    g, tm, d = x_refs[0].shape
    h = _norm_mod(_tile_rows(x_refs, n_lat_tiles), nw_ref[...], mod_ref[:, 0:1, :], mod_ref[:, 1:2, :])
    h = h.reshape(g * tm, d)
    nt_dims = (((1,), (1,)), ((), ()))

    def mm(lo, width):
        return jnp.dot(h, w_ref[:, lo:lo + width], preferred_element_type=F32)

    def store(ref, lo, scale=None):
        def run():
            y = mm(lo, ref.shape[-1])
            ref[...] = (y if scale is None else y * scale).astype(ref.dtype).reshape(ref.shape)
        return run

    def values_t():
        mvt = lax.dot_general(wvt_ref[...], h, nt_dims, preferred_element_type=F32).astype(BF16)
        for bi in range(g):
            for cc in range(mvt_ref.shape[1]):
                lo = bi * tm + cc * ML_CHUNK
                mvt_ref[bi, cc] = mvt[:, lo:lo + ML_CHUNK]

    def gates_t():
        mgt = lax.dot_general(wgt_ref[...], h, nt_dims, preferred_element_type=F32)
        for bi in range(g):
            mgt_ref[bi] = mgt[:, bi * tm:(bi + 1) * tm]

    misc = mm(COL_MISC, MISC_W)
    misc_ref[...] = misc.reshape(misc_ref.shape)
    prep = _mla_prep_parts(mm(COL_CQ, MLA_Q_LORA), mm(COL_CKV, MLA_KV_LORA), misc, cos_ref, sin_ref, qan_ref,
                           wq_ref, kvan_ref, wkv_ref, qn_ref, kn_ref, qp_ref, kp_ref, v_ref)
    proj = [store(u_ref, COL_U), store(mq_ref, COL_MQ), store(mk_ref, COL_MK, ML_DH ** -0.5), values_t,
            store(mo_ref, COL_MO), store(lx_ref, COL_LX), store(lg_ref, COL_LG), gates_t]
    for k in range(max(len(prep), len(proj))):
        if k < len(proj):
            proj[k]()
        if k < len(prep):
            prep[k]()


def _residual_specs(src, tm, n_lat_tiles, g=None):
    D = src[0].shape[-1]
    if len(src) == 1:
        return [pl.BlockSpec((g, tm, D), lambda b, t: (b, t, 0))]
    return [pl.BlockSpec((g, tm, D), lambda b, t: (b, jnp.minimum(t, n_lat_tiles - 1), 0)),
            pl.BlockSpec((g, tm, D), lambda b, t: (b, jnp.maximum(t - n_lat_tiles, 0), 0),
                         pipeline_mode=pl.Buffered(1))]


def _win(src, mod_l, nw, w_p, wgt, wvt, mla_w, tables, layer, n_lat_tiles):
    B, _, D = src[0].shape
    Lt = sum(a.shape[1] for a in src)
    tm, g = TOKEN_TILE, ROW_BATCHES
    nt = Lt // tm
    hp = MLA_HEADS * MLA_PAD
    bm = lambda b, t: (b, t, 0)
    per_layer = lambda *shape: pl.BlockSpec((None,) + shape, lambda b, t: (layer,) + (0,) * len(shape))
    table = pl.BlockSpec((tm, 128), lambda b, t: (t, 0))
    in_specs = _residual_specs(src, tm, n_lat_tiles, g) + [
        pl.BlockSpec((g, 6, D), lambda b, t: (jnp.where(t >= n_lat_tiles, B // g, b), 0, 0)),
        pl.BlockSpec((1, D), lambda b, t: (0, 0)),
        pl.BlockSpec((None, D, IN_COLS_PACKED), lambda b, t: (layer, 0, 0), pipeline_mode=pl.Buffered(1)),
        per_layer(16, D), per_layer(GROUP_W, D), table, table,
        per_layer(1, MLA_Q_LORA), per_layer(MLA_Q_LORA, hp),
        per_layer(1, MLA_KV_LORA), per_layer(MLA_KV_LORA, MLA_HEADS * (MLA_NOPE + MLA_V)),
        per_layer(1, MLA_PAD), per_layer(1, MLA_PAD),
    ]
    cpt = tm // ML_CHUNK
    out_specs = [
        pl.BlockSpec((g, tm, GROUP_W), bm),
        pl.BlockSpec((g, tm, GROUP_W), bm),
        pl.BlockSpec((g, tm, GROUP_W), bm),
        pl.BlockSpec((g, cpt, GROUP_W, ML_CHUNK), lambda b, t: (b, t, 0, 0)),
        pl.BlockSpec((g, tm, GROUP_W), bm),
        pl.BlockSpec((g, tm, GROUP_W), bm),
        pl.BlockSpec((g, tm, GROUP_W), bm),
        pl.BlockSpec((g, tm, MISC_W), bm),
        pl.BlockSpec((g, 16, tm), lambda b, t: (b, 0, t)),
        pl.BlockSpec((g, tm, hp), bm),
        pl.BlockSpec((g, tm, hp), bm),
        pl.BlockSpec((g, tm, GROUP_W), bm),
    ]
    sds = jax.ShapeDtypeStruct
    out_shape = [
        sds((B, Lt, GROUP_W), F32),
        sds((B, Lt, GROUP_W), BF16), sds((B, Lt, GROUP_W), BF16),
        sds((B, Lt // ML_CHUNK, GROUP_W, ML_CHUNK), BF16),
        sds((B, Lt, GROUP_W), F32),
        sds((B, Lt, GROUP_W), F32), sds((B, Lt, GROUP_W), F32),
        sds((B, Lt, MISC_W), F32),
        sds((B, 16, Lt), F32),
        sds((B, Lt, hp), BF16), sds((B, Lt, hp), BF16), sds((B, Lt, GROUP_W), BF16),
    ]
    return pl.pallas_call(
        functools.partial(_win_kernel, n_src=len(src), n_lat_tiles=n_lat_tiles),
        grid=(B // g, nt), in_specs=in_specs, out_specs=out_specs, out_shape=out_shape,
        compiler_params=_cparams(("parallel", "parallel"), 58), name="in_proj",
    )(*src, mod_l, nw.reshape(1, D), w_p, wgt, wvt, tables[0], tables[1], *mla_w)


def _rec_kernel(*refs, tc, nb, nlc, nc, reverse, final):
    if final:
        (ub_ref, yp_ref, wd_ref, a_ref, wc_ref, dsk_ref, gw_ref, gb_ref,
         x_ref, xp_ref, xn_ref, hp_ref, g_ref, cw_ref, cb_ref, w_ref, bias_ref, lam_ref,
         oa_ref, or_ref, u_ref, buf, st, xpad, abuf, bbuf, lst, ybuf, o3) = refs
    else:
        (ub_ref, wd_ref, a_ref, wc_ref,
         x_ref, xp_ref, xn_ref, cw_ref, cb_ref, w_ref, bias_ref, lam_ref,
         oa_ref, or_ref, u_ref, buf, st, xpad, abuf, bbuf, lst) = refs
    i = pl.program_id(0)
    c = (nc - 1 - i) if reverse else lax.rem(i + nlc, nc)
    R = tc * nb
    RB = min(R, 512)
    RL = min(R, 256)
    NS = S5_STRIP_STATES
    nh = nb // 8
    left = LRU_CONV // 2
    halo = xp_ref.shape[1]

    @pl.when(i == 0)
    def _():
        st[...] = jnp.zeros_like(st)
        lst[...] = jnp.zeros_like(lst)

    starts = jnp.logical_or(c == 0, c == nlc)
    ends = jnp.logical_or(c == nlc - 1, c == nc - 1)
    for j in range(left):
        xpad[j * nb:(j + 1) * nb, :] = jnp.where(starts, 0.0, xp_ref[:, halo - left + j, :])
    for t in range(tc):
        u_ref[t * nb:(t + 1) * nb, :] = ub_ref[:, t, :]
        xpad[(left + t) * nb:(left + t + 1) * nb, :] = x_ref[:, t, :]
    for j in range(LRU_CONV - 1 - left):
        xpad[(left + tc + j) * nb:(left + tc + j + 1) * nb, :] = jnp.where(ends, 0.0, xn_ref[:, j, :])
    sp = _softplus(-lam_ref[...])

    def lru_gate_parts(rb):
        r0 = rb * RL
        cache = {}

        def conv():
            if "xs" not in cache:
                xs = cb_ref[...]
                for j in range(LRU_CONV):
                    xs = xs + xpad[r0 + j * nb:r0 + j * nb + RL, :] * cw_ref[j:j + 1, :]
                cache["xs"] = xs
            return cache["xs"]

        def block(n):
            cs = slice(n * LRU_BLOCK_W, (n + 1) * LRU_BLOCK_W)
            xs = conv()[:, cs]
            z = jnp.dot(xs.astype(BF16), w_ref[n], preferred_element_type=F32)
            r = jax.nn.sigmoid(z[:, :LRU_BLOCK_W] + bias_ref[0:1, cs])
            ig = jax.nn.sigmoid(z[:, LRU_BLOCK_W:] + bias_ref[1:2, cs])
            log_a = (-LRU_C * r) * sp[:, cs]
            abuf[r0:r0 + RL, cs] = jnp.exp(log_a)
            th = jnp.tanh(log_a)
            bbuf[r0:r0 + RL, cs] = jnp.sqrt(-2.0 * th / (1.0 - th)) * (ig * xs)

        return [functools.partial(block, n) for n in range(LRU_BLOCKS)]

    def s5_drive_parts(s):
        cs = slice(s * 128, (s + 1) * 128)

        def part(rb):
            rs = slice(rb * RB, (rb + 1) * RB)
            buf[s % 2, rs, :] = jnp.dot(u_ref[rs, cs].astype(BF16), wd_ref[s], preferred_element_type=F32)

        return [functools.partial(part, rb) for rb in range(R // RB)]

    def s5_scan(s):
        sb = buf.at[s % 2]
        ar = jnp.broadcast_to(a_ref[s, 0:1, :], (8, NS))
        ai = jnp.broadcast_to(a_ref[s, 1:2, :], (8, NS))
        init = []
        for hh in range(nh):
            init += [st[s, hh * 8:(hh + 1) * 8, 0:NS], st[s, hh * 8:(hh + 1) * 8, NS:2 * NS]]

        def body(k, carry):
            t = (tc - 1 - k) if reverse else k
            out = []
            for hh in range(nh):
                sr, si = carry[2 * hh], carry[2 * hh + 1]
                row = pl.multiple_of(t * nb + hh * 8, 8)
                br = sb[pl.ds(row, 8), 0:NS]
                bi = sb[pl.ds(row, 8), NS:2 * NS]
                nsr = ar * sr - ai * si + br
                nsi = ar * si + ai * sr + bi
                sb[pl.ds(row, 8), 0:NS] = nsr
                sb[pl.ds(row, 8), NS:2 * NS] = nsi
                out += [nsr, nsi]
            return tuple(out)

        fin = lax.fori_loop(0, tc, body, tuple(init), unroll=2)
        for hh in range(nh):
            st[s, hh * 8:(hh + 1) * 8, 0:NS] = fin[2 * hh]
            st[s, hh * 8:(hh + 1) * 8, NS:2 * NS] = fin[2 * hh + 1]

    def s5_readout_parts(s):
        cs = slice(s * 128, (s + 1) * 128)

        def part(rb):
            rs = slice(rb * RB, (rb + 1) * RB)
            y = jnp.dot(buf[s % 2, rs, :].astype(BF16), wc_ref[s], preferred_element_type=F32)
            if final:
                ybuf[rs, cs] = y
            else:
                oa_ref[rs, cs] = y

        return [functools.partial(part, rb) for rb in range(R // RB)]

    def emit_interleaved(mxu_parts, vpu_parts):
        n = max(len(mxu_parts), len(vpu_parts))
        for k in range(n):
            if k < len(mxu_parts):
                mxu_parts[k]()
            if k < len(vpu_parts):
                vpu_parts[k]()

    n_gate_blocks = R // RL
    for s in range(S5_STRIPS + 1):
        mxu = (s5_readout_parts(s - 1) if s > 0 else []) + (s5_drive_parts(s) if s < S5_STRIPS else [])
        vpu = []
        for rb in range(n_gate_blocks):
            if rb % S5_STRIPS == s:
                vpu += lru_gate_parts(rb)
        emit_interleaved(mxu, vpu)
        if s < S5_STRIPS:
            s5_scan(s)

    init = tuple(lst[hh * 8:(hh + 1) * 8, :] for hh in range(nh))

    def lru_body(k, carry):
        t = (tc - 1 - k) if reverse else k
        out = []
        for hh in range(nh):
            row = pl.multiple_of(t * nb + hh * 8, 8)
            hnew = abuf[pl.ds(row, 8), :] * carry[hh] + bbuf[pl.ds(row, 8), :]
            bbuf[pl.ds(row, 8), :] = hnew
            out.append(hnew)
        return tuple(out)

    fin = lax.fori_loop(0, tc, lru_body, init, unroll=4)
    for hh in range(nh):
        lst[hh * 8:(hh + 1) * 8, :] = fin[hh]

    if not final:
        or_ref[...] = bbuf[...]
        return
    for rb in range(R // RB):
        rs = slice(rb * RB, (rb + 1) * RB)
        y = ybuf[rs, :] + yp_ref[rs, :] + dsk_ref[...] * u_ref[rs, :]
        g = jax.nn.gelu(y)
        z = jnp.dot(g.astype(BF16), gw_ref[...], preferred_element_type=F32) + gb_ref[...]
        res = g * jax.nn.sigmoid(z)
        for tt in range(RB // nb):
            o3[:, rb * (RB // nb) + tt, :] = res[tt * nb:(tt + 1) * nb, :]
    oa_ref[...] = o3[...].astype(BF16)
    for t in range(tc):
        rows = slice(t * nb, (t + 1) * nb)
        o3[:, t, :] = hp_ref[rows, :] + bbuf[rows, :]
    or_ref[...] = (o3[...] * jax.nn.gelu(g_ref[...])).astype(BF16)


def _chunk_order(nc, nlc, reverse):
    if reverse:
        return lambda i: nc - 1 - i
    return lambda i: lax.rem(i + nlc, nc)


def _recurrent(u, lx, lg, s5p, d_skip, glu_w, glu_b, conv_w, conv_b, wab, bias, lam, n_lat):
    nb, lt, _ = u.shape
    rows = nb * lt
    tc = S5_CHUNK
    R = tc * nb
    nc, nlc = lt // tc, n_lat // tc
    halo = 8
    wd, a, wc = s5p
    fwd = None
    for d in range(2):
        final = d == 1
        cidx = _chunk_order(nc, nlc, reverse=final)
        row_spec = pl.BlockSpec((R, GROUP_W), lambda i: (cidx(i), 0))
        bt_spec = pl.BlockSpec((nb, tc, GROUP_W), lambda i: (0, cidx(i), 0))
        prev_spec = pl.BlockSpec((nb, halo, GROUP_W),
                                 lambda i: (0, jnp.maximum(cidx(i) * (tc // halo) - 1, 0), 0))
        next_spec = pl.BlockSpec((nb, halo, GROUP_W),
                                 lambda i: (0, jnp.minimum((cidx(i) + 1) * (tc // halo), lt // halo - 1), 0))
        full = lambda shape: pl.BlockSpec(shape, lambda i: (0,) * len(shape))
        s5_w = [full((S5_STRIPS, 128, 2 * S5_STRIP_STATES)), full((S5_STRIPS, 2, S5_STRIP_STATES)),
                full((S5_STRIPS, 2 * S5_STRIP_STATES, 128))]
        lru_w = [full((LRU_CONV, GROUP_W)), full((1, GROUP_W)), full((LRU_BLOCKS, LRU_BLOCK_W, 2 * LRU_BLOCK_W)),
                 full((2, GROUP_W)), full((1, GROUP_W))]
        lru_args = [conv_w, conv_b.reshape(1, GROUP_W), wab[d], bias[d], lam[d].reshape(1, GROUP_W)]
        if final:
            in_specs = ([bt_spec, row_spec] + s5_w + [full((1, GROUP_W)), full((GROUP_W, GROUP_W)), full((1, GROUP_W))]
                        + [bt_spec, prev_spec, next_spec, row_spec, bt_spec] + lru_w)
            args = ([u, fwd[0], wd[d], a[d], wc[d], d_skip.reshape(1, GROUP_W), glu_w, glu_b.reshape(1, GROUP_W)]
                    + [lx, lx, lx, fwd[1], lg] + lru_args)
            out_specs = [bt_spec, bt_spec]
            out_shape = [jax.ShapeDtypeStruct((nb, lt, GROUP_W), BF16)] * 2
        else:
            in_specs = [bt_spec] + s5_w + [bt_spec, prev_spec, next_spec] + lru_w
            args = [u, wd[d], a[d], wc[d], lx, lx, lx] + lru_args
            out_specs = [row_spec, row_spec]
            out_shape = [jax.ShapeDtypeStruct((rows, GROUP_W), F32)] * 2
        scratch = [pltpu.VMEM((R, GROUP_W), F32), pltpu.VMEM((2, R, 2 * S5_STRIP_STATES), F32),
                   pltpu.VMEM((S5_STRIPS, nb, 2 * S5_STRIP_STATES), F32),
                   pltpu.VMEM((R + 3 * nb, GROUP_W), F32), pltpu.VMEM((R, GROUP_W), F32),
                   pltpu.VMEM((R, GROUP_W), F32), pltpu.VMEM((nb, GROUP_W), F32)]
        if final:
            scratch += [pltpu.VMEM((R, GROUP_W), F32), pltpu.VMEM((nb, tc, GROUP_W), F32)]
        fwd = pl.pallas_call(
            functools.partial(_rec_kernel, tc=tc, nb=nb, nlc=nlc, nc=nc, reverse=final, final=final),
            grid=(nc,), in_specs=in_specs, out_specs=out_specs, out_shape=out_shape,
            scratch_shapes=scratch,
            compiler_params=_cparams(("arbitrary",), 56), name="rec_rev" if final else "rec_fwd",
        )(*args)
    return fwd


def _s5_pole_kernel(lr_ref, li_ref, ldt_ref, ar_ref, ai_ref, fr_ref, fi_ref):
    lr = jnp.minimum(lr_ref[...], S5_MIN_NEG)
    li = li_ref[...]
    dt = jnp.exp(ldt_ref[...])
    mag = jnp.exp(lr * dt)
    ar, ai = mag * jnp.cos(li * dt), mag * jnp.sin(li * dt)
    den = lr * lr + li * li
    ar_ref[...] = ar
    ai_ref[...] = ai
    fr_ref[...] = ((ar - 1.0) * lr + ai * li) / den
    fi_ref[...] = (ai * lr - (ar - 1.0) * li) / den


def _s5_input_kernel(fr_ref, fi_ref, bre_ref, bim_ref, bbr_ref, bbi_ref):
    fr, fi = fr_ref[...], fi_ref[...]
    bbr_ref[...] = fr * bre_ref[...] - fi * bim_ref[...]
    bbi_ref[...] = fr * bim_ref[...] + fi * bre_ref[...]


def _s5_discretise(lam_re, lam_im, log_dt, b_re, b_im):
    shape = lam_re.shape
    rows, p = lam_re.size // shape[-1], shape[-1]
    flat = lambda t: t.astype(F32).reshape(rows, p)
    ldt = jnp.broadcast_to(log_dt.astype(F32)[..., None], shape)
    ar, ai, fr, fi = pl.pallas_call(
        _s5_pole_kernel, out_shape=[jax.ShapeDtypeStruct((rows, p), F32)] * 4, name="s5_poles",
    )(flat(lam_re), flat(lam_im), flat(ldt))
    n, rb = lam_re.size, 2048
    cspec = pl.BlockSpec((rb, 1), lambda i: (i, 0))
    bspec = pl.BlockSpec((rb, S5_CH), lambda i: (i, 0))
    bbr, bbi = pl.pallas_call(
        _s5_input_kernel, grid=(n // rb,), in_specs=[cspec, cspec, bspec, bspec], out_specs=[bspec, bspec],
        out_shape=[jax.ShapeDtypeStruct((n, S5_CH), F32)] * 2,
        compiler_params=_cparams(("parallel",), 32), name="s5_input_matrix",
    )(fr.reshape(n, 1), fi.reshape(n, 1), b_re.astype(F32).reshape(n, S5_CH), b_im.astype(F32).reshape(n, S5_CH))
    return ar.reshape(shape), ai.reshape(shape), bbr.reshape(b_re.shape), bbi.reshape(b_re.shape)


def _s5_params(ar, ai, bbr, bbi, c_re, c_im):
    gs = 128 // S5_CH
    eye = jnp.eye(gs, dtype=F32)

    def drive(bb):
        t = bb.reshape(2, S5_STRIPS, gs, S5_STATE, S5_CH)
        return jnp.einsum('dsgpc,gh->dsgchp', t, eye).reshape(2, S5_STRIPS, 128, S5_STRIP_STATES)

    def read(cc):
        t = cc.reshape(2, S5_STRIPS, gs, S5_CH, S5_STATE)
        return jnp.einsum('dsgcp,gh->dsgphc', t, eye).reshape(2, S5_STRIPS, S5_STRIP_STATES, 128)

    wd = jnp.concatenate([drive(bbr), drive(bbi)], axis=-1).astype(BF16)
    wc = jnp.concatenate([read(c_re.astype(F32)), read(-c_im.astype(F32))], axis=-2).astype(BF16)
    a = jnp.stack([ar.reshape(2, S5_STRIPS, S5_STRIP_STATES), ai.reshape(2, S5_STRIPS, S5_STRIP_STATES)], axis=2)
    return wd, a, wc


def _mlstm_kernel(q_ref, k_ref, vt_ref, o_ref, gr_ref, gc_ref, br_ref, bc_ref, nw_ref, out_ref,
                  xb_scr, row_scr, nt_scr, u_scr, vec_scr, *, nch, nlat, hp):
    T = ML_CHUNK
    nt_dims = (((1,), (1,)), ((), ()))
    rid = lax.broadcasted_iota(jnp.int32, (T, T), 0)
    cid = lax.broadcasted_iota(jnp.int32, (T, T), 1)
    incl_rows = (rid <= cid).astype(F32)
    incl_cols = (rid >= cid).astype(F32)

    for j in range(hp):
        g = (gr_ref[j] + br_ref[j]).reshape(nch * 8, T)
        kind = lax.broadcasted_iota(jnp.int32, (nch * 8, T), 0) % 8
        lf = _log_sigmoid(g)
        pre = jnp.dot(lf, incl_rows, precision=HIGHEST, preferred_element_type=F32)
        b = jnp.where(kind == 1, pre[:, T - 1:T] - pre + lf, pre)
        x = g - pltpu.roll(b, 2, axis=0)
        row_scr[j] = jnp.where(kind < 2, b, x).reshape(nch, 8, T)
        gl = gc_ref[j] + bc_ref[j]
        kind = lax.broadcasted_iota(jnp.int32, (T, 128), 1) % 4
        lf = _log_sigmoid(gl)
        pre = jnp.dot(incl_cols, lf, precision=HIGHEST, preferred_element_type=F32)
        b = jnp.where(kind == 1, pre[T - 1:T, :] - pre + lf, pre)
        x = gl - pltpu.roll(b, 2, axis=1)
        for c in range(nch):
            for d in range(2):
                lane = c * 4 + 2 + d
                xb_scr[j, d, c] = jnp.broadcast_to(x[:, lane:lane + 1], (T, T))

    def independent(c, carry):
        rows = pl.ds(pl.multiple_of(c * T, T), T)
        pairs = [(j, d) for j in range(hp) for d in range(2)]
        hs = lambda j: slice(j * ML_DH, (j + 1) * ML_DH)
        swept = lambda d: (rid <= cid) if d == 0 else (rid >= cid)
        kq = {j: lax.dot_general(k_ref[rows, hs(j)], q_ref[rows, hs(j)], nt_dims, preferred_element_type=F32)
              for j in range(hp)}
        xm = {p: jnp.where(swept(p[1]), xb_scr[p[0], p[1], c], -jnp.inf) for p in pairs}
        a_row = {p: jnp.max(xm[p], axis=0, keepdims=True) for p in pairs}
        w0 = {}
        for j, d in pairs:
            a_last = a_row[j, d][:, T - 1:T] if d == 0 else a_row[j, d][:, 0:1]
            w0[j, d] = jnp.exp(row_scr[j, c, 2 + d:3 + d, :] - a_last)
        s0 = {p: kq[p[0]] * jnp.exp(xm[p] - a_row[p]) for p in pairs}
        for j, d in pairs:
            vt, kc = vt_ref[c, hs(j), :], k_ref[rows, hs(j)]
            nt_scr[j, d, c] = jnp.dot(vt, s0[j, d].astype(BF16), preferred_element_type=F32)
            wv = (vt.astype(F32) * w0[j, d]).astype(BF16)
            u_scr[j, d, c] = jnp.dot(wv, kc, preferred_element_type=F32)
            vec_scr[j, d, c, 0:1, :] = a_row[j, d]
            vec_scr[j, d, c, 1:2, :] = jnp.sum(s0[j, d], axis=0, keepdims=True)
            vec_scr[j, d, c, 2:3, :] = jnp.dot(jnp.broadcast_to(w0[j, d], (8, T)).astype(BF16), kc,
                                               preferred_element_type=F32)[0:1, :]
        return carry

    lax.fori_loop(0, nch, independent, 0)

    def pass2(c, d, j, state):
        cm, n, m = state
        hs = slice(j * ML_DH, (j + 1) * ML_DH)
        qc = q_ref[pl.ds(pl.multiple_of(c * T, T), T), hs]
        a_row, d0, n0 = vec_scr[j, d, c, 0:1, :], vec_scr[j, d, c, 1:2, :], vec_scr[j, d, c, 2:3, :]
        b_row = row_scr[j, c, d:d + 1, :]
        mu = jnp.maximum(m, a_row)
        inter, r = jnp.exp(m - mu), jnp.exp(a_row - mu)
        qct = lax.dot_general(cm.astype(BF16), qc, nt_dims, preferred_element_type=F32)
        qn = lax.dot_general(jnp.broadcast_to(n, (8, ML_DH)).astype(BF16), qc, nt_dims,
                             preferred_element_type=F32)[0:1, :]
        den = inter * qn + r * d0
        inv = 1.0 / jnp.maximum(jnp.abs(den), jnp.exp(-b_row - mu))
        nt_scr[j, d, c] = (inter * qct + r * nt_scr[j, d, c]) * inv
        a_last = a_row[:, T - 1:T] if d == 0 else a_row[:, 0:1]
        b_last = b_row[:, T - 1:T] if d == 0 else b_row[:, 0:1]
        mul = jnp.maximum(m, a_last)
        decay, rl = jnp.exp(m - mul), jnp.exp(a_last - mul)
        return decay * cm + rl * u_scr[j, d, c], decay * n + rl * n0, b_last + mul

    def states(i, carry):
        out = []
        for j in range(hp):
            out.append(pass2(lax.rem(i + nlat, nch), 0, j, carry[2 * j]))
            out.append(pass2(nch - 1 - i, 1, j, carry[2 * j + 1]))
        return tuple(out)

    zero = (jnp.zeros((ML_DH, ML_DH), F32), jnp.zeros((1, ML_DH), F32), jnp.zeros((1, 1), F32))
    lax.fori_loop(0, nch, states, (zero,) * (2 * hp))

    for c in range(nch):
        rows = slice(c * T, (c + 1) * T)
        for j in range(hp):
            hs = slice(j * ML_DH, (j + 1) * ML_DH)
            ht = nt_scr[j, 0, c] + nt_scr[j, 1, c]
            hn = ht * lax.rsqrt(jnp.mean(ht * ht, axis=0, keepdims=True) + EPS) * nw_ref[j]
            out_ref[rows, hs] = (hn.T * jax.nn.sigmoid(o_ref[rows, hs])).astype(BF16)


def _mlstm(mq, mk, mvt, mo, misc, mgt, ig_bias, fg_bias, out_norm, n_lat):
    B, Lt, _ = mq.shape
    T = ML_CHUNK
    nch, nlat = Lt // T, n_lat // T
    hp = ML_HEADS_PER_STEP
    assert 4 * nch <= 128
    gr = jnp.transpose(mgt.reshape(B, ML_HEADS, 4, nch, T), (0, 1, 3, 2, 4))
    gr = jnp.pad(gr, ((0, 0), (0, 0), (0, 0), (0, 4), (0, 0)))
    gc = misc[:, :, MLA_ROPE:MLA_ROPE + 16].reshape(B, nch, T, ML_HEADS, 4)
    gc = jnp.transpose(gc, (0, 3, 2, 1, 4)).reshape(B, ML_HEADS, T, nch * 4)
    gc = jnp.pad(gc, ((0, 0), (0, 0), (0, 0), (0, 128 - nch * 4)))
    kinds = jnp.concatenate([fg_bias.astype(F32), ig_bias.astype(F32)], axis=0).T
    br = jnp.broadcast_to(jnp.pad(kinds, ((0, 0), (0, 4)))[:, :, None], (ML_HEADS, 8, T))
    bc = jnp.pad(jnp.tile(kinds, (1, nch)), ((0, 0), (0, 128 - nch * 4)))[:, None, :]
    nwb = jnp.broadcast_to(out_norm.astype(F32)[:, :, None], (ML_HEADS, ML_DH, T))
    head_spec = pl.BlockSpec((None, Lt, hp * ML_DH), lambda b, h: (b, 0, h))
    slab = lambda: pltpu.VMEM((hp, 2, nch, T, T), F32)
    return pl.pallas_call(
        functools.partial(_mlstm_kernel, nch=nch, nlat=nlat, hp=hp),
        grid=(B, ML_HEADS // hp),
        in_specs=[head_spec, head_spec,
                  pl.BlockSpec((None, nch, hp * ML_DH, T), lambda b, h: (b, 0, h, 0)),
                  head_spec,
                  pl.BlockSpec((None, hp, nch, 8, T), lambda b, h: (b, h, 0, 0, 0)),
                  pl.BlockSpec((None, hp, T, 128), lambda b, h: (b, h, 0, 0)),
                  pl.BlockSpec((hp, 8, T), lambda b, h: (h, 0, 0)),
                  pl.BlockSpec((hp, 1, 128), lambda b, h: (h, 0, 0)),
                  pl.BlockSpec((hp, ML_DH, T), lambda b, h: (h, 0, 0))],
        out_specs=head_spec,
        out_shape=jax.ShapeDtypeStruct((B, Lt, GROUP_W), BF16),
        scratch_shapes=[slab(), pltpu.VMEM((hp, nch, 8, T), F32), slab(), slab(),
                        pltpu.VMEM((hp, 2, nch, 8, T), F32)],
        compiler_params=_cparams(("parallel", "parallel"), 60), name="mlstm",
    )(mq, mk, mvt, mo, gr, gc, br, bc, nwb)


def _mla_prep_parts(cq, ckv, misc, cos_ref, sin_ref, qan_ref, wq_ref, kvan_ref, wkv_ref, qn_ref, kn_ref,
                    qp_ref, kp_ref, v_ref):
    def rms(x, w):
        return (x * lax.rsqrt(jnp.mean(x * x, axis=-1, keepdims=True) + EPS)) * w

    rows = cq.shape[0]
    g = rows // cos_ref.shape[0]
    lane = lax.broadcasted_iota(jnp.int32, (rows, 128), 1)
    first = (lane % (MLA_ROPE // 2)) < (MLA_ROPE // 4)
    hw = MLA_HEADS * MLA_NOPE
    vals = {}

    def put(ref, lo, val):
        ref[:, :, lo:lo + val.shape[-1]] = val.astype(BF16).reshape(ref.shape[0], ref.shape[1], val.shape[-1])

    def rope(x):
        sw = jnp.where(first, pltpu.roll(x, 128 - MLA_ROPE // 4, axis=1), pltpu.roll(x, MLA_ROPE // 4, axis=1))
        return x * vals["cos"] + sw * vals["sin"]

    def project():
        vals["cos"] = jnp.concatenate([cos_ref[...]] * g, axis=0)
        vals["sin"] = jnp.concatenate([sin_ref[...]] * g, axis=0)
        vals["q"] = jnp.dot(rms(cq, qan_ref[...]).astype(BF16), wq_ref[...], preferred_element_type=F32)
        kv = jnp.dot(rms(ckv, kvan_ref[...]).astype(BF16), wkv_ref[...], preferred_element_type=F32)
        vals["k"] = kv[:, :hw]
        put(v_ref, 0, kv[:, hw:])
        kr = jnp.where(lane < MLA_ROPE, misc, 0.0)
        vals["kr_sq"] = jnp.sum(kr * kr, axis=-1, keepdims=True)
        vals["kr_rot"] = rope(kr * kn_ref[:, 128:256])

    def head(h):
        base = h * MLA_PAD
        qa, qb = vals["q"][:, base:base + 128], vals["q"][:, base + 128:base + 256]
        inv = lax.rsqrt(jnp.sum(qa * qa + qb * qb, axis=-1, keepdims=True) / MLA_QK + EPS)
        put(qp_ref, base, (qa * inv) * (qn_ref[:, 0:128] * ATTN_SCALE))
        put(qp_ref, base + 128, rope((qb * inv) * (qn_ref[:, 128:256] * ATTN_SCALE)))
        ka = vals["k"][:, h * MLA_NOPE:(h + 1) * MLA_NOPE]
        inv = lax.rsqrt((jnp.sum(ka * ka, axis=-1, keepdims=True) + vals["kr_sq"]) / MLA_QK + EPS)
        put(kp_ref, base, (ka * inv) * kn_ref[:, 0:128])
        put(kp_ref, base + 128, vals["kr_rot"] * inv)

    return [project] + [functools.partial(head, h) for h in range(MLA_HEADS)]


def _attn_kernel(q_ref, k_ref, v_ref, o_ref, *, n_lat, n_lat_tiles, sub):
    def attend(n_sub, k, v):
        for first in range(0, n_sub, 2):
            blocks = [slice(i * sub, (i + 1) * sub) for i in range(first, min(first + 2, n_sub))]
            s = [lax.dot_general(q_ref[r, :], k, (((1,), (1,)), ((), ())), preferred_element_type=F32)
                 for r in blocks]
            p = [jnp.exp(si - jnp.max(si, axis=-1, keepdims=True)) for si in s]
            for r, pi in zip(blocks, p):
                den = jnp.sum(pi, axis=-1, keepdims=True)
                o_ref[r, :] = (jnp.dot(pi.astype(BF16), v, preferred_element_type=F32) / den).astype(BF16)

    qi = pl.program_id(2)

    @pl.when(qi < n_lat_tiles)
    def _():
        attend(q_ref.shape[0] // sub, k_ref[...], v_ref[...])

    @pl.when(qi >= n_lat_tiles)
    def _():
        attend((k_ref.shape[0] - n_lat) // sub, k_ref[n_lat:, :], v_ref[n_lat:, :])


def _rope_tables(n_lat, n_ctx):
    n_rows = n_lat // GRID_W
    rows = jnp.repeat(jnp.arange(n_rows, dtype=F32), GRID_W)
    cols = jnp.tile(jnp.arange(GRID_W, dtype=F32), n_rows)
    n_freq = MLA_ROPE // 4
    inv_freq = ROPE_BASE ** (-jnp.arange(n_freq, dtype=F32) / n_freq)
    ar, ac = rows[:, None] * inv_freq, cols[:, None] * inv_freq
    cos = jnp.concatenate([jnp.cos(ar), jnp.cos(ar), jnp.cos(ac), jnp.cos(ac)], axis=1)
    sins = jnp.concatenate([-jnp.sin(ar), jnp.sin(ar), -jnp.sin(ac), jnp.sin(ac)], axis=1)
    cos = jnp.pad(cos, ((0, n_ctx), (0, 128 - MLA_ROPE)), constant_values=1.0)
    sins = jnp.pad(sins, ((0, n_ctx), (0, 128 - MLA_ROPE)))
    return cos, sins


def _mla_attn(qp, kp, v, n_lat):
    B, Lt, _ = v.shape
    tq = min(Q_TILE, n_lat)
    return pl.pallas_call(
        functools.partial(_attn_kernel, n_lat=n_lat, n_lat_tiles=n_lat // tq, sub=Q_SUB),
        grid=(B, MLA_HEADS, pl.cdiv(Lt, tq)),
        in_specs=[pl.BlockSpec((None, tq, MLA_PAD), lambda b, h, t: (b, t, h)),
                  pl.BlockSpec((None, Lt, MLA_PAD), lambda b, h, t: (b, 0, h)),
                  pl.BlockSpec((None, Lt, MLA_V), lambda b, h, t: (b, 0, h))],
        out_specs=pl.BlockSpec((None, tq, MLA_V), lambda b, h, t: (b, t, h)),
        out_shape=jax.ShapeDtypeStruct((B, Lt, GROUP_W), BF16),
        compiler_params=_cparams(("parallel", "parallel", "arbitrary"), 40), name="mla_attn",
    )(qp, kp, v)


def _wout_kernel(*refs, n_src, n_lat_tiles):
    x_refs, (mod_ref, a_ref, b_ref, m_ref, r_ref, w_ref, o_ref) = refs[:n_src], refs[n_src:]
    g, tm, d = o_ref.shape
    acc = None
    for k, ref in enumerate((a_ref, b_ref, m_ref, r_ref)):
        part = jnp.dot(ref[...].reshape(g * tm, GROUP_W), w_ref[k * GROUP_W:(k + 1) * GROUP_W, :],
                       preferred_element_type=F32)
        acc = part if acc is None else acc + part
    o_ref[...] = _tile_rows(x_refs, n_lat_tiles) + mod_ref[:, 2:3, :] * acc.reshape(g, tm, d)


def _wout(src, mod_l, a, b, m, r, w, layer, n_lat_tiles):
    B, _, D = src[0].shape
    Lt = sum(s.shape[1] for s in src)
    tm, g = TOKEN_TILE, ROW_BATCHES
    bm = lambda b, t: (b, t, 0)
    mix = pl.BlockSpec((g, tm, GROUP_W), bm)
    return pl.pallas_call(
        functools.partial(_wout_kernel, n_src=len(src), n_lat_tiles=n_lat_tiles), grid=(B // g, Lt // tm),
        in_specs=_residual_specs(src, tm, n_lat_tiles, g) + [
            pl.BlockSpec((g, 6, D), lambda b, t: (jnp.where(t >= n_lat_tiles, B // g, b), 0, 0)),
            mix, mix, mix, mix,
            pl.BlockSpec((None, 4 * GROUP_W, D), lambda b, t: (layer, 0, 0))],
        out_specs=pl.BlockSpec((g, tm, D), bm),
        out_shape=jax.ShapeDtypeStruct((B, Lt, D), F32),
        compiler_params=_cparams(("parallel", "parallel"), 52), name="out_proj",
    )(*src, mod_l, a, b, m, r, w)


def _mlp_kernel(x_ref, xc_ref, mod_ref, modc_ref, nw_ref, w1_ref, w2_ref, *rest, nf, rows, saxis):
    o_ref, h_ref, a_ref = rest[-3:]
    s = pl.program_id(saxis)

    def up(h, f, r):
        a = jnp.maximum(jnp.dot(h, w1_ref[...], preferred_element_type=F32), 0.0)
        a_ref[f, r, :] = (a * a).astype(BF16)

    @pl.when(s == 0)
    def _():
        x = x_ref[...].reshape(rows, x_ref.shape[-1])
        half = rows // 2
        for r in (slice(0, half), slice(half, rows)):
            h = _norm_mod(x[r], nw_ref[...], mod_ref[3:4, :], mod_ref[4:5, :])
            h_ref[r, :] = h
            up(h, 0, r)

    @pl.when(jnp.logical_and(s > 0, s < nf))
    def _():
        up(h_ref[...], s, slice(None))

    @pl.when(s >= nf)
    def _():
        y = jnp.dot(a_ref[0], w2_ref[0], preferred_element_type=F32)
        for f in range(1, nf):
            y += jnp.dot(a_ref[f], w2_ref[f], preferred_element_type=F32)
        o_ref[...] = xc_ref[...] + (modc_ref[5:6, :] * y).reshape(o_ref.shape)


def _mlp(X, mod_l, nw, w1, w2, layer, n_lat, with_ctx):
    B, Lt, D = X.shape
    n_ctx = Lt - n_lat
    ff = w1.shape[2]
    tm, tf, tn = MLP_ROWS, MLP_FF_TILE, MLP_OUT_TILE
    nf, nn = ff // tf, D // tn
    ns = nf + nn
    w2r = w2.reshape(w2.shape[0], nf, tf, D)
    nwr = nw.reshape(1, D)
    col = lambda s: jnp.maximum(s - nf, 0)
    scratch = [pltpu.VMEM((tm, D), BF16), pltpu.VMEM((nf, tm, tf), BF16)]
    lat = pl.pallas_call(
        functools.partial(_mlp_kernel, nf=nf, rows=tm, saxis=2), grid=(B, n_lat // tm, ns),
        in_specs=[pl.BlockSpec((None, tm, D), lambda b, t, s: (b, t, 0), pipeline_mode=pl.Buffered(1)),
                  pl.BlockSpec((None, tm, tn), lambda b, t, s: (b, t, col(s))),
                  pl.BlockSpec((None, 6, D), lambda b, t, s: (b, 0, 0)),
                  pl.BlockSpec((None, 6, tn), lambda b, t, s: (b, 0, col(s))),
                  pl.BlockSpec((1, D), lambda b, t, s: (0, 0)),
                  pl.BlockSpec((None, D, tf), lambda b, t, s: (layer, 0, jnp.minimum(s, nf - 1))),
                  pl.BlockSpec((None, nf, tf, tn), lambda b, t, s: (layer, 0, 0, col(s)))],
        out_specs=pl.BlockSpec((None, tm, tn), lambda b, t, s: (b, t, col(s))),
        out_shape=jax.ShapeDtypeStruct((B, Lt if with_ctx else n_lat, D), F32),
        scratch_shapes=scratch,
        compiler_params=_cparams(("parallel", "parallel", "arbitrary"), 58), name="mlp_lat",
    )(X, X, mod_l, mod_l, nwr, w1, w2r)
    if not with_ctx:
        return lat
    g = tm // n_ctx
    cblk = n_lat // n_ctx
    return pl.pallas_call(
        functools.partial(_mlp_kernel, nf=nf, rows=tm, saxis=1), grid=(B // g, ns),
        in_specs=[pl.BlockSpec((g, n_ctx, D), lambda i, s: (i, cblk, 0), pipeline_mode=pl.Buffered(1)),
                  pl.BlockSpec((g, n_ctx, tn), lambda i, s: (i, cblk, col(s))),
                  pl.BlockSpec((None, 6, D), lambda i, s: (B, 0, 0)),
                  pl.BlockSpec((None, 6, tn), lambda i, s: (B, 0, col(s))),
                  pl.BlockSpec((1, D), lambda i, s: (0, 0)),
                  pl.BlockSpec((None, D, tf), lambda i, s: (layer, 0, jnp.minimum(s, nf - 1))),
                  pl.BlockSpec((None, nf, tf, tn), lambda i, s: (layer, 0, 0, col(s))),
                  pl.BlockSpec(memory_space=pl.ANY)],
        out_specs=pl.BlockSpec((g, n_ctx, tn), lambda i, s: (i, cblk, col(s))),
        out_shape=jax.ShapeDtypeStruct((B, Lt, D), F32),
        input_output_aliases={7: 0},
        scratch_shapes=scratch,
        compiler_params=_cparams(("parallel", "arbitrary"), 58), name="mlp_ctx",
    )(X, X, mod_l, mod_l, nwr, w1, w2r, lat)


def _pack_w_in(w_in):
    offs = [0, 512, 1024, 1536, 2048, 2560, 2576, 2960, 3088, 3152, 3664, 4176]
    u, mq, mk, mv, mo, mg, cq, ckv, kr, lx, lg = [w_in[..., offs[i]:offs[i + 1]] for i in range(11)]
    depth, d = w_in.shape[:2]
    mg = jnp.transpose(mg.reshape(depth, d, 2, 2, ML_HEADS), (0, 1, 4, 3, 2))[:, :, :, ::-1, :].reshape(depth, d, 16)
    pad = jnp.zeros((depth, d, MISC_W - MLA_ROPE - 16), w_in.dtype)
    w_p = jnp.concatenate([u, mq, mk, mv, mo, cq, ckv, lx, lg, kr, mg, pad], axis=-1).astype(BF16)
    return w_p, jnp.transpose(mg, (0, 2, 1)).astype(BF16), jnp.transpose(mv, (0, 2, 1)).astype(BF16)


def _pack_mla(q_a_norm, w_q_up, kv_a_norm, w_kv_up, q_norm, k_norm):
    depth = w_q_up.shape[0]
    wq = w_q_up.reshape(depth, MLA_Q_LORA, MLA_HEADS, MLA_QK)
    wq = jnp.pad(wq, ((0, 0), (0, 0), (0, 0), (0, MLA_PAD - MLA_QK))).reshape(depth, MLA_Q_LORA, -1).astype(BF16)
    wkv = w_kv_up.reshape(depth, MLA_KV_LORA, MLA_HEADS, MLA_NOPE + MLA_V)
    wkv = jnp.concatenate([wkv[..., :MLA_NOPE].reshape(depth, MLA_KV_LORA, -1),
                           wkv[..., MLA_NOPE:].reshape(depth, MLA_KV_LORA, -1)], axis=-1).astype(BF16)
    row = lambda a: a.astype(F32)[:, None, :]
    head = lambda a: jnp.pad(a.astype(F32), ((0, 0), (0, MLA_PAD - MLA_QK)))[:, None, :]
    return row(q_a_norm), wq, row(kv_a_norm), wkv, head(q_norm), head(k_norm)


def _pack_lru(wa, ba, wx, bx):
    wab = jnp.concatenate([wa, wx], axis=-1).astype(BF16)
    bias = jnp.stack([ba, bx], axis=2).astype(F32)
    return wab, bias


def kernel(x, c, ctx, c_ctx, ada_w, ada_b, norm1_w, norm2_w, w_in, w_out, s5_lam_re, s5_lam_im, s5_log_dt, s5_b_re, s5_b_im, s5_c_re, s5_c_im, s5_d, s5_glu_w, s5_glu_b, ml_ig_bias, ml_fg_bias, ml_out_norm, mla_q_a_norm, mla_w_q_up, mla_kv_a_norm, mla_w_kv_up, mla_q_norm, mla_k_norm, lru_conv_w, lru_conv_b, lru_wa, lru_ba, lru_wx, lru_bx, lru_lam, mlp_w1, mlp_w2):
    B, L, D = x.shape
    Lc = ctx.shape[1]
    depth = ada_w.shape[0]
    n_lat_tiles = L // TOKEN_TILE
    src = (x.astype(F32), ctx.astype(F32))
    mod = _ada_all(c, c_ctx, ada_w, ada_b)
    w_p, wgt, wvt = _pack_w_in(w_in)
    mla_w = _pack_mla(mla_q_a_norm, mla_w_q_up, mla_kv_a_norm, mla_w_kv_up, mla_q_norm, mla_k_norm)
    wab, lru_bias = _pack_lru(lru_wa, lru_ba, lru_wx, lru_bx)
    tables = _rope_tables(L, Lc)
    w_out_b, glu_w_b = w_out.astype(BF16), s5_glu_w.astype(BF16)
    w1_b, w2_b = mlp_w1.astype(BF16), mlp_w2.astype(BF16)
    s5_ar, s5_ai, s5_bbr, s5_bbi = _s5_discretise(s5_lam_re, s5_lam_im, s5_log_dt, s5_b_re, s5_b_im)
    for l in range(depth):
        u, mq, mk, mv, mo, lx, lg, misc, mgt, qp, kp, v = _win(src, mod[l], norm1_w[l], w_p, wgt, wvt, mla_w, tables,
                                                               l, n_lat_tiles)
        s5p = _s5_params(s5_ar[l], s5_ai[l], s5_bbr[l], s5_bbi[l], s5_c_re[l], s5_c_im[l])
        a_mix, r_mix = _recurrent(u, lx, lg, s5p, s5_d[l], glu_w_b[l], s5_glu_b[l], lru_conv_w[l], lru_conv_b[l],
                                  wab[l], lru_bias[l], lru_lam[l], L)
        b_mix = _mlstm(mq, mk, mv, mo, misc, mgt, ml_ig_bias[l], ml_fg_bias[l], ml_out_norm[l], L)
        m_mix = _mla_attn(qp, kp, v, L)
        X = _wout(src, mod[l], a_mix, b_mix, m_mix, r_mix, w_out_b, l, n_lat_tiles)
        X = _mlp(X, mod[l], norm2_w[l], w1_b, w2_b, l, L, with_ctx=l < depth - 1)
        src = (X,)
    return X
```
